```python
import math
import jax
import jax.numpy as jnp
from jax import lax
import numpy as np

D_MODEL = 1024
BATCH = 4
SEQ = 4096
DEPTH = 2

N_MEM = 256
EPS = 1e-6
ROPE_THETA = 10000.0
Q_BLOCK = 128
NEG_BIG = -1e30
TINY = 1e-20

NSA_HEADS = 8
NSA_KV_GROUPS = 2
NSA_HPG = NSA_HEADS // NSA_KV_GROUPS
HEAD_DIM = 64
CMP_BLOCK = 32
CMP_STRIDE = 16
CMP_HIDDEN = 128
SEL_BLOCK = 64
SEL_TOPK = 16
WINDOW = 512
FORCE_SCORE = 1e6
NSA_Q_WIDTH = NSA_HEADS * HEAD_DIM
NSA_KV_WIDTH = NSA_KV_GROUPS * HEAD_DIM

HGRN_HEADS = 4
HGRN_EXPAND = 64
HGRN_VDIM = 64
HGRN_CHUNK = 64
HGRN_K_WIDTH = HGRN_HEADS * HGRN_EXPAND
HGRN_V_WIDTH = HGRN_HEADS * HGRN_VDIM

GDN_HEADS = 4
GDN_DK = 64
GDN_DV = 64
GDN_CONV = 4
GDN_CHUNK = 64
GDN_K_WIDTH = GDN_HEADS * GDN_DK
GDN_V_WIDTH = GDN_HEADS * GDN_DV

XATTN_HEADS = 4
XATTN_HEAD_DIM = 128
XATTN_WIDTH = XATTN_HEADS * XATTN_HEAD_DIM

D_FF = 2816
FFN_CONV = 3

IN_SIZES = (
    NSA_Q_WIDTH, NSA_KV_WIDTH, NSA_KV_WIDTH, NSA_KV_WIDTH, NSA_KV_WIDTH, NSA_KV_WIDTH, NSA_KV_WIDTH, 3 * NSA_HEADS,
    HGRN_K_WIDTH, HGRN_K_WIDTH, HGRN_V_WIDTH, HGRN_V_WIDTH,
    GDN_K_WIDTH, GDN_K_WIDTH, GDN_V_WIDTH, GDN_V_WIDTH, GDN_HEADS, GDN_HEADS,
    D_MODEL, D_MODEL, D_MODEL,
)
IN_WIDTH = sum(IN_SIZES)

kernel_name = 'hybrid_nsa_hgrn2_gdn_block'

F32 = jnp.float32


def _split(z, sizes):
    offs = [int(o) for o in np.cumsum(sizes)[:-1]]
    return jnp.split(z, offs, axis=-1)


def rmsnorm(x, g):
    xf = x.astype(F32)
    y = xf * lax.rsqrt(jnp.mean(xf * xf, axis=-1, keepdims=True) + EPS)
    return (y * g.astype(F32)).astype(x.dtype)


def l2norm(x):
    xf = x.astype(F32)
    return xf * lax.rsqrt(jnp.sum(xf * xf, axis=-1, keepdims=True) + EPS)


def masked_softmax(s, mask):
    s = jnp.where(mask, s, NEG_BIG)
    m = jnp.max(s, axis=-1, keepdims=True)
    e = jnp.where(mask, jnp.exp(s - m), 0.0)
    return e / jnp.maximum(jnp.sum(e, axis=-1, keepdims=True), 1e-30)


def masked_exp(d, mask):
    return jnp.where(mask, jnp.exp(jnp.where(mask, d, 0.0)), 0.0)


def rope_tables(seq_len):
    inv_freq = 1.0 / (ROPE_THETA ** (jnp.arange(0, HEAD_DIM, 2, dtype=F32) / HEAD_DIM))
    ang = jnp.arange(seq_len, dtype=F32)[:, None] * inv_freq[None, :]
    return jnp.cos(ang), jnp.sin(ang)


def apply_rope(x, cos, sin):
    half = x.shape[-1] // 2
    xf = x.astype(F32)
    x1, x2 = xf[..., :half], xf[..., half:]
    c = cos[None, :, None, :]
    s = sin[None, :, None, :]
    return jnp.concatenate([x1 * c - x2 * s, x2 * c + x1 * s], axis=-1).astype(x.dtype)


def causal_depthwise_conv(x, w):
    k = w.shape[0]
    return lax.conv_general_dilated(
        x, w[:, None, :].astype(x.dtype), window_strides=(1,), padding=((k - 1, 0),),
        dimension_numbers=('NWC', 'WIO', 'NWC'), feature_group_count=x.shape[-1])


def _to_chunks(t, c):
    b, s, h = t.shape[:3]
    t = t.reshape((b, s // c, c, h) + t.shape[3:])
    return t.transpose((1, 0, 3, 2) + tuple(range(4, t.ndim)))


def _from_chunks(t):
    n, b, h, c, d = t.shape
    return t.transpose(1, 0, 3, 2, 4).reshape(b, n * c, h, d)


def nsa_attention(q, k_cmp, v_cmp, k_sel, v_sel, k_win, v_win, gate_logits,
                  q_norm, k_norm, pos_k, pos_v, ck_w1, ck_w2, cv_w1, cv_w2, cos, sin):
    B, S, _ = q.shape
    G, HG, DH = NSA_KV_GROUPS, NSA_HPG, HEAD_DIM
    scale = DH ** -0.5
    q = apply_rope(rmsnorm(q.reshape(B, S, NSA_HEADS, DH), q_norm), cos, sin).reshape(B, S, G, HG, DH)

    def keys(k, g):
        return apply_rope(rmsnorm(k.reshape(B, S, G, DH), g), cos, sin)

    k_cmp, k_sel, k_win = keys(k_cmp, k_norm[0]), keys(k_sel, k_norm[1]), keys(k_win, k_norm[2])
    v_cmp, v_sel, v_win = (v.reshape(B, S, G, DH) for v in (v_cmp, v_sel, v_win))
    t_pos = jnp.arange(S)

    n_cmp = (S - CMP_BLOCK) // CMP_STRIDE + 1
    tok = jnp.arange(n_cmp)[:, None] * CMP_STRIDE + jnp.arange(CMP_BLOCK)[None, :]

    def compress(t, pos, w1, w2):
        blk = t[:, tok] + pos[None, None, :, None, :].astype(t.dtype)
        blk = blk.transpose(0, 1, 3, 2, 4).reshape(B, n_cmp, G, CMP_BLOCK * DH)
        return jax.nn.gelu(blk @ w1) @ w2

    kc = compress(k_cmp, pos_k, ck_w1, ck_w2)
    vc = compress(v_cmp, pos_v, cv_w1, cv_w2)
    cmp_mask = tok[:, -1][None, :] <= t_pos[:, None]
    s_cmp = jnp.einsum('bsghd,bngd->bghsn', q, kc).astype(F32) * scale
    p_cmp = masked_softmax(s_cmp, cmp_mask)
    o_cmp = jnp.einsum('bghsn,bngd->bsghd', p_cmp.astype(vc.dtype), vc)

    n_sel = S // SEL_BLOCK
    top = min(SEL_TOPK, n_sel)
    overlap = jax.nn.one_hot(tok // SEL_BLOCK, n_sel, dtype=F32).sum(axis=1)
    imp = jnp.einsum('bghsn,nj->bgsj', p_cmp, overlap)
    blk = jnp.arange(n_sel)[None, :]
    cur = (t_pos // SEL_BLOCK)[:, None]
    forced = (blk == 0) | (blk == cur) | (blk == cur - 1)
    valid = blk <= cur
    imp = jnp.where(valid, jnp.where(forced, FORCE_SCORE, imp), -1.0)
    _, sel_idx = lax.top_k(imp, top)

    n_qb = S // Q_BLOCK
    ks_tbl = k_sel.reshape(B, n_sel, SEL_BLOCK, G, DH).transpose(0, 3, 1, 2, 4)
    vs_tbl = v_sel.reshape(B, n_sel, SEL_BLOCK, G, DH).transpose(0, 3, 1, 2, 4)
    q_blocks = q.reshape(B, n_qb, Q_BLOCK, G, HG, DH).transpose(1, 0, 3, 2, 4, 5)
    idx_blocks = sel_idx.reshape(B, G, n_qb, Q_BLOCK, top).transpose(2, 0, 1, 3, 4)
    starts = jnp.arange(n_qb) * Q_BLOCK
    gather = jax.vmap(jax.vmap(lambda tbl, ix: tbl[ix]))

    def sel_block(args):
        qb, ib, start = args
        kg = gather(ks_tbl, ib).reshape(B, G, Q_BLOCK, top * SEL_BLOCK, DH)
        vg = gather(vs_tbl, ib).reshape(B, G, Q_BLOCK, top * SEL_BLOCK, DH)
        kpos = (ib[..., None] * SEL_BLOCK + jnp.arange(SEL_BLOCK)).reshape(B, G, Q_BLOCK, top * SEL_BLOCK)
        qpos = start + jnp.arange(Q_BLOCK)
        mask = (kpos <= qpos[None, None, :, None])[:, :, :, None, :]
        s = jnp.einsum('bgqhd,bgqnd->bgqhn', qb, kg).astype(F32) * scale
        p = masked_softmax(s, mask)
        return jnp.einsum('bgqhn,bgqnd->bgqhd', p.astype(vg.dtype), vg)

    o_sel = lax.map(sel_block, (q_blocks, idx_blocks, starts))
    o_sel = o_sel.transpose(1, 0, 3, 2, 4, 5).reshape(B, S, G, HG, DH)

    n_pre = WINDOW // Q_BLOCK
    band = (n_pre + 1) * Q_BLOCK

    def banded(t):
        tp = jnp.pad(t, ((0, 0), (WINDOW, 0), (0, 0), (0, 0))).reshape(B, n_qb + n_pre, Q_BLOCK, G, DH)
        return jnp.concatenate([tp[:, j:j + n_qb] for j in range(n_pre + 1)], axis=2)

    kw, vw = banded(k_win), banded(v_win)
    qw = q.reshape(B, n_qb, Q_BLOCK, G, HG, DH)
    qpos = starts[:, None] + jnp.arange(Q_BLOCK)[None, :]
    kpos = starts[:, None] - WINDOW + jnp.arange(band)[None, :]
    dist = qpos[:, :, None] - kpos[:, None, :]
    wmask = (dist >= 0) & (dist < WINDOW) & (kpos[:, None, :] >= 0)
    s_win = jnp.einsum('bnqghd,bnkgd->bnghqk', qw, kw).astype(F32) * scale
    p_win = masked_softmax(s_win, wmask[None, :, None, None])
    o_win = jnp.einsum('bnghqk,bnkgd->bnqghd', p_win.astype(vw.dtype), vw).reshape(B, S, G, HG, DH)

    g = jax.nn.sigmoid(gate_logits.astype(F32)).reshape(B, S, G, HG, 3)
    o = g[..., 0:1] * o_cmp + g[..., 1:2] * o_sel + g[..., 2:3] * o_win
    return o.reshape(B, S, NSA_Q_WIDTH).astype(q.dtype)


def gla_chunk_scan(q, k, v, log_f):
    B, S, H, DK = q.shape
    DV = v.shape[-1]
    C = HGRN_CHUNK
    xs = tuple(_to_chunks(t.astype(F32), C) for t in (q, k, v, log_f))
    causal = jnp.tril(jnp.ones((C, C), dtype=bool))[:, :, None]

    def step(state, inp):
        qc, kc, vc, gc = inp
        b = jnp.cumsum(gc, axis=2)
        o_inter = jnp.einsum('bhtd,bhde->bhte', qc * jnp.exp(b), state)
        decay = masked_exp(b[:, :, :, None, :] - b[:, :, None, :, :], causal)
        attn = jnp.einsum('bhtd,bhsd,bhtsd->bhts', qc, kc, decay)
        o_intra = jnp.einsum('bhts,bhse->bhte', attn, vc)
        b_last = b[:, :, -1:, :]
        state = jnp.exp(b_last[:, :, 0, :, None]) * state + jnp.einsum(
            'bhsd,bhse->bhde', kc * jnp.exp(b_last - b), vc)
        return state, o_inter + o_intra

    _, o = lax.scan(step, jnp.zeros((B, H, DK, DV), F32), xs)
    return _from_chunks(o)


def hgrn2_recurrence(q, f_logit, i, out_gate, lower_bound, out_norm):
    B, S, _ = q.shape
    H, DK, DV = HGRN_HEADS, HGRN_EXPAND, HGRN_VDIM
    qh = jax.nn.silu(q).reshape(B, S, H, DK)
    fz = f_logit.astype(F32).reshape(B, S, H, DK)
    lb = lower_bound.astype(F32).reshape(H, DK)
    f = lb + (1.0 - lb) * jax.nn.sigmoid(fz)
    log_f = jnp.log(jnp.maximum(f, TINY))
    k = (1.0 - lb) * jax.nn.sigmoid(-fz)
    v = i.reshape(B, S, H, DV)
    o = gla_chunk_scan(qh, k, v, log_f)
    o = rmsnorm(o, out_norm) * jax.nn.sigmoid(out_gate.astype(F32)).reshape(B, S, H, DV)
    return o.reshape(B, S, H * DV).astype(q.dtype)


def gated_delta_chunk_scan(q, k, v, beta, log_alpha):
    B, S, H, DK = q.shape
    DV = v.shape[-1]
    C = GDN_CHUNK
    xs = tuple(_to_chunks(t.astype(F32), C) for t in (q, k, v, beta, log_alpha))
    strict = jnp.tril(jnp.ones((C, C), dtype=bool), -1)
    incl = jnp.tril(jnp.ones((C, C), dtype=bool))
    eye = jnp.eye(C, dtype=F32)

    def step(state, inp):
        qc, kc, vc, bc, gc = inp
        b = jnp.cumsum(gc, axis=-1)
        diff = b[..., :, None] - b[..., None, :]
        dec_strict = masked_exp(diff, strict)
        dec_incl = masked_exp(diff, incl)
        kb = kc * bc[..., None]
        lower = eye + jnp.einsum('bhtd,bhsd->bhts', kb, kc) * dec_strict
        rhs = vc * bc[..., None] - jnp.einsum('bhtd,bhde->bhte', kb * jnp.exp(b)[..., None], state)
        v_new = lax.linalg.triangular_solve(lower, rhs, left_side=True, lower=True)
        o = jnp.einsum('bhtd,bhde->bhte', qc * jnp.exp(b)[..., None], state) + jnp.einsum(
            'bhts,bhse->bhte', jnp.einsum('bhtd,bhsd->bhts', qc, kc) * dec_incl, v_new)
        b_last = b[..., -1:]
        state = jnp.exp(b_last)[..., None] * state + jnp.einsum(
            'bhsd,bhse->bhde', kc * jnp.exp(b_last - b)[..., None], v_new)
        return state, o

    _, o = lax.scan(step, jnp.zeros((B, H, DK, DV), F32), xs)
    return _from_chunks(o)


def gated_deltanet(q, k, v, z, beta_logit, a_logit, conv_w, a_log, dt_bias, out_norm):
    B, S, _ = q.shape
    H, DK, DV = GDN_HEADS, GDN_DK, GDN_DV
    qkv = jax.nn.silu(causal_depthwise_conv(jnp.concatenate([q, k, v], axis=-1), conv_w))
    qc, kc, vc = jnp.split(qkv, [GDN_K_WIDTH, 2 * GDN_K_WIDTH], axis=-1)
    qc = l2norm(qc.reshape(B, S, H, DK)) * DK ** -0.5
    kc = l2norm(kc.reshape(B, S, H, DK))
    vc = vc.reshape(B, S, H, DV)
    beta = jax.nn.sigmoid(beta_logit.astype(F32))
    log_alpha = -jnp.exp(a_log.astype(F32)) * jax.nn.softplus(a_logit.astype(F32) + dt_bias.astype(F32))
    o = gated_delta_chunk_scan(qc, kc, vc, beta, log_alpha)
    o = rmsnorm(o, out_norm) * jax.nn.silu(z.astype(F32)).reshape(B, S, H, DV)
    return o.reshape(B, S, H * DV).astype(z.dtype)


def hybrid_mixer(h, w_in, nsa_q_norm, nsa_k_norm, cmp_pos_k, cmp_pos_v, cmp_k_w1, cmp_k_w2, cmp_v_w1, cmp_v_w2,
                 hgrn_lb, hgrn_out_norm, gdn_conv, gdn_a_log, gdn_dt_bias, gdn_out_norm,
                 w_branch_a, w_branch_b, w_branch_c, w_mix_out, cos, sin):
    (nq, nkc, nvc, nks, nvs, nkw, nvw, ngate,
     hq, hf, hi, hg,
     cq, ck, cv, cz, cb, ca,
     ma, mb, mc) = _split(h @ w_in, IN_SIZES)
    ya = nsa_attention(nq, nkc, nvc, nks, nvs, nkw, nvw, ngate, nsa_q_norm, nsa_k_norm,
                       cmp_pos_k, cmp_pos_v, cmp_k_w1, cmp_k_w2, cmp_v_w1, cmp_v_w2, cos, sin)
    yb = hgrn2_recurrence(hq, hf, hi, hg, hgrn_lb, hgrn_out_norm)
    yc = gated_deltanet(cq, ck, cv, cz, cb, ca, gdn_conv, gdn_a_log, gdn_dt_bias, gdn_out_norm)
    merged = (jax.nn.sigmoid(ma) * (ya @ w_branch_a)
              + jax.nn.sigmoid(mb) * (yb @ w_branch_b)
              + jax.nn.sigmoid(mc) * (yc @ w_branch_c))
    return merged @ w_mix_out


def memory_cross_attention(h, mem_k, mem_v, w_q, q_norm, k_norm, w_o):
    B, S, _ = h.shape
    q = rmsnorm((h @ w_q).reshape(B, S, XATTN_HEADS, XATTN_HEAD_DIM), q_norm)
    k = rmsnorm(mem_k, k_norm)
    s = jnp.einsum('bshd,bmhd->bhsm', q, k).astype(F32) * XATTN_HEAD_DIM ** -0.5
    p = jax.nn.softmax(s, axis=-1).astype(mem_v.dtype)
    o = jnp.einsum('bhsm,bmhd->bshd', p, mem_v).reshape(B, S, XATTN_WIDTH)
    return o @ w_o


def conv_glu_ffn(h, w_up, conv_w, w_down):
    u = causal_depthwise_conv(h @ w_up, conv_w)
    a, b = jnp.split(u, 2, axis=-1)
    return (jax.nn.silu(a) * b) @ w_down


def setup_inputs(seed: int = 0) -> dict:
    key = jax.random.key(seed)
    keys = jax.random.split(key, 33)
    L = DEPTH

    def nrm(i, shape, scale):
        return jax.random.normal(keys[i], shape, F32) * scale

    def gain(i, shape):
        return 1.0 + 0.02 * jax.random.normal(keys[i], shape, F32)

    u = jax.random.uniform(keys[18], (L, GDN_HEADS), F32)
    dt = jnp.exp(u * (math.log(0.1) - math.log(0.001)) + math.log(0.001))
    return {
        'x': nrm(0, (BATCH, SEQ, D_MODEL), 1.0),
        'mem': nrm(1, (BATCH, N_MEM, D_MODEL), 1.0),
        'mem_norm': gain(2, (D_MODEL,)),
        'mem_w_kv': nrm(3, (D_MODEL, 2 * XATTN_WIDTH), D_MODEL ** -0.5),
        'hgrn_lb_logits': nrm(4, (L, HGRN_K_WIDTH), 0.5),
        'norm_mix': gain(5, (L, D_MODEL)),
        'w_in': nrm(6, (L, D_MODEL, IN_WIDTH), D_MODEL ** -0.5),
        'nsa_q_norm': gain(7, (L, HEAD_DIM)),
        'nsa_k_norm': gain(8, (L, 3, HEAD_DIM)),
        'cmp_pos_k': nrm(9, (L, CMP_BLOCK, HEAD_DIM), 0.1),
        'cmp_pos_v': nrm(10, (L, CMP_BLOCK, HEAD_DIM), 0.1),
        'cmp_k_w1': nrm(11, (L, CMP_BLOCK * HEAD_DIM, CMP_HIDDEN), (CMP_BLOCK * HEAD_DIM) ** -0.5),
        'cmp_k_w2': nrm(12, (L, CMP_HIDDEN, HEAD_DIM), CMP_HIDDEN ** -0.5),
        'cmp_v_w1': nrm(13, (L, CMP_BLOCK * HEAD_DIM, CMP_HIDDEN), (CMP_BLOCK * HEAD_DIM) ** -0.5),
        'cmp_v_w2': nrm(14, (L, CMP_HIDDEN, HEAD_DIM), CMP_HIDDEN ** -0.5),
        'hgrn_out_norm': gain(15, (L, HGRN_VDIM)),
        'gdn_conv': nrm(16, (L, GDN_CONV, 2 * GDN_K_WIDTH + GDN_V_WIDTH), GDN_CONV ** -0.5),
        'gdn_a_log': jnp.log(jax.random.uniform(keys[17], (L, GDN_HEADS), F32, minval=1.0, maxval=16.0)),
        'gdn_dt_bias': dt + jnp.log(-jnp.expm1(-dt)),
        'gdn_out_norm': gain(19, (L, GDN_DV)),
        'w_branch_a': nrm(20, (L, NSA_Q_WIDTH, D_MODEL), NSA_Q_WIDTH ** -0.5),
        'w_branch_b': nrm(21, (L, HGRN_V_WIDTH, D_MODEL), HGRN_V_WIDTH ** -0.5),
        'w_branch_c': nrm(22, (L, GDN_V_WIDTH, D_MODEL), GDN_V_WIDTH ** -0.5),
        'w_mix_out': nrm(23, (L, D_MODEL, D_MODEL), D_MODEL ** -0.5),
        'norm_cross': gain(24, (L, D_MODEL)),
        'xattn_wq': nrm(25, (L, D_MODEL, XATTN_WIDTH), D_MODEL ** -0.5),
        'xattn_q_norm': gain(26, (L, XATTN_HEAD_DIM)),
        'xattn_k_norm': gain(27, (L, XATTN_HEAD_DIM)),
        'xattn_wo': nrm(28, (L, XATTN_WIDTH, D_MODEL), XATTN_WIDTH ** -0.5),
        'norm_ffn': gain(29, (L, D_MODEL)),
        'ffn_w_up': nrm(30, (L, D_MODEL, 2 * D_FF), D_MODEL ** -0.5),
        'ffn_conv': nrm(31, (L, FFN_CONV, 2 * D_FF), FFN_CONV ** -0.5),
        'ffn_w_down': nrm(32, (L, D_FF, D_MODEL), D_FF ** -0.5),
    }


def reference(x, mem, mem_norm, mem_w_kv, hgrn_lb_logits, norm_mix, w_in, nsa_q_norm, nsa_k_norm,
              cmp_pos_k, cmp_pos_v, cmp_k_w1, cmp_k_w2, cmp_v_w1, cmp_v_w2, hgrn_out_norm,
              gdn_conv, gdn_a_log, gdn_dt_bias, gdn_out_norm, w_branch_a, w_branch_b, w_branch_c,
              w_mix_out, norm_cross, xattn_wq, xattn_q_norm, xattn_k_norm, xattn_wo,
              norm_ffn, ffn_w_up, ffn_conv, ffn_w_down):
    B, S, _ = x.shape
    cos, sin = rope_tables(S)
    mkv = rmsnorm(mem, mem_norm) @ mem_w_kv
    mem_k, mem_v = jnp.split(mkv, 2, axis=-1)
    mem_k = mem_k.reshape(B, mem.shape[1], XATTN_HEADS, XATTN_HEAD_DIM)
    mem_v = mem_v.reshape(B, mem.shape[1], XATTN_HEADS, XATTN_HEAD_DIM)
    probs = jax.nn.softmax(hgrn_lb_logits.astype(F32), axis=0)
    lower_bounds = jnp.cumsum(probs, axis=0) - probs[0:1]
    for l in range(DEPTH):
        x = x + hybrid_mixer(rmsnorm(x, norm_mix[l]), w_in[l], nsa_q_norm[l], nsa_k_norm[l],
                             cmp_pos_k[l], cmp_pos_v[l], cmp_k_w1[l], cmp_k_w2[l], cmp_v_w1[l], cmp_v_w2[l],
                             lower_bounds[l], hgrn_out_norm[l], gdn_conv[l], gdn_a_log[l], gdn_dt_bias[l],
                             gdn_out_norm[l], w_branch_a[l], w_branch_b[l], w_branch_c[l], w_mix_out[l],
                             cos, sin)
        x = x + memory_cross_attention(rmsnorm(x, norm_cross[l]), mem_k, mem_v, xattn_wq[l],
                                       xattn_q_norm[l], xattn_k_norm[l], xattn_wo[l])
        x = x + conv_glu_ffn(rmsnorm(x, norm_ffn[l]), ffn_w_up[l], ffn_conv[l], ffn_w_down[l])
    return x
```

```python
import functools
import math

import numpy as np
import jax
import jax.numpy as jnp
from jax import lax
from jax.experimental import pallas as pl
from jax.experimental.pallas import tpu as pltpu

F32 = jnp.float32
BF16 = jnp.bfloat16

EPS = 1e-6
ROPE_THETA = 10000.0
NEG_BIG = -1e30
TINY = 1e-20
FORCE_SCORE = 1e6

D_MODEL = 1024
N_MEM = 256
HEAD_DIM = 64
NSA_HEADS = 8
NSA_GROUPS = 2
NSA_HPG = 4
CMP_BLOCK = 32
CMP_STRIDE = 16
CMP_HIDDEN = 128
SEL_BLOCK = 64
SEL_TOPK = 16
WINDOW = 512
N_HEADS4 = 4
CHUNK = 64
SUB = 16
XATTN_HEAD_DIM = 128
D_FF = 2816

COL_M = 0
COL_NQ = 3072
COL_KV = 3584
COL_HG = 4352
COL_GD = 5376
COL_NG = 6400
COL_BA = 6528
IN_PAD = 6656

VMEM_LIMIT = 48 * 1024 * 1024


def _cp(*sem):
    return pltpu.CompilerParams(dimension_semantics=sem, vmem_limit_bytes=VMEM_LIMIT)


def _dot(a, b):
    return jnp.dot(a.astype(BF16), b.astype(BF16), preferred_element_type=F32)


def _dot_nt(a, b):
    return lax.dot_general(a.astype(BF16), b.astype(BF16), (((1,), (1,)), ((), ())),
                           preferred_element_type=F32)


def _split(a, n):
    parts = []
    r = a
    for _ in range(n):
        p = r.astype(BF16)
        parts.append(p)
        r = r - p.astype(F32)
    return parts


def _dot_ls(a, b_exact, n=2):
    acc = None
    for p in _split(a, n):
        t = jnp.dot(p, b_exact, preferred_element_type=F32)
        acc = t if acc is None else acc + t
    return acc


def _dot_rs(a_exact, b, n=3):
    acc = None
    for p in _split(b, n):
        t = jnp.dot(a_exact, p, preferred_element_type=F32)
        acc = t if acc is None else acc + t
    return acc


def _dot3(a, b):
    ah, al = _split(a, 2)
    bh, bl = _split(b, 2)
    return (jnp.dot(ah, bh, preferred_element_type=F32) + jnp.dot(al, bh, preferred_element_type=F32)
            + jnp.dot(ah, bl, preferred_element_type=F32))


def _sigmoid(x):
    return 1.0 / (1.0 + jnp.exp(-x))


def _silu(x):
    return x * _sigmoid(x)


def _block_diag4(y, mask):
    return jnp.where(mask, jnp.concatenate([y, y, y, y], axis=0), 0.0)


def _bd_mask():
    r = lax.broadcasted_iota(jnp.int32, (256, 256), 0) // 64
    c = lax.broadcasted_iota(jnp.int32, (256, 256), 1) // 64
    return r == c


def _norm_matmul_kernel(x_ref, g_ref, w_ref, o_ref, h_ref):
    @pl.when(pl.program_id(1) == 0)
    def _():
        x = x_ref[...]
        ms = jnp.mean(x * x, axis=-1, keepdims=True)
        h_ref[...] = (x * lax.rsqrt(ms + EPS) * g_ref[...]).astype(BF16)

    o_ref[...] = jnp.dot(h_ref[...], w_ref[...], preferred_element_type=F32)


def _norm_matmul(x, g, w, tm, tn):
    m, d = x.shape
    n = w.shape[1]
    return pl.pallas_call(
        _norm_matmul_kernel,
        grid=(m // tm, n // tn),
        in_specs=[pl.BlockSpec((tm, d), lambda i, j: (i, 0)),
                  pl.BlockSpec((1, d), lambda i, j: (0, 0)),
                  pl.BlockSpec((d, tn), lambda i, j: (0, j))],
        out_specs=pl.BlockSpec((tm, tn), lambda i, j: (i, j)),
        out_shape=jax.ShapeDtypeStruct((m, n), F32),
        scratch_shapes=[pltpu.VMEM((tm, d), BF16)],
        compiler_params=_cp("parallel", "arbitrary"),
        name="norm_matmul",
    )(x, g.reshape(1, d), w)


def _head_norm(x, bd, g):
    ms = _dot_ls(x * x, bd, 2)
    return x * lax.rsqrt(ms + EPS) * g


def _rope(x, c, s):
    w = x.shape[1]
    lane = lax.broadcasted_iota(jnp.int32, x.shape, 1)
    sw = jnp.where((lane & 32) != 0, pltpu.roll(x, 32, 1), pltpu.roll(x, w - 32, 1))
    return x * c + sw * s


def _nsa_prep_kernel(q_ref, kvc_ref, kvs_ref, kvw_ref, cos_ref, sin_ref, qn_ref, kn_ref, bd512_ref, bd128_ref,
                     qo_ref, kc_ref, vc_ref, kst_ref, kwt_ref, vs_ref, vw_ref):
    c = cos_ref[...]
    s = sin_ref[...]
    c4 = jnp.concatenate([c, c, c, c], axis=1)
    s4 = jnp.concatenate([s, s, s, s], axis=1)
    q = _rope(_head_norm(q_ref[...], bd512_ref[...], qn_ref[...]), c4, s4)
    qo_ref[...] = (q * (HEAD_DIM ** -0.5)).astype(BF16)

    def key(ref, row):
        return _rope(_head_norm(ref[:, 0:128], bd128_ref[...], kn_ref[row:row + 1, :]), c, s)

    kc_ref[...] = key(kvc_ref, 0)
    vc_ref[...] = kvc_ref[:, 128:256]
    kst_ref[...] = key(kvs_ref, 1).T.astype(BF16)
    kwt_ref[...] = key(kvw_ref, 2).T.astype(BF16)
    vs = kvs_ref[:, 128:256].astype(BF16)
    vw = kvw_ref[:, 128:256].astype(BF16)
    for g in range(NSA_GROUPS):
        vs_ref[g] = vs[:, g * 64:(g + 1) * 64]
        vw_ref[g] = vw[:, g * 64:(g + 1) * 64]


def _nsa_prep(z, cos_t, sin_t, q_norm, k_norm, b, s, tm):
    m = b * s
    nt = s // tm
    bd512 = jnp.asarray(np.kron(np.eye(8), np.full((64, 64), 1.0 / 64)), BF16)
    bd128 = jnp.asarray(np.kron(np.eye(2), np.full((64, 64), 1.0 / 64)), BF16)
    qn = jnp.tile(q_norm, 8).reshape(1, 512)
    kn = jnp.tile(k_norm, (1, 2))
    row = lambda i: (i, 0)
    const = lambda i: (0, 0)
    return pl.pallas_call(
        _nsa_prep_kernel,
        grid=(m // tm,),
        in_specs=[pl.BlockSpec((tm, 512), lambda i: (i, COL_NQ // 512)),
                  pl.BlockSpec((tm, 256), lambda i: (i, COL_KV // 256)),
                  pl.BlockSpec((tm, 256), lambda i: (i, COL_KV // 256 + 1)),
                  pl.BlockSpec((tm, 256), lambda i: (i, COL_KV // 256 + 2)),
                  pl.BlockSpec((tm, 128), lambda i: (i % nt, 0)),
                  pl.BlockSpec((tm, 128), lambda i: (i % nt, 0)),
                  pl.BlockSpec((1, 512), const),
                  pl.BlockSpec((3, 128), const),
                  pl.BlockSpec((512, 512), const),
                  pl.BlockSpec((128, 128), const)],
        out_specs=[pl.BlockSpec((tm, 512), row),
                   pl.BlockSpec((tm, 128), row),
                   pl.BlockSpec((tm, 128), row),
                   pl.BlockSpec((None, 128, tm), lambda i: (i // nt, 0, i % nt)),
                   pl.BlockSpec((None, 128, tm), lambda i: (i // nt, 0, i % nt)),
                   pl.BlockSpec((NSA_GROUPS, tm, 64), lambda i: (0, i, 0)),
                   pl.BlockSpec((NSA_GROUPS, tm, 64), lambda i: (0, i, 0))],
        out_shape=[jax.ShapeDtypeStruct((m, 512), BF16),
                   jax.ShapeDtypeStruct((m, 128), F32),
                   jax.ShapeDtypeStruct((m, 128), F32),
                   jax.ShapeDtypeStruct((b, 128, s), BF16),
                   jax.ShapeDtypeStruct((b, 128, s), BF16),
                   jax.ShapeDtypeStruct((NSA_GROUPS, m, 64), BF16),
                   jax.ShapeDtypeStruct((NSA_GROUPS, m, 64), BF16)],
        compiler_params=_cp("parallel"),
        name="nsa_prep",
    )(z, z, z, z, cos_t, sin_t, qn, kn, bd512, bd128)


def _gelu_tanh(x):
    return 0.5 * x * (1.0 + jnp.tanh(math.sqrt(2.0 / math.pi) * (x + 0.044715 * (x * x * x))))


def _compress_kernel(rk_ref, rv_ref, pk_ref, pv_ref, wkt_ref, wkb_ref, wvt_ref, wvb_ref, wk2_ref, wv2_ref,
                     kct_ref, vc_ref):
    def mlp(r_ref, p_ref, wt_ref, wb_ref, w2_ref):
        r = r_ref[...]
        n = r.shape[0]
        top = _dot(r + p_ref[0:1, :], wt_ref[...])
        bot = _dot(r + p_ref[1:2, :], wb_ref[...])
        hid = top + pltpu.roll(bot, n - 1, 0)
        return _dot(_gelu_tanh(hid), w2_ref[...])

    kc = mlp(rk_ref, pk_ref, wkt_ref, wkb_ref, wk2_ref)
    vc = mlp(rv_ref, pv_ref, wvt_ref, wvb_ref, wv2_ref)
    kct_ref[...] = kc.T
    for g in range(NSA_GROUPS):
        vc_ref[g] = vc[:, g * 64:(g + 1) * 64]


def _compress_weights(pos, w1, w2):
    w1r = w1.reshape(2, 16, 64, CMP_HIDDEN)
    zero = jnp.zeros_like(w1r)
    big = jnp.stack([jnp.stack([w1r, zero], axis=-2), jnp.stack([zero, w1r], axis=-2)], axis=2)
    big = big.reshape(2, 16 * 2 * 64, 2 * CMP_HIDDEN).astype(BF16)
    posr = jnp.broadcast_to(pos.reshape(2, 16, 1, 64), (2, 16, 2, 64)).reshape(2, 2048)
    w2bd = jnp.zeros((2, CMP_HIDDEN, 2, 64), F32)
    w2bd = w2bd.at[0, :, 0, :].set(w2).at[1, :, 1, :].set(w2).reshape(2 * CMP_HIDDEN, 128).astype(BF16)
    return posr, big[0], big[1], w2bd


def _compress(rk, rv, pos_k, pos_v, ck_w1, ck_w2, cv_w1, cv_w2):
    b, n, _ = rk.shape
    pk, wkt, wkb, wk2 = _compress_weights(pos_k, ck_w1, ck_w2)
    pv, wvt, wvb, wv2 = _compress_weights(pos_v, cv_w1, cv_w2)
    c2 = lambda i: (0, 0)
    return pl.pallas_call(
        _compress_kernel,
        grid=(b,),
        in_specs=[pl.BlockSpec((None, n, 2048), lambda i: (i, 0, 0)),
                  pl.BlockSpec((None, n, 2048), lambda i: (i, 0, 0)),
                  pl.BlockSpec((2, 2048), c2), pl.BlockSpec((2, 2048), c2),
                  pl.BlockSpec((2048, 256), c2), pl.BlockSpec((2048, 256), c2),
                  pl.BlockSpec((2048, 256), c2), pl.BlockSpec((2048, 256), c2),
                  pl.BlockSpec((256, 128), c2), pl.BlockSpec((256, 128), c2)],
        out_specs=[pl.BlockSpec((None, 128, n), lambda i: (i, 0, 0)),
                   pl.BlockSpec((None, NSA_GROUPS, n, 64), lambda i: (i, 0, 0, 0))],
        out_shape=[jax.ShapeDtypeStruct((b, 128, n), F32),
                   jax.ShapeDtypeStruct((b, NSA_GROUPS, n, 64), F32)],
        compiler_params=_cp("parallel"),
        name="nsa_compress",
    )(rk, rv, pk, pv, wkt, wkb, wvt, wvb, wk2, wv2)


def _cmp_attn_kernel(q_ref, kct_ref, vc_ref, ovt_ref, o_ref, sel_ref, imp_ref, *, tq, ncp, nsel):
    i = pl.program_id(2)
    tpos = i * tq + lax.broadcasted_iota(jnp.int32, (tq, 1), 0)
    nblk = lax.broadcasted_iota(jnp.int32, (1, ncp), 1)
    mask = (nblk * CMP_STRIDE + (CMP_BLOCK - 1)) <= tpos
    kt = kct_ref[...].astype(BF16)
    v = vc_ref[...].astype(BF16)
    psum = jnp.zeros((tq, ncp), F32)
    for h in range(NSA_HPG):
        s = jnp.dot(q_ref[:, h * 64:(h + 1) * 64], kt, preferred_element_type=F32)
        s = jnp.where(mask, s, NEG_BIG)
        mx = jnp.max(s, axis=-1, keepdims=True)
        e = jnp.where(mask, jnp.exp(s - mx), 0.0)
        p = e / jnp.maximum(jnp.sum(e, axis=-1, keepdims=True), 1e-30)
        o_ref[:, h * 64:(h + 1) * 64] = jnp.dot(p.astype(BF16), v, preferred_element_type=F32)
        psum = psum + p
    ph, plo = _split(psum, 2)
    ovt = ovt_ref[...]
    imp = _dot_nt(ovt, ph) + _dot_nt(ovt, plo)
    j = lax.broadcasted_iota(jnp.int32, (nsel, 1), 0)
    cur = (i * tq + lax.broadcasted_iota(jnp.int32, (1, tq), 1)) // SEL_BLOCK
    forced = (j == 0) | (j == cur) | (j == cur - 1)
    val = jnp.where(j <= cur, jnp.where(forced, FORCE_SCORE, imp), -1.0)
    imp_ref[...] = val
    cnt = jnp.zeros((nsel, tq), F32)
    for r in range(nsel):
        vr = imp_ref[r:r + 1, :]
        beats = (vr > val) | ((vr == val) & (j > r))
        cnt = cnt + jnp.where(beats, 1.0, 0.0)
    sel = jnp.where(cnt < float(min(SEL_TOPK, nsel)), 1.0, 0.0)
    sel_ref[...] = sel.T.astype(BF16)


def _cmp_attn(q_r, kct, vc, b, s, tq):
    m = b * s
    nq = s // tq
    ncp = kct.shape[2]
    nsel = s // SEL_BLOCK
    n_cmp = (s - CMP_BLOCK) // CMP_STRIDE + 1
    tok = np.arange(n_cmp)[:, None] * CMP_STRIDE + np.arange(CMP_BLOCK)[None, :]
    ov = np.zeros((ncp, nsel), np.float32)
    for l in range(CMP_BLOCK):
        ov[np.arange(n_cmp), tok[:, l] // SEL_BLOCK] += 1.0
    ovt = jnp.asarray(ov.T, BF16)
    kern = functools.partial(_cmp_attn_kernel, tq=tq, ncp=ncp, nsel=nsel)
    return pl.pallas_call(
        kern,
        grid=(b, NSA_GROUPS, nq),
        in_specs=[pl.BlockSpec((tq, 256), lambda bb, g, i: (bb * nq + i, g)),
                  pl.BlockSpec((None, 64, ncp), lambda bb, g, i: (bb, g, 0)),
                  pl.BlockSpec((None, None, ncp, 64), lambda bb, g, i: (bb, g, 0, 0)),
                  pl.BlockSpec((nsel, ncp), lambda bb, g, i: (0, 0))],
        out_specs=[pl.BlockSpec((tq, 256), lambda bb, g, i: (bb * nq + i, g)),
                   pl.BlockSpec((None, tq, nsel), lambda bb, g, i: (g, bb * nq + i, 0))],
        out_shape=[jax.ShapeDtypeStruct((m, 512), F32),
                   jax.ShapeDtypeStruct((NSA_GROUPS, m, nsel), BF16)],
        scratch_shapes=[pltpu.VMEM((nsel, tq), F32)],
        compiler_params=_cp("parallel", "parallel", "parallel"),
        name="nsa_cmp_attn",
    )(q_r, kct, vc, ovt)


def _flash_kernel(*refs, mode, tq, tk, nk, nsel):
    if mode == "sel":
        q_ref, kt_ref, v_ref, sel_ref, o_ref, m_ref, l_ref, acc_ref = refs
    else:
        q_ref, kt_ref, v_ref, o_ref, m_ref, l_ref, acc_ref = refs
    qi = pl.program_id(2)
    kk = pl.program_id(3)
    if mode == "sel":
        ktile = kk
        active = kk <= qi
    else:
        ktile = qi - (nk - 1) + kk
        active = ktile >= 0

    @pl.when(kk == 0)
    def _():
        m_ref[...] = jnp.full(m_ref.shape, NEG_BIG, F32)
        l_ref[...] = jnp.zeros(l_ref.shape, F32)
        acc_ref[...] = jnp.zeros(acc_ref.shape, F32)

    @pl.when(active)
    def _():
        kpos = ktile * tk + lax.broadcasted_iota(jnp.int32, (1, tk), 1)
        qpos = qi * tq + lax.broadcasted_iota(jnp.int32, (tq, 1), 0)
        if mode == "sel":
            blk = lax.broadcasted_iota(jnp.int32, (nsel, 1), 0)
            expand = jnp.where(blk == kpos // SEL_BLOCK, 1.0, 0.0).astype(BF16)
            chosen = jnp.dot(sel_ref[...], expand, preferred_element_type=F32) > 0.5
            mask = chosen & (kpos <= qpos)
        else:
            dist = qpos - kpos
            mask = (dist >= 0) & (dist < WINDOW)
        kt = kt_ref[...]
        v = v_ref[...]
        for h in range(NSA_HPG):
            s = jnp.dot(q_ref[:, h * 64:(h + 1) * 64], kt, preferred_element_type=F32)
            s = jnp.where(mask, s, NEG_BIG)
            m_old = m_ref[h]
            m_new = jnp.maximum(m_old, jnp.max(s, axis=-1, keepdims=True))
            alpha = jnp.exp(m_old - m_new)
            p = jnp.where(mask, jnp.exp(s - m_new), 0.0)
            l_ref[h] = alpha * l_ref[h] + jnp.sum(p, axis=-1, keepdims=True)
            acc_ref[h] = alpha * acc_ref[h] + jnp.dot(p.astype(BF16), v, preferred_element_type=F32)
            m_ref[h] = m_new

    @pl.when(kk == nk - 1)
    def _():
        for h in range(NSA_HPG):
            o_ref[:, h * 64:(h + 1) * 64] = acc_ref[h] / jnp.maximum(l_ref[h], 1e-30)


def _flash(mode, q_r, kt, v, sel, b, s, tq, tk):
    m = b * s
    nq = s // tq
    nsel = s // SEL_BLOCK
    if mode == "sel":
        nk = s // tk
        kidx = lambda qi, kk: jnp.minimum(kk, qi)
    else:
        nk = WINDOW // tk + 1
        kidx = lambda qi, kk: jnp.maximum(qi - (nk - 1) + kk, 0)
    nkb = s // tk
    in_specs = [pl.BlockSpec((tq, 256), lambda bb, g, qi, kk: (bb * nq + qi, g)),
                pl.BlockSpec((None, 64, tk), lambda bb, g, qi, kk: (bb, g, kidx(qi, kk))),
                pl.BlockSpec((None, tk, 64), lambda bb, g, qi, kk: (g, bb * nkb + kidx(qi, kk), 0))]
    args = [q_r, kt, v]
    if mode == "sel":
        in_specs.append(pl.BlockSpec((None, tq, nsel), lambda bb, g, qi, kk: (g, bb * nq + qi, 0)))
        args.append(sel)
    kern = functools.partial(_flash_kernel, mode=mode, tq=tq, tk=tk, nk=nk, nsel=nsel)
    return pl.pallas_call(
        kern,
        grid=(b, NSA_GROUPS, nq, nk),
        in_specs=in_specs,
        out_specs=pl.BlockSpec((tq, 256), lambda bb, g, qi, kk: (bb * nq + qi, g)),
        out_shape=jax.ShapeDtypeStruct((m, 512), F32),
        scratch_shapes=[pltpu.VMEM((NSA_HPG, tq, 1), F32), pltpu.VMEM((NSA_HPG, tq, 1), F32),
                        pltpu.VMEM((NSA_HPG, tq, 64), F32)],
        compiler_params=_cp("parallel", "parallel", "parallel", "arbitrary"),
        name="nsa_" + mode + "_attn",
    )(*args)


def _consts_recurrent():
    ltri = np.kron(np.eye(4), np.tril(np.ones((CHUNK, CHUNK))))
    bd_mean = np.kron(np.eye(4), np.full((64, 64), 1.0 / 64))
    bd_ones = np.kron(np.eye(4), np.ones((64, 64)))
    return (jnp.asarray(ltri, BF16), jnp.asarray(bd_mean, BF16), jnp.asarray(bd_ones, BF16))


def _group_rmsnorm(o, bd_mean, g):
    ms = _dot_ls(o * o, bd_mean, 2)
    return o * lax.rsqrt(ms + EPS) * g


def _hgrn_kernel(q_ref, f_ref, i_ref, g_ref, lbl_ref, on_ref, ltri_ref, bdm_ref, selp_ref,
                 o_ref, st_ref, z_ref, *, layer, depth, t):
    @pl.when(pl.program_id(1) == 0)
    def _():
        st_ref[...] = jnp.zeros(st_ref.shape, F32)

    lg = lbl_ref[...]
    mx = jnp.max(lg, axis=0, keepdims=True)
    ex = jnp.exp(lg - mx)
    pr = ex / jnp.sum(ex, axis=0, keepdims=True)
    cs = pr[0:1, :]
    for r in range(1, layer + 1):
        cs = cs + pr[r:r + 1, :]
    lb = cs - pr[0:1, :]

    q = _silu(q_ref[...])
    fz = f_ref[...]
    f = lb + (1.0 - lb) * _sigmoid(fz)
    logf = jnp.log(jnp.maximum(f, TINY))
    k = (1.0 - lb) * _sigmoid(-fz)
    v = i_ref[...]
    b = _dot_rs(ltri_ref[...], logf, 3)

    bdm = _bd_mask()
    nsub = t // SUB
    tl = lax.broadcasted_iota(jnp.int32, (1, SUB, 1), 1)
    a_parts = []
    for hp in range(2):
        ls = slice(hp * 128, (hp + 1) * 128)
        q3 = q[:, ls].reshape(nsub, SUB, 128)
        k3 = k[:, ls].reshape(nsub, SUB, 128)
        b3 = b[:, ls].reshape(nsub, SUB, 128)
        for sl in range(SUB):
            msk = tl >= sl
            e = jnp.exp(jnp.where(msk, b3 - b3[:, sl:sl + 1, :], 0.0))
            zz = jnp.where(msk, q3 * k3[:, sl:sl + 1, :] * e, 0.0)
            z_ref[:, sl * 128:(sl + 1) * 128] = zz.reshape(t, 128)
        a_parts.append(_dot_ls(z_ref[...], selp_ref[...], 2))
    a_all = jnp.concatenate(a_parts, axis=1)
    trow = lax.broadcasted_iota(jnp.int32, (CHUNK, 256), 0)
    scol = lax.broadcasted_iota(jnp.int32, (CHUNK, 256), 1) % CHUNK
    diag_mask = (trow // SUB) == (scol // SUB)
    m1 = (trow >= 32) & (scol < 32)
    m2 = ((trow >= 16) & (trow < 32) & (scol < 16)) | ((trow >= 48) & (scol >= 32) & (scol < 48))
    rowi = lax.broadcasted_iota(jnp.int32, (CHUNK, 1), 0)

    for c in range(t // CHUNK):
        rs = slice(c * CHUNK, (c + 1) * CHUNK)
        qc, kc, vc, bc = q[rs], k[rs], v[rs], b[rs]
        r1 = bc[31:32, :]
        q1 = qc * jnp.exp(jnp.minimum(bc - r1, 0.0))
        k1 = jnp.where(rowi < 32, kc * jnp.exp(jnp.minimum(r1 - bc, 0.0)), 0.0)
        r2 = jnp.where(rowi < 32, bc[15:16, :], bc[47:48, :])
        q2 = qc * jnp.exp(jnp.minimum(bc - r2, 0.0))
        k2 = kc * jnp.exp(jnp.minimum(r2 - bc, 0.0))
        a_off = jnp.where(m1, _dot_nt(q1, _block_diag4(k1, bdm)), 0.0) \
            + jnp.where(m2, _dot_nt(q2, _block_diag4(k2, bdm)), 0.0)
        attn = jnp.where(diag_mask, a_all[rs], 0.0) + a_off
        st = st_ref[...]
        o = _dot_nt(qc * jnp.exp(bc), st) + _dot(attn, _block_diag4(vc, bdm))
        bl = bc[CHUNK - 1:CHUNK, :]
        kd = kc * jnp.exp(bl - bc)
        st_ref[...] = st * jnp.exp(bl) + jnp.where(bdm, _dot(vc.T, kd), 0.0)
        y = _group_rmsnorm(o, bdm_ref[...], on_ref[...]) * _sigmoid(g_ref[rs, :])
        o_ref[rs, :] = y.astype(BF16)


def _hgrn(z, lb_logits, out_norm, layer, b, s, t):
    m = b * s
    nt = s // t
    depth = lb_logits.shape[0]
    ltri, bd_mean, _ = _consts_recurrent()
    selp = np.zeros((SUB, 2, 64, 2, CHUNK), np.float32)
    for sl in range(SUB):
        for h2 in range(2):
            selp[sl, h2, :, h2, sl::SUB] = 1.0
    selp = jnp.asarray(selp.reshape(SUB * 128, 128), BF16)
    col = COL_HG // 256
    c2 = lambda bb, i: (0, 0)
    kern = functools.partial(_hgrn_kernel, layer=layer, depth=depth, t=t)
    return pl.pallas_call(
        kern,
        grid=(b, nt),
        in_specs=[pl.BlockSpec((t, 256), lambda bb, i: (bb * nt + i, col)),
                  pl.BlockSpec((t, 256), lambda bb, i: (bb * nt + i, col + 1)),
                  pl.BlockSpec((t, 256), lambda bb, i: (bb * nt + i, col + 2)),
                  pl.BlockSpec((t, 256), lambda bb, i: (bb * nt + i, col + 3)),
                  pl.BlockSpec((depth, 256), c2),
                  pl.BlockSpec((1, 256), c2),
                  pl.BlockSpec((256, 256), c2),
                  pl.BlockSpec((256, 256), c2),
                  pl.BlockSpec((SUB * 128, 128), c2)],
        out_specs=pl.BlockSpec((t, 256), lambda bb, i: (bb * nt + i, 0)),
        out_shape=jax.ShapeDtypeStruct((m, 256), BF16),
        scratch_shapes=[pltpu.VMEM((256, 256), F32), pltpu.VMEM((t, SUB * 128), F32)],
        compiler_params=_cp("parallel", "arbitrary"),
        name="hgrn2",
    )(z, z, z, z, lb_logits, jnp.tile(out_norm, 4).reshape(1, 256), ltri, bd_mean, selp)


def _gdn_kernel(q_ref, k_ref, v_ref, z_ref, ba_ref, cw_ref, al_ref, dt_ref, on_ref, ltri_ref, bdm_ref, bdo_ref,
                eb_ref, ea_ref, o_ref, st_ref, prev_ref, *, t):
    @pl.when(pl.program_id(1) == 0)
    def _():
        st_ref[...] = jnp.zeros(st_ref.shape, F32)
        prev_ref[...] = jnp.zeros(prev_ref.shape, F32)

    def conv_silu(x_ref, p):
        x = x_ref[...]
        xp = jnp.concatenate([prev_ref[:, p * 256:(p + 1) * 256], x], axis=0)
        w = cw_ref[:, p * 256:(p + 1) * 256]
        y = w[3:4, :] * xp[8:, :]
        for j in range(1, 4):
            y = y + w[3 - j:4 - j, :] * pltpu.roll(xp, j, 0)[8:, :]
        return _silu(y), x[t - 8:, :]

    qa, qtail = conv_silu(q_ref, 0)
    ka, ktail = conv_silu(k_ref, 1)
    va, vtail = conv_silu(v_ref, 2)
    prev_ref[:, 0:256] = qtail
    prev_ref[:, 256:512] = ktail
    prev_ref[:, 512:768] = vtail

    bdo = bdo_ref[...]
    q = qa * lax.rsqrt(_dot_ls(qa * qa, bdo, 2) + EPS) * (HEAD_DIM ** -0.5)
    k = ka * lax.rsqrt(_dot_ls(ka * ka, bdo, 2) + EPS)
    v = va

    gl = ba_ref[...]
    beta = _dot_ls(_sigmoid(gl), eb_ref[...], 2)
    xs = gl + dt_ref[...]
    softplus = jnp.maximum(xs, 0.0) + jnp.log(1.0 + jnp.exp(-jnp.abs(xs)))
    la = _dot_ls(-jnp.exp(al_ref[...]) * softplus, ea_ref[...], 3)
    b = _dot_rs(ltri_ref[...], la, 3)

    bdm = _bd_mask()
    trow = lax.broadcasted_iota(jnp.int32, (CHUNK, 256), 0)
    scol = lax.broadcasted_iota(jnp.int32, (CHUNK, 256), 1) % CHUNK
    incl = scol <= trow
    strict = scol < trow
    eye = jnp.where(scol == trow, 1.0, 0.0)

    def mm3(x, y):
        return _dot3(x, _block_diag4(y, bdm))

    for c in range(t // CHUNK):
        rs = slice(c * CHUNK, (c + 1) * CHUNK)
        qc, kc, vc, bc, btc = q[rs], k[rs], v[rs], b[rs], beta[rs]
        brow = jnp.concatenate([bc[:, h * 64:(h + 1) * 64].T for h in range(N_HEADS4)], axis=1)
        e = jnp.exp(jnp.where(incl, bc - brow, 0.0))
        kb = kc * btc
        gram = _dot_nt(jnp.concatenate([kb, qc], axis=0), _block_diag4(kc, bdm))
        aw = jnp.where(strict, gram[:CHUNK] * e, 0.0)
        aq = jnp.where(incl, gram[CHUNK:] * e, 0.0)
        tinv = eye - aw
        pw = aw
        for _ in range(5):
            pw = mm3(pw, pw)
            tinv = tinv + mm3(tinv, pw)
        eb = jnp.exp(bc)
        u = mm3(tinv, vc * btc)
        w = mm3(tinv, kb * eb)
        st = st_ref[...]
        vnew = u - _dot_nt(w, st)
        o = _dot_nt(qc * eb, st) + _dot(aq, _block_diag4(vnew, bdm))
        bl = bc[CHUNK - 1:CHUNK, :]
        kd = kc * jnp.exp(bl - bc)
        st_ref[...] = st * jnp.exp(bl) + jnp.where(bdm, _dot(vnew.T, kd), 0.0)
        y = _group_rmsnorm(o, bdm_ref[...], on_ref[...]) * _silu(z_ref[rs, :])
        o_ref[rs, :] = y.astype(BF16)


def _gdn(z, conv_w, a_log, dt_bias, out_norm, b, s, t):
    m = b * s
    nt = s // t
    ltri, bd_mean, bd_ones = _consts_recurrent()
    eb = np.zeros((128, 256), np.float32)
    ea = np.zeros((128, 256), np.float32)
    for h in range(N_HEADS4):
        eb[h, h * 64:(h + 1) * 64] = 1.0
        ea[4 + h, h * 64:(h + 1) * 64] = 1.0
    al = jnp.zeros((1, 128), F32).at[0, 4:8].set(a_log)
    dt = jnp.zeros((1, 128), F32).at[0, 4:8].set(dt_bias)
    col = COL_GD // 256
    c2 = lambda bb, i: (0, 0)
    kern = functools.partial(_gdn_kernel, t=t)
    return pl.pallas_call(
        kern,
        grid=(b, nt),
        in_specs=[pl.BlockSpec((t, 256), lambda bb, i: (bb * nt + i, col)),
                  pl.BlockSpec((t, 256), lambda bb, i: (bb * nt + i, col + 1)),
                  pl.BlockSpec((t, 256), lambda bb, i: (bb * nt + i, col + 2)),
                  pl.BlockSpec((t, 256), lambda bb, i: (bb * nt + i, col + 3)),
                  pl.BlockSpec((t, 128), lambda bb, i: (bb * nt + i, COL_BA // 128)),
                  pl.BlockSpec((4, 768), c2),
                  pl.BlockSpec((1, 128), c2), pl.BlockSpec((1, 128), c2),
                  pl.BlockSpec((1, 256), c2),
                  pl.BlockSpec((256, 256), c2), pl.BlockSpec((256, 256), c2), pl.BlockSpec((256, 256), c2),
                  pl.BlockSpec((128, 256), c2), pl.BlockSpec((128, 256), c2)],
        out_specs=pl.BlockSpec((t, 256), lambda bb, i: (bb * nt + i, 0)),
        out_shape=jax.ShapeDtypeStruct((m, 256), BF16),
        scratch_shapes=[pltpu.VMEM((256, 256), F32), pltpu.VMEM((8, 768), F32)],
        compiler_params=_cp("parallel", "arbitrary"),
        name="gdn",
    )(z, z, z, z, z, conv_w, al, dt, jnp.tile(out_norm, 4).reshape(1, 256), ltri, bd_mean, bd_ones,
      jnp.asarray(eb, BF16), jnp.asarray(ea, BF16))


def _merge_kernel(x_ref, oc_ref, os_ref, ow_ref, ng_ref, yb_ref, yc_ref, ma_ref, mb_ref, mc_ref,
                  ex_ref, wa_ref, wb_ref, wc_ref, wo_ref, o_ref):
    sg = _sigmoid(ng_ref[...])
    ya = (_dot_ls(sg, ex_ref[0], 2) * oc_ref[...] + _dot_ls(sg, ex_ref[1], 2) * os_ref[...]
          + _dot_ls(sg, ex_ref[2], 2) * ow_ref[...])
    merged = (_sigmoid(ma_ref[...]) * _dot(ya, wa_ref[...])
              + _sigmoid(mb_ref[...]) * jnp.dot(yb_ref[...], wb_ref[...], preferred_element_type=F32)
              + _sigmoid(mc_ref[...]) * jnp.dot(yc_ref[...], wc_ref[...], preferred_element_type=F32))
    o_ref[...] = x_ref[...] + _dot(merged, wo_ref[...])


def _merge(x2, o_cmp, o_sel, o_win, z, yb, yc, wa, wb, wc, wo, tm):
    m, d = x2.shape
    ex = np.zeros((3, 128, 512), np.float32)
    for c in range(3):
        for h in range(NSA_HEADS):
            ex[c, c * NSA_HEADS + h, h * 64:(h + 1) * 64] = 1.0
    row = lambda i: (i, 0)
    c2 = lambda i: (0, 0)
    return pl.pallas_call(
        _merge_kernel,
        grid=(m // tm,),
        in_specs=[pl.BlockSpec((tm, d), row),
                  pl.BlockSpec((tm, 512), row), pl.BlockSpec((tm, 512), row), pl.BlockSpec((tm, 512), row),
                  pl.BlockSpec((tm, 128), lambda i: (i, COL_NG // 128)),
                  pl.BlockSpec((tm, 256), row), pl.BlockSpec((tm, 256), row),
                  pl.BlockSpec((tm, 1024), lambda i: (i, 0)),
                  pl.BlockSpec((tm, 1024), lambda i: (i, 1)),
                  pl.BlockSpec((tm, 1024), lambda i: (i, 2)),
                  pl.BlockSpec((3, 128, 512), lambda i: (0, 0, 0)),
                  pl.BlockSpec((512, d), c2), pl.BlockSpec((256, d), c2), pl.BlockSpec((256, d), c2),
                  pl.BlockSpec((d, d), c2)],
        out_specs=pl.BlockSpec((tm, d), row),
        out_shape=jax.ShapeDtypeStruct((m, d), F32),
        compiler_params=_cp("parallel"),
        name="merge_out",
    )(x2, o_cmp, o_sel, o_win, z, yb, yc, z, z, z, jnp.asarray(ex, BF16), wa, wb, wc, wo)


def _xattn_kernel(x_ref, g_ref, wq_ref, qn_ref, kn_ref, mkv_ref, wo_ref, o_ref):
    x = x_ref[...]
    h = x * lax.rsqrt(jnp.mean(x * x, axis=-1, keepdims=True) + EPS) * g_ref[...]
    q = _dot(h, wq_ref[...])
    outs = []
    for hd in range(N_HEADS4):
        ls = slice(hd * 128, (hd + 1) * 128)
        qh = q[:, ls]
        qh = qh * lax.rsqrt(jnp.mean(qh * qh, axis=-1, keepdims=True) + EPS) * qn_ref[...]
        kh = mkv_ref[:, ls]
        kh = kh * lax.rsqrt(jnp.mean(kh * kh, axis=-1, keepdims=True) + EPS) * kn_ref[...]
        s = _dot_nt(qh, kh) * (XATTN_HEAD_DIM ** -0.5)
        e = jnp.exp(s - jnp.max(s, axis=-1, keepdims=True))
        p = e / jnp.sum(e, axis=-1, keepdims=True)
        outs.append(_dot(p, mkv_ref[:, 512 + hd * 128:512 + (hd + 1) * 128]))
    o = jnp.concatenate(outs, axis=1)
    o_ref[...] = x + _dot(o, wo_ref[...])


def _xattn(x2, g, wq, qn, kn, mkv, wo, b, s, tm):
    m, d = x2.shape
    nt = s // tm
    c2 = lambda i: (0, 0)
    return pl.pallas_call(
        _xattn_kernel,
        grid=(m // tm,),
        in_specs=[pl.BlockSpec((tm, d), lambda i: (i, 0)),
                  pl.BlockSpec((1, d), c2),
                  pl.BlockSpec((d, 512), c2),
                  pl.BlockSpec((1, 128), c2), pl.BlockSpec((1, 128), c2),
                  pl.BlockSpec((N_MEM, 1024), lambda i: (i // nt, 0)),
                  pl.BlockSpec((512, d), c2)],
        out_specs=pl.BlockSpec((tm, d), lambda i: (i, 0)),
        out_shape=jax.ShapeDtypeStruct((m, d), F32),
        compiler_params=_cp("parallel"),
        name="mem_xattn",
    )(x2, g.reshape(1, d), wq, qn.reshape(1, 128), kn.reshape(1, 128), mkv, wo)


FFN_HALO = 16


def _ffn_kernel(x_ref, xh_ref, g_ref, wua_ref, wub_ref, cwa_ref, cwb_ref, wd_ref, o_ref, h_ref, acc_ref, *, nt, tm):
    i = pl.program_id(0)
    f = pl.program_id(1)

    @pl.when(f == 0)
    def _():
        def norm(x):
            return x * lax.rsqrt(jnp.mean(x * x, axis=-1, keepdims=True) + EPS) * g_ref[...]
        first = (i % nt) == 0
        h_ref[0:FFN_HALO, :] = jnp.where(first, 0.0, norm(xh_ref[...])).astype(BF16)
        h_ref[FFN_HALO:, :] = norm(x_ref[...]).astype(BF16)
        acc_ref[...] = jnp.zeros(acc_ref.shape, F32)

    h = h_ref[...]

    def up_conv(w_ref, cw_ref):
        u = jnp.dot(h, w_ref[...], preferred_element_type=F32)
        cw = cw_ref[...]
        y = cw[2:3, :] * u + cw[1:2, :] * pltpu.roll(u, 1, 0) + cw[0:1, :] * pltpu.roll(u, 2, 0)
        return y[FFN_HALO:, :]

    a = up_conv(wua_ref, cwa_ref)
    bb = up_conv(wub_ref, cwb_ref)
    acc_ref[...] += _dot(_silu(a) * bb, wd_ref[...])

    @pl.when(f == pl.num_programs(1) - 1)
    def _():
        o_ref[...] = x_ref[...] + acc_ref[...]


def _ffn(x2, g, w_up, conv_w, w_down, b, s, tm, tf):
    m, d = x2.shape
    nt = s // tm
    nf = D_FF // tf
    hb = tm // FFN_HALO
    kern = functools.partial(_ffn_kernel, nt=nt, tm=tm)
    return pl.pallas_call(
        kern,
        grid=(m // tm, nf),
        in_specs=[pl.BlockSpec((tm, d), lambda i, f: (i, 0)),
                  pl.BlockSpec((FFN_HALO, d), lambda i, f: (jnp.maximum(i * hb - 1, 0), 0)),
                  pl.BlockSpec((1, d), lambda i, f: (0, 0)),
                  pl.BlockSpec((d, tf), lambda i, f: (0, f)),
                  pl.BlockSpec((d, tf), lambda i, f: (0, nf + f)),
                  pl.BlockSpec((3, tf), lambda i, f: (0, f)),
                  pl.BlockSpec((3, tf), lambda i, f: (0, nf + f)),
                  pl.BlockSpec((tf, d), lambda i, f: (f, 0))],
        out_specs=pl.BlockSpec((tm, d), lambda i, f: (i, 0)),
        out_shape=jax.ShapeDtypeStruct((m, d), F32),
        scratch_shapes=[pltpu.VMEM((tm + FFN_HALO, d), BF16), pltpu.VMEM((tm, d), F32)],
        compiler_params=_cp("parallel", "arbitrary"),
        name="conv_glu_ffn",
    )(x2, x2, g.reshape(1, d), w_up, w_up, conv_w, conv_w, w_down)


def _permute_w_in(w):
    d = w.shape[0]
    gate = w[:, 1280:1304].reshape(d, NSA_HEADS, 3).transpose(0, 2, 1).reshape(d, 24)
    z104 = jnp.zeros((d, 104), w.dtype)
    z120 = jnp.zeros((d, 120), w.dtype)
    return jnp.concatenate([w[:, 3360:6432], w[:, 0:1280], w[:, 1304:3352], gate, z104, w[:, 3352:3360], z120],
                           axis=1).astype(BF16)


def _rope_tables(s):
    inv_freq = 1.0 / (ROPE_THETA ** (jnp.arange(0, HEAD_DIM, 2, dtype=F32) / HEAD_DIM))
    ang = jnp.arange(s, dtype=F32)[:, None] * inv_freq[None, :]
    c, sn = jnp.cos(ang), jnp.sin(ang)
    return jnp.tile(jnp.concatenate([c, c], axis=1), (1, 2)), jnp.tile(jnp.concatenate([-sn, sn], axis=1), (1, 2))


def kernel(x, mem, mem_norm, mem_w_kv, hgrn_lb_logits, norm_mix, w_in, nsa_q_norm, nsa_k_norm, cmp_pos_k, cmp_pos_v, cmp_k_w1, cmp_k_w2, cmp_v_w1, cmp_v_w2, hgrn_out_norm, gdn_conv, gdn_a_log, gdn_dt_bias, gdn_out_norm, w_branch_a, w_branch_b, w_branch_c, w_mix_out, norm_cross, xattn_wq, xattn_q_norm, xattn_k_norm, xattn_wo, norm_ffn, ffn_w_up, ffn_conv, ffn_w_down):
    b, s, d = x.shape
    m = b * s
    depth = w_in.shape[0]
    cos_t, sin_t = _rope_tables(s)
    x2 = x.reshape(m, d)
    mkv = _norm_matmul(mem.reshape(b * N_MEM, d), mem_norm, mem_w_kv.astype(BF16), N_MEM, 512)

    tm_in = min(1024, m)
    for l in range(depth):
        z = _norm_matmul(x2, norm_mix[l], _permute_w_in(w_in[l]), tm_in, 512)
        q_r, kcmp, vcmp, kst, kwt, vs, vw = _nsa_prep(z, cos_t, sin_t, nsa_q_norm[l], nsa_k_norm[l], b, s, 512)
        rk = kcmp.reshape(b, s // CMP_STRIDE, CMP_STRIDE * 128)
        rv = vcmp.reshape(b, s // CMP_STRIDE, CMP_STRIDE * 128)
        kct, vc = _compress(rk, rv, cmp_pos_k[l], cmp_pos_v[l], cmp_k_w1[l], cmp_k_w2[l], cmp_v_w1[l], cmp_v_w2[l])
        o_cmp, sel = _cmp_attn(q_r, kct, vc, b, s, 256)
        o_sel = _flash("sel", q_r, kst, vs, sel, b, s, 256, 256)
        o_win = _flash("win", q_r, kwt, vw, None, b, s, 256, 256)
        yb = _hgrn(z, hgrn_lb_logits, hgrn_out_norm[l], l, b, s, 256)
        yc = _gdn(z, gdn_conv[l], gdn_a_log[l], gdn_dt_bias[l], gdn_out_norm[l], b, s, 256)
        x2 = _merge(x2, o_cmp, o_sel, o_win, z, yb, yc, w_branch_a[l].astype(BF16), w_branch_b[l].astype(BF16),
                    w_branch_c[l].astype(BF16), w_mix_out[l].astype(BF16), 256)
        x2 = _xattn(x2, norm_cross[l], xattn_wq[l].astype(BF16), xattn_q_norm[l], xattn_k_norm[l], mkv,
                    xattn_wo[l].astype(BF16), b, s, 512)
        x2 = _ffn(x2, norm_ffn[l], ffn_w_up[l].astype(BF16), ffn_conv[l], ffn_w_down[l].astype(BF16), b, s, 512, 256)
    return x2.reshape(b, s, d)
```

```python
import functools
import math

import numpy as np
import jax
import jax.numpy as jnp
from jax import lax
from jax.experimental import pallas as pl
from jax.experimental.pallas import tpu as pltpu

F32 = jnp.float32
BF16 = jnp.bfloat16

EPS = 1e-6
ROPE_THETA = 10000.0
NEG_BIG = -1e30
TINY = 1e-20
FORCE_SCORE = 1e6

D_MODEL = 1024
N_MEM = 256
HEAD_DIM = 64
NSA_HEADS = 8
NSA_GROUPS = 2
NSA_HPG = 4
CMP_BLOCK = 32
CMP_STRIDE = 16
CMP_HIDDEN = 128
SEL_BLOCK = 64
SEL_TOPK = 16
WINDOW = 512
N_HEADS4 = 4
CHUNK = 64
SUB = 16
XATTN_HEAD_DIM = 128
D_FF = 2816

COL_M = 0
COL_NQ = 3072
COL_KV = 3584
COL_HG = 4352
COL_GD = 5376
COL_NG = 6400
COL_BA = 6528
IN_PAD = 6656

VMEM_LIMIT = 48 * 1024 * 1024


def _cp(*sem):
    return pltpu.CompilerParams(dimension_semantics=sem, vmem_limit_bytes=VMEM_LIMIT)


def _dot(a, b):
    return jnp.dot(a.astype(BF16), b.astype(BF16), preferred_element_type=F32)


def _dot_nt(a, b):
    return lax.dot_general(a.astype(BF16), b.astype(BF16), (((1,), (1,)), ((), ())),
                           preferred_element_type=F32)


def _split(a, n):
    parts = []
    r = a
    for _ in range(n):
        p = r.astype(BF16)
        parts.append(p)
        r = r - p.astype(F32)
    return parts


def _dot_ls(a, b_exact, n=2):
    acc = None
    for p in _split(a, n):
        t = jnp.dot(p, b_exact, preferred_element_type=F32)
        acc = t if acc is None else acc + t
    return acc


def _dot_rs(a_exact, b, n=3):
    acc = None
    for p in _split(b, n):
        t = jnp.dot(a_exact, p, preferred_element_type=F32)
        acc = t if acc is None else acc + t
    return acc


def _dot3(a, b):
    ah, al = _split(a, 2)
    bh, bl = _split(b, 2)
    return (jnp.dot(ah, bh, preferred_element_type=F32) + jnp.dot(al, bh, preferred_element_type=F32)
            + jnp.dot(ah, bl, preferred_element_type=F32))


def _sigmoid(x):
    return 1.0 / (1.0 + jnp.exp(-x))


def _silu(x):
    return x * _sigmoid(x)


def _block_diag4(y, mask):
    return jnp.where(mask, jnp.concatenate([y, y, y, y], axis=0), 0.0)


def _bd_mask():
    r = lax.broadcasted_iota(jnp.int32, (256, 256), 0) // 64
    c = lax.broadcasted_iota(jnp.int32, (256, 256), 1) // 64
    return r == c


def _norm_matmul_kernel(x_ref, g_ref, w_ref, o_ref, h_ref):
    @pl.when(pl.program_id(1) == 0)
    def _():
        x = x_ref[...]
        ms = jnp.mean(x * x, axis=-1, keepdims=True)
        h_ref[...] = (x * lax.rsqrt(ms + EPS) * g_ref[...]).astype(BF16)

    o_ref[...] = jnp.dot(h_ref[...], w_ref[...], preferred_element_type=F32)


def _norm_matmul(x, g, w, tm, tn):
    m, d = x.shape
    n = w.shape[1]
    return pl.pallas_call(
        _norm_matmul_kernel,
        grid=(m // tm, n // tn),
        in_specs=[pl.BlockSpec((tm, d), lambda i, j: (i, 0)),
                  pl.BlockSpec((1, d), lambda i, j: (0, 0)),
                  pl.BlockSpec((d, tn), lambda i, j: (0, j))],
        out_specs=pl.BlockSpec((tm, tn), lambda i, j: (i, j)),
        out_shape=jax.ShapeDtypeStruct((m, n), F32),
        scratch_shapes=[pltpu.VMEM((tm, d), BF16)],
        compiler_params=_cp("parallel", "arbitrary"),
        name="norm_matmul",
    )(x, g.reshape(1, d), w)


def _head_norm(x, bd, g):
    ms = _dot_ls(x * x, bd, 2)
    return x * lax.rsqrt(ms + EPS) * g


def _rope(x, c, s):
    w = x.shape[1]
    lane = lax.broadcasted_iota(jnp.int32, x.shape, 1)
    sw = jnp.where((lane & 32) != 0, pltpu.roll(x, 32, 1), pltpu.roll(x, w - 32, 1))
    return x * c + sw * s


def _nsa_prep_kernel(q_ref, kvc_ref, kvs_ref, kvw_ref, cos_ref, sin_ref, qn_ref, kn_ref, bd512_ref, bd128_ref,
                     qo_ref, kc_ref, vc_ref, kst_ref, kwt_ref, vs_ref, vw_ref):
    c = cos_ref[...]
    s = sin_ref[...]
    c4 = jnp.concatenate([c, c, c, c], axis=1)
    s4 = jnp.concatenate([s, s, s, s], axis=1)
    q = _rope(_head_norm(q_ref[...], bd512_ref[...], qn_ref[...]), c4, s4)
    qo_ref[...] = (q * (HEAD_DIM ** -0.5)).astype(BF16)

    def key(ref, row):
        return _rope(_head_norm(ref[:, 0:128], bd128_ref[...], kn_ref[row:row + 1, :]), c, s)

    kc_ref[...] = key(kvc_ref, 0)
    vc_ref[...] = kvc_ref[:, 128:256]
    kst_ref[...] = key(kvs_ref, 1).T.astype(BF16)
    kwt_ref[...] = key(kvw_ref, 2).T.astype(BF16)
    vs = kvs_ref[:, 128:256].astype(BF16)
    vw = kvw_ref[:, 128:256].astype(BF16)
    ones = jnp.ones((vs.shape[0], 64), BF16)
    for g in range(NSA_GROUPS):
        vs_ref[g] = jnp.concatenate([vs[:, g * 64:(g + 1) * 64], ones], axis=1)
        vw_ref[g] = jnp.concatenate([vw[:, g * 64:(g + 1) * 64], ones], axis=1)


def _nsa_prep(z, cos_t, sin_t, q_norm, k_norm, b, s, tm):
    m = b * s
    nt = s // tm
    bd512 = jnp.asarray(np.kron(np.eye(8), np.full((64, 64), 1.0 / 64)), BF16)
    bd128 = jnp.asarray(np.kron(np.eye(2), np.full((64, 64), 1.0 / 64)), BF16)
    qn = jnp.tile(q_norm, 8).reshape(1, 512)
    kn = jnp.tile(k_norm, (1, 2))
    row = lambda i: (i, 0)
    const = lambda i: (0, 0)
    return pl.pallas_call(
        _nsa_prep_kernel,
        grid=(m // tm,),
        in_specs=[pl.BlockSpec((tm, 512), lambda i: (i, COL_NQ // 512)),
                  pl.BlockSpec((tm, 256), lambda i: (i, COL_KV // 256)),
                  pl.BlockSpec((tm, 256), lambda i: (i, COL_KV // 256 + 1)),
                  pl.BlockSpec((tm, 256), lambda i: (i, COL_KV // 256 + 2)),
                  pl.BlockSpec((tm, 128), lambda i: (i % nt, 0)),
                  pl.BlockSpec((tm, 128), lambda i: (i % nt, 0)),
                  pl.BlockSpec((1, 512), const),
                  pl.BlockSpec((3, 128), const),
                  pl.BlockSpec((512, 512), const),
                  pl.BlockSpec((128, 128), const)],
        out_specs=[pl.BlockSpec((tm, 512), row),
                   pl.BlockSpec((tm, 128), row),
                   pl.BlockSpec((tm, 128), row),
                   pl.BlockSpec((None, 128, tm), lambda i: (i // nt, 0, i % nt)),
                   pl.BlockSpec((None, 128, tm), lambda i: (i // nt, 0, i % nt)),
                   pl.BlockSpec((NSA_GROUPS, tm, 128), lambda i: (0, i, 0)),
                   pl.BlockSpec((NSA_GROUPS, tm, 128), lambda i: (0, i, 0))],
        out_shape=[jax.ShapeDtypeStruct((m, 512), BF16),
                   jax.ShapeDtypeStruct((m, 128), F32),
                   jax.ShapeDtypeStruct((m, 128), F32),
                   jax.ShapeDtypeStruct((b, 128, s), BF16),
                   jax.ShapeDtypeStruct((b, 128, s), BF16),
                   jax.ShapeDtypeStruct((NSA_GROUPS, m, 128), BF16),
                   jax.ShapeDtypeStruct((NSA_GROUPS, m, 128), BF16)],
        compiler_params=_cp("parallel"),
        name="nsa_prep",
    )(z, z, z, z, cos_t, sin_t, qn, kn, bd512, bd128)


def _gelu_tanh(x):
    return 0.5 * x * (1.0 + jnp.tanh(math.sqrt(2.0 / math.pi) * (x + 0.044715 * (x * x * x))))


def _compress_kernel(rk_ref, rv_ref, pk_ref, pv_ref, wkt_ref, wkb_ref, wvt_ref, wvb_ref, wk2_ref, wv2_ref,
                     kct_ref, vc_ref):
    def mlp(r_ref, p_ref, wt_ref, wb_ref, w2_ref):
        r = r_ref[...]
        n = r.shape[0]
        top = _dot(r + p_ref[0:1, :], wt_ref[...])
        bot = _dot(r + p_ref[1:2, :], wb_ref[...])
        hid = top + pltpu.roll(bot, n - 1, 0)
        return _dot(_gelu_tanh(hid), w2_ref[...])

    kc = mlp(rk_ref, pk_ref, wkt_ref, wkb_ref, wk2_ref)
    vc = mlp(rv_ref, pv_ref, wvt_ref, wvb_ref, wv2_ref)
    kct_ref[...] = kc.T
    for g in range(NSA_GROUPS):
        vc_ref[g] = vc[:, g * 64:(g + 1) * 64]


def _compress_weights(pos, w1, w2):
    w1r = w1.reshape(2, 16, 64, CMP_HIDDEN)
    zero = jnp.zeros_like(w1r)
    big = jnp.stack([jnp.stack([w1r, zero], axis=-2), jnp.stack([zero, w1r], axis=-2)], axis=2)
    big = big.reshape(2, 16 * 2 * 64, 2 * CMP_HIDDEN).astype(BF16)
    posr = jnp.broadcast_to(pos.reshape(2, 16, 1, 64), (2, 16, 2, 64)).reshape(2, 2048)
    w2bd = jnp.zeros((2, CMP_HIDDEN, 2, 64), F32)
    w2bd = w2bd.at[0, :, 0, :].set(w2).at[1, :, 1, :].set(w2).reshape(2 * CMP_HIDDEN, 128).astype(BF16)
    return posr, big[0], big[1], w2bd


def _compress(rk, rv, pos_k, pos_v, ck_w1, ck_w2, cv_w1, cv_w2):
    b, n, _ = rk.shape
    pk, wkt, wkb, wk2 = _compress_weights(pos_k, ck_w1, ck_w2)
    pv, wvt, wvb, wv2 = _compress_weights(pos_v, cv_w1, cv_w2)
    c2 = lambda i: (0, 0)
    return pl.pallas_call(
        _compress_kernel,
        grid=(b,),
        in_specs=[pl.BlockSpec((None, n, 2048), lambda i: (i, 0, 0)),
                  pl.BlockSpec((None, n, 2048), lambda i: (i, 0, 0)),
                  pl.BlockSpec((2, 2048), c2), pl.BlockSpec((2, 2048), c2),
                  pl.BlockSpec((2048, 256), c2), pl.BlockSpec((2048, 256), c2),
                  pl.BlockSpec((2048, 256), c2), pl.BlockSpec((2048, 256), c2),
                  pl.BlockSpec((256, 128), c2), pl.BlockSpec((256, 128), c2)],
        out_specs=[pl.BlockSpec((None, 128, n), lambda i: (i, 0, 0)),
                   pl.BlockSpec((None, NSA_GROUPS, n, 64), lambda i: (i, 0, 0, 0))],
        out_shape=[jax.ShapeDtypeStruct((b, 128, n), F32),
                   jax.ShapeDtypeStruct((b, NSA_GROUPS, n, 64), F32)],
        compiler_params=_cp("parallel"),
        name="nsa_compress",
    )(rk, rv, pk, pv, wkt, wkb, wvt, wvb, wk2, wv2)


def _cmp_attn_kernel(q_ref, kct_ref, vc_ref, ovt_ref, o_ref, sel_ref, imp_ref, *, tq, ncp, nsel):
    i = pl.program_id(2)
    tpos = i * tq + lax.broadcasted_iota(jnp.int32, (tq, 1), 0)
    nblk = lax.broadcasted_iota(jnp.int32, (1, ncp), 1)
    mask = (nblk * CMP_STRIDE + (CMP_BLOCK - 1)) <= tpos
    kt = kct_ref[...].astype(BF16)
    v = vc_ref[...].astype(BF16)
    psum = jnp.zeros((tq, ncp), F32)
    for h in range(NSA_HPG):
        s = jnp.dot(q_ref[:, h * 64:(h + 1) * 64], kt, preferred_element_type=F32)
        s = jnp.where(mask, s, NEG_BIG)
        mx = jnp.max(s, axis=-1, keepdims=True)
        e = jnp.where(mask, jnp.exp(s - mx), 0.0)
        p = e / jnp.maximum(jnp.sum(e, axis=-1, keepdims=True), 1e-30)
        o_ref[:, h * 64:(h + 1) * 64] = jnp.dot(p.astype(BF16), v, preferred_element_type=F32)
        psum = psum + p
    ph, plo = _split(psum, 2)
    ovt = ovt_ref[...]
    imp = _dot_nt(ovt, ph) + _dot_nt(ovt, plo)
    j = lax.broadcasted_iota(jnp.int32, (nsel, 1), 0)
    cur = (i * tq + lax.broadcasted_iota(jnp.int32, (1, tq), 1)) // SEL_BLOCK
    forced = (j == 0) | (j == cur) | (j == cur - 1)
    val = jnp.where(j <= cur, jnp.where(forced, FORCE_SCORE, imp), -1.0)
    imp_ref[...] = val
    cnt = jnp.zeros((nsel, tq), F32)
    for r in range(nsel):
        vr = imp_ref[r:r + 1, :]
        beats = (vr > val) | ((vr == val) & (j > r))
        cnt = cnt + jnp.where(beats, 1.0, 0.0)
    sel = jnp.where(cnt < float(min(SEL_TOPK, nsel)), 1.0, 0.0)
    sel_ref[...] = sel.T.astype(BF16)


def _cmp_attn(q_r, kct, vc, b, s, tq):
    m = b * s
    nq = s // tq
    ncp = kct.shape[2]
    nsel = s // SEL_BLOCK
    n_cmp = (s - CMP_BLOCK) // CMP_STRIDE + 1
    tok = np.arange(n_cmp)[:, None] * CMP_STRIDE + np.arange(CMP_BLOCK)[None, :]
    ov = np.zeros((ncp, nsel), np.float32)
    for l in range(CMP_BLOCK):
        ov[np.arange(n_cmp), tok[:, l] // SEL_BLOCK] += 1.0
    ovt = jnp.asarray(ov.T, BF16)
    kern = functools.partial(_cmp_attn_kernel, tq=tq, ncp=ncp, nsel=nsel)
    return pl.pallas_call(
        kern,
        grid=(b, NSA_GROUPS, nq),
        in_specs=[pl.BlockSpec((tq, 256), lambda bb, g, i: (bb * nq + i, g)),
                  pl.BlockSpec((None, 64, ncp), lambda bb, g, i: (bb, g, 0)),
                  pl.BlockSpec((None, None, ncp, 64), lambda bb, g, i: (bb, g, 0, 0)),
                  pl.BlockSpec((nsel, ncp), lambda bb, g, i: (0, 0))],
        out_specs=[pl.BlockSpec((tq, 256), lambda bb, g, i: (bb * nq + i, g)),
                   pl.BlockSpec((None, tq, nsel), lambda bb, g, i: (g, bb * nq + i, 0))],
        out_shape=[jax.ShapeDtypeStruct((m, 512), F32),
                   jax.ShapeDtypeStruct((NSA_GROUPS, m, nsel), BF16)],
        scratch_shapes=[pltpu.VMEM((nsel, tq), F32)],
        compiler_params=_cp("parallel", "parallel", "parallel"),
        name="nsa_cmp_attn",
    )(q_r, kct, vc, ovt)


def _flash_kernel(qi_ref, kt_ref, first_ref, last_ref, *refs, mode, tq, tk, nsel):
    if mode == "sel":
        q_ref, k_ref, v_ref, sel_ref, o_ref, q4_ref, m_ref, acc_ref = refs
    else:
        q_ref, k_ref, v_ref, o_ref, q4_ref, m_ref, acc_ref = refs
    n = pl.program_id(2)
    qi = qi_ref[n]
    ktile = kt_ref[n]

    @pl.when(first_ref[n] == 1)
    def _():
        for h in range(NSA_HPG):
            q4_ref[h * tq:(h + 1) * tq, :] = q_ref[:, h * 64:(h + 1) * 64]
        m_ref[...] = jnp.full(m_ref.shape, NEG_BIG, F32)
        acc_ref[...] = jnp.zeros(acc_ref.shape, F32)

    kpos = ktile * tk + lax.broadcasted_iota(jnp.int32, (1, tk), 1)
    qpos = qi * tq + lax.broadcasted_iota(jnp.int32, (tq, 1), 0)
    if mode == "sel":
        blk = lax.broadcasted_iota(jnp.int32, (nsel, 1), 0)
        expand = jnp.where(blk == kpos // SEL_BLOCK, 1.0, 0.0).astype(BF16)
        chosen = jnp.dot(sel_ref[...], expand, preferred_element_type=F32) > 0.5
        mask = chosen & (kpos <= qpos)
    else:
        dist = qpos - kpos
        mask = (dist >= 0) & (dist < WINDOW)
    bias = jnp.where(mask, 0.0, NEG_BIG)
    s = jnp.dot(q4_ref[...], k_ref[...], preferred_element_type=F32)
    s = (s.reshape(NSA_HPG, tq, tk) + bias[None]).reshape(NSA_HPG * tq, tk)
    m_prev = m_ref[...]
    m_next = jnp.maximum(m_prev, jnp.max(s, axis=-1, keepdims=True))
    p = jnp.exp(s - jnp.tile(m_next, (1, tk // 128)))
    pv = jnp.dot(p.astype(BF16), v_ref[...], preferred_element_type=F32)
    acc_ref[...] = jnp.exp(m_prev - m_next) * acc_ref[...] + pv
    m_ref[...] = m_next

    @pl.when(last_ref[n] == 1)
    def _():
        a = acc_ref[...]
        o = a / pltpu.roll(a, 64, 1)
        for h in range(NSA_HPG):
            o_ref[:, h * 64:(h + 1) * 64] = o[h * tq:(h + 1) * tq, 0:64]


def _flash(mode, q_r, kt, v, sel, b, s, tq, tk):
    m = b * s
    nq = s // tq
    nkb = s // tk
    nsel = s // SEL_BLOCK
    pairs = []
    for qi in range(nq):
        lo = 0 if mode == "sel" else qi * tq - WINDOW + 1
        kts = [k for k in range(nkb) if k * tk <= qi * tq + tq - 1 and (k + 1) * tk - 1 >= lo]
        pairs += [(qi, k, int(j == 0), int(j == len(kts) - 1)) for j, k in enumerate(kts)]
    tabs = [jnp.asarray(np.array([pr[c] for pr in pairs], np.int32)) for c in range(4)]
    in_specs = [pl.BlockSpec((tq, 256), lambda bb, g, n, qt, kt_, f, l: (bb * nq + qt[n], g)),
                pl.BlockSpec((None, 64, tk), lambda bb, g, n, qt, kt_, f, l: (bb, g, kt_[n])),
                pl.BlockSpec((None, tk, 128), lambda bb, g, n, qt, kt_, f, l: (g, bb * nkb + kt_[n], 0))]
    args = [q_r, kt, v]
    if mode == "sel":
        in_specs.append(pl.BlockSpec((None, tq, nsel), lambda bb, g, n, qt, kt_, f, l: (g, bb * nq + qt[n], 0)))
        args.append(sel)
    kern = functools.partial(_flash_kernel, mode=mode, tq=tq, tk=tk, nsel=nsel)
    return pl.pallas_call(
        kern,
        grid_spec=pltpu.PrefetchScalarGridSpec(
            num_scalar_prefetch=4,
            grid=(b, NSA_GROUPS, len(pairs)),
            in_specs=in_specs,
            out_specs=pl.BlockSpec((tq, 256), lambda bb, g, n, qt, kt_, f, l: (bb * nq + qt[n], g)),
            scratch_shapes=[pltpu.VMEM((NSA_HPG * tq, 64), BF16), pltpu.VMEM((NSA_HPG * tq, 128), F32),
                            pltpu.VMEM((NSA_HPG * tq, 128), F32)]),
        out_shape=jax.ShapeDtypeStruct((m, 512), F32),
        compiler_params=_cp("parallel", "parallel", "arbitrary"),
        name="nsa_" + mode + "_attn",
    )(*tabs, *args)


def _consts_recurrent():
    ltri = np.kron(np.eye(4), np.tril(np.ones((CHUNK, CHUNK))))
    bd_mean = np.kron(np.eye(4), np.full((64, 64), 1.0 / 64))
    bd_ones = np.kron(np.eye(4), np.ones((64, 64)))
    return (jnp.asarray(ltri, BF16), jnp.asarray(bd_mean, BF16), jnp.asarray(bd_ones, BF16))


def _group_rmsnorm(o, bd_mean, g):
    ms = _dot_ls(o * o, bd_mean, 2)
    return o * lax.rsqrt(ms + EPS) * g


def _hgrn_kernel(q_ref, f_ref, i_ref, g_ref, lbl_ref, on_ref, ltri_ref, bdm_ref, selp_ref,
                 o_ref, st_ref, z_ref, *, layer, depth, t):
    @pl.when(pl.program_id(1) == 0)
    def _():
        st_ref[...] = jnp.zeros(st_ref.shape, F32)

    lg = lbl_ref[...]
    mx = jnp.max(lg, axis=0, keepdims=True)
    ex = jnp.exp(lg - mx)
    pr = ex / jnp.sum(ex, axis=0, keepdims=True)
    cs = pr[0:1, :]
    for r in range(1, layer + 1):
        cs = cs + pr[r:r + 1, :]
    lb = cs - pr[0:1, :]

    q = _silu(q_ref[...])
    fz = f_ref[...]
    f = lb + (1.0 - lb) * _sigmoid(fz)
    logf = jnp.log(jnp.maximum(f, TINY))
    k = (1.0 - lb) * _sigmoid(-fz)
    v = i_ref[...]
    b = _dot_rs(ltri_ref[...], logf, 3)

    bdm = _bd_mask()
    nsub = t // SUB
    tl = lax.broadcasted_iota(jnp.int32, (1, SUB, 1), 1)
    a_parts = []
    for hp in range(2):
        ls = slice(hp * 128, (hp + 1) * 128)
        q3 = q[:, ls].reshape(nsub, SUB, 128)
        k3 = k[:, ls].reshape(nsub, SUB, 128)
        b3 = b[:, ls].reshape(nsub, SUB, 128)
        for sl in range(SUB):
            msk = tl >= sl
            e = jnp.exp(jnp.where(msk, b3 - b3[:, sl:sl + 1, :], 0.0))
            zz = jnp.where(msk, q3 * k3[:, sl:sl + 1, :] * e, 0.0)
            z_ref[:, sl * 128:(sl + 1) * 128] = zz.reshape(t, 128)
        a_parts.append(_dot_ls(z_ref[...], selp_ref[...], 2))
    a_all = jnp.concatenate(a_parts, axis=1)
    trow = lax.broadcasted_iota(jnp.int32, (CHUNK, 256), 0)
    scol = lax.broadcasted_iota(jnp.int32, (CHUNK, 256), 1) % CHUNK
    diag_mask = (trow // SUB) == (scol // SUB)
    m1 = (trow >= 32) & (scol < 32)
    m2 = ((trow >= 16) & (trow < 32) & (scol < 16)) | ((trow >= 48) & (scol >= 32) & (scol < 48))
    rowi = lax.broadcasted_iota(jnp.int32, (CHUNK, 1), 0)

    for c in range(t // CHUNK):
        rs = slice(c * CHUNK, (c + 1) * CHUNK)
        qc, kc, vc, bc = q[rs], k[rs], v[rs], b[rs]
        r1 = bc[31:32, :]
        q1 = qc * jnp.exp(jnp.minimum(bc - r1, 0.0))
        k1 = jnp.where(rowi < 32, kc * jnp.exp(jnp.minimum(r1 - bc, 0.0)), 0.0)
        r2 = jnp.where(rowi < 32, bc[15:16, :], bc[47:48, :])
        q2 = qc * jnp.exp(jnp.minimum(bc - r2, 0.0))
        k2 = kc * jnp.exp(jnp.minimum(r2 - bc, 0.0))
        a_off = jnp.where(m1, _dot_nt(q1, _block_diag4(k1, bdm)), 0.0) \
            + jnp.where(m2, _dot_nt(q2, _block_diag4(k2, bdm)), 0.0)
        attn = jnp.where(diag_mask, a_all[rs], 0.0) + a_off
        st = st_ref[...]
        o = _dot_nt(qc * jnp.exp(bc), st) + _dot(attn, _block_diag4(vc, bdm))
        bl = bc[CHUNK - 1:CHUNK, :]
        kd = kc * jnp.exp(bl - bc)
        st_ref[...] = st * jnp.exp(bl) + jnp.where(bdm, _dot(vc.T, kd), 0.0)
        y = _group_rmsnorm(o, bdm_ref[...], on_ref[...]) * _sigmoid(g_ref[rs, :])
        o_ref[rs, :] = y.astype(BF16)


def _hgrn(z, lb_logits, out_norm, layer, b, s, t):
    m = b * s
    nt = s // t
    depth = lb_logits.shape[0]
    ltri, bd_mean, _ = _consts_recurrent()
    selp = np.zeros((SUB, 2, 64, 2, CHUNK), np.float32)
    for sl in range(SUB):
        for h2 in range(2):
            selp[sl, h2, :, h2, sl::SUB] = 1.0
    selp = jnp.asarray(selp.reshape(SUB * 128, 128), BF16)
    col = COL_HG // 256
    c2 = lambda bb, i: (0, 0)
    kern = functools.partial(_hgrn_kernel, layer=layer, depth=depth, t=t)
    return pl.pallas_call(
        kern,
        grid=(b, nt),
        in_specs=[pl.BlockSpec((t, 256), lambda bb, i: (bb * nt + i, col)),
                  pl.BlockSpec((t, 256), lambda bb, i: (bb * nt + i, col + 1)),
                  pl.BlockSpec((t, 256), lambda bb, i: (bb * nt + i, col + 2)),
                  pl.BlockSpec((t, 256), lambda bb, i: (bb * nt + i, col + 3)),
                  pl.BlockSpec((depth, 256), c2),
                  pl.BlockSpec((1, 256), c2),
                  pl.BlockSpec((256, 256), c2),
                  pl.BlockSpec((256, 256), c2),
                  pl.BlockSpec((SUB * 128, 128), c2)],
        out_specs=pl.BlockSpec((t, 256), lambda bb, i: (bb * nt + i, 0)),
        out_shape=jax.ShapeDtypeStruct((m, 256), BF16),
        scratch_shapes=[pltpu.VMEM((256, 256), F32), pltpu.VMEM((t, SUB * 128), F32)],
        compiler_params=_cp("parallel", "arbitrary"),
        name="hgrn2",
    )(z, z, z, z, lb_logits, jnp.tile(out_norm, 4).reshape(1, 256), ltri, bd_mean, selp)


def _gdn_kernel(q_ref, k_ref, v_ref, z_ref, ba_ref, cw_ref, al_ref, dt_ref, on_ref, ltri_ref, bdm_ref, bdo_ref,
                eb_ref, ea_ref, o_ref, st_ref, prev_ref, *, t):
    @pl.when(pl.program_id(1) == 0)
    def _():
        st_ref[...] = jnp.zeros(st_ref.shape, F32)
        prev_ref[...] = jnp.zeros(prev_ref.shape, F32)

    def conv_silu(x_ref, p):
        x = x_ref[...]
        xp = jnp.concatenate([prev_ref[:, p * 256:(p + 1) * 256], x], axis=0)
        w = cw_ref[:, p * 256:(p + 1) * 256]
        y = w[3:4, :] * xp[8:, :]
        for j in range(1, 4):
            y = y + w[3 - j:4 - j, :] * pltpu.roll(xp, j, 0)[8:, :]
        return _silu(y), x[t - 8:, :]

    qa, qtail = conv_silu(q_ref, 0)
    ka, ktail = conv_silu(k_ref, 1)
    va, vtail = conv_silu(v_ref, 2)
    prev_ref[:, 0:256] = qtail
    prev_ref[:, 256:512] = ktail
    prev_ref[:, 512:768] = vtail

    bdo = bdo_ref[...]
    q = qa * lax.rsqrt(_dot_ls(qa * qa, bdo, 2) + EPS) * (HEAD_DIM ** -0.5)
    k = ka * lax.rsqrt(_dot_ls(ka * ka, bdo, 2) + EPS)
    v = va

    gl = ba_ref[...]
    beta = _dot_ls(_sigmoid(gl), eb_ref[...], 2)
    xs = gl + dt_ref[...]
    softplus = jnp.maximum(xs, 0.0) + jnp.log(1.0 + jnp.exp(-jnp.abs(xs)))
    la = _dot_ls(-jnp.exp(al_ref[...]) * softplus, ea_ref[...], 3)
    b = _dot_rs(ltri_ref[...], la, 3)

    bdm = _bd_mask()
    trow = lax.broadcasted_iota(jnp.int32, (CHUNK, 256), 0)
    scol = lax.broadcasted_iota(jnp.int32, (CHUNK, 256), 1) % CHUNK
    incl = scol <= trow
    strict = scol < trow
    eye = jnp.where(scol == trow, 1.0, 0.0)

    def mm3(x, y):
        return _dot3(x, _block_diag4(y, bdm))

    for c in range(t // CHUNK):
        rs = slice(c * CHUNK, (c + 1) * CHUNK)
        qc, kc, vc, bc, btc = q[rs], k[rs], v[rs], b[rs], beta[rs]
        brow = jnp.concatenate([bc[:, h * 64:(h + 1) * 64].T for h in range(N_HEADS4)], axis=1)
        e = jnp.exp(jnp.where(incl, bc - brow, 0.0))
        kb = kc * btc
        gram = _dot_nt(jnp.concatenate([kb, qc], axis=0), _block_diag4(kc, bdm))
        aw = jnp.where(strict, gram[:CHUNK] * e, 0.0)
        aq = jnp.where(incl, gram[CHUNK:] * e, 0.0)
        tinv = eye - aw
        pw = aw
        for _ in range(5):
            pw = mm3(pw, pw)
            tinv = tinv + mm3(tinv, pw)
        eb = jnp.exp(bc)
        u = mm3(tinv, vc * btc)
        w = mm3(tinv, kb * eb)
        st = st_ref[...]
        vnew = u - _dot_nt(w, st)
        o = _dot_nt(qc * eb, st) + _dot(aq, _block_diag4(vnew, bdm))
        bl = bc[CHUNK - 1:CHUNK, :]
        kd = kc * jnp.exp(bl - bc)
        st_ref[...] = st * jnp.exp(bl) + jnp.where(bdm, _dot(vnew.T, kd), 0.0)
        y = _group_rmsnorm(o, bdm_ref[...], on_ref[...]) * _silu(z_ref[rs, :])
        o_ref[rs, :] = y.astype(BF16)


def _gdn(z, conv_w, a_log, dt_bias, out_norm, b, s, t):
    m = b * s
    nt = s // t
    ltri, bd_mean, bd_ones = _consts_recurrent()
    eb = np.zeros((128, 256), np.float32)
    ea = np.zeros((128, 256), np.float32)
    for h in range(N_HEADS4):
        eb[h, h * 64:(h + 1) * 64] = 1.0
        ea[4 + h, h * 64:(h + 1) * 64] = 1.0
    al = jnp.zeros((1, 128), F32).at[0, 4:8].set(a_log)
    dt = jnp.zeros((1, 128), F32).at[0, 4:8].set(dt_bias)
    col = COL_GD // 256
    c2 = lambda bb, i: (0, 0)
    kern = functools.partial(_gdn_kernel, t=t)
    return pl.pallas_call(
        kern,
        grid=(b, nt),
        in_specs=[pl.BlockSpec((t, 256), lambda bb, i: (bb * nt + i, col)),
                  pl.BlockSpec((t, 256), lambda bb, i: (bb * nt + i, col + 1)),
                  pl.BlockSpec((t, 256), lambda bb, i: (bb * nt + i, col + 2)),
                  pl.BlockSpec((t, 256), lambda bb, i: (bb * nt + i, col + 3)),
                  pl.BlockSpec((t, 128), lambda bb, i: (bb * nt + i, COL_BA // 128)),
                  pl.BlockSpec((4, 768), c2),
                  pl.BlockSpec((1, 128), c2), pl.BlockSpec((1, 128), c2),
                  pl.BlockSpec((1, 256), c2),
                  pl.BlockSpec((256, 256), c2), pl.BlockSpec((256, 256), c2), pl.BlockSpec((256, 256), c2),
                  pl.BlockSpec((128, 256), c2), pl.BlockSpec((128, 256), c2)],
        out_specs=pl.BlockSpec((t, 256), lambda bb, i: (bb * nt + i, 0)),
        out_shape=jax.ShapeDtypeStruct((m, 256), BF16),
        scratch_shapes=[pltpu.VMEM((256, 256), F32), pltpu.VMEM((8, 768), F32)],
        compiler_params=_cp("parallel", "arbitrary"),
        name="gdn",
    )(z, z, z, z, z, conv_w, al, dt, jnp.tile(out_norm, 4).reshape(1, 256), ltri, bd_mean, bd_ones,
      jnp.asarray(eb, BF16), jnp.asarray(ea, BF16))


def _merge_kernel(x_ref, oc_ref, os_ref, ow_ref, ng_ref, yb_ref, yc_ref, ma_ref, mb_ref, mc_ref,
                  ex_ref, wa_ref, wb_ref, wc_ref, wo_ref, o_ref):
    sg = _sigmoid(ng_ref[...])
    ya = (_dot_ls(sg, ex_ref[0], 2) * oc_ref[...] + _dot_ls(sg, ex_ref[1], 2) * os_ref[...]
          + _dot_ls(sg, ex_ref[2], 2) * ow_ref[...])
    merged = (_sigmoid(ma_ref[...]) * _dot(ya, wa_ref[...])
              + _sigmoid(mb_ref[...]) * jnp.dot(yb_ref[...], wb_ref[...], preferred_element_type=F32)
              + _sigmoid(mc_ref[...]) * jnp.dot(yc_ref[...], wc_ref[...], preferred_element_type=F32))
    o_ref[...] = x_ref[...] + _dot(merged, wo_ref[...])


def _merge(x2, o_cmp, o_sel, o_win, z, yb, yc, wa, wb, wc, wo, tm):
    m, d = x2.shape
    ex = np.zeros((3, 128, 512), np.float32)
    for c in range(3):
        for h in range(NSA_HEADS):
            ex[c, c * NSA_HEADS + h, h * 64:(h + 1) * 64] = 1.0
    row = lambda i: (i, 0)
    c2 = lambda i: (0, 0)
    return pl.pallas_call(
        _merge_kernel,
        grid=(m // tm,),
        in_specs=[pl.BlockSpec((tm, d), row),
                  pl.BlockSpec((tm, 512), row), pl.BlockSpec((tm, 512), row), pl.BlockSpec((tm, 512), row),
                  pl.BlockSpec((tm, 128), lambda i: (i, COL_NG // 128)),
                  pl.BlockSpec((tm, 256), row), pl.BlockSpec((tm, 256), row),
                  pl.BlockSpec((tm, 1024), lambda i: (i, 0)),
                  pl.BlockSpec((tm, 1024), lambda i: (i, 1)),
                  pl.BlockSpec((tm, 1024), lambda i: (i, 2)),
                  pl.BlockSpec((3, 128, 512), lambda i: (0, 0, 0)),
                  pl.BlockSpec((512, d), c2), pl.BlockSpec((256, d), c2), pl.BlockSpec((256, d), c2),
                  pl.BlockSpec((d, d), c2)],
        out_specs=pl.BlockSpec((tm, d), row),
        out_shape=jax.ShapeDtypeStruct((m, d), F32),
        compiler_params=_cp("parallel"),
        name="merge_out",
    )(x2, o_cmp, o_sel, o_win, z, yb, yc, z, z, z, jnp.asarray(ex, BF16), wa, wb, wc, wo)


def _xattn_kernel(x_ref, g_ref, wq_ref, qn_ref, kn_ref, mkv_ref, wo_ref, o_ref):
    x = x_ref[...]
    h = x * lax.rsqrt(jnp.mean(x * x, axis=-1, keepdims=True) + EPS) * g_ref[...]
    q = _dot(h, wq_ref[...])
    outs = []
    for hd in range(N_HEADS4):
        ls = slice(hd * 128, (hd + 1) * 128)
        qh = q[:, ls]
        qh = qh * lax.rsqrt(jnp.mean(qh * qh, axis=-1, keepdims=True) + EPS) * qn_ref[...]
        kh = mkv_ref[:, ls]
        kh = kh * lax.rsqrt(jnp.mean(kh * kh, axis=-1, keepdims=True) + EPS) * kn_ref[...]
        s = _dot_nt(qh, kh) * (XATTN_HEAD_DIM ** -0.5)
        e = jnp.exp(s - jnp.max(s, axis=-1, keepdims=True))
        p = e / jnp.sum(e, axis=-1, keepdims=True)
        outs.append(_dot(p, mkv_ref[:, 512 + hd * 128:512 + (hd + 1) * 128]))
    o = jnp.concatenate(outs, axis=1)
    o_ref[...] = x + _dot(o, wo_ref[...])


def _xattn(x2, g, wq, qn, kn, mkv, wo, b, s, tm):
    m, d = x2.shape
    nt = s // tm
    c2 = lambda i: (0, 0)
    return pl.pallas_call(
        _xattn_kernel,
        grid=(m // tm,),
        in_specs=[pl.BlockSpec((tm, d), lambda i: (i, 0)),
                  pl.BlockSpec((1, d), c2),
                  pl.BlockSpec((d, 512), c2),
                  pl.BlockSpec((1, 128), c2), pl.BlockSpec((1, 128), c2),
                  pl.BlockSpec((N_MEM, 1024), lambda i: (i // nt, 0)),
                  pl.BlockSpec((512, d), c2)],
        out_specs=pl.BlockSpec((tm, d), lambda i: (i, 0)),
        out_shape=jax.ShapeDtypeStruct((m, d), F32),
        compiler_params=_cp("parallel"),
        name="mem_xattn",
    )(x2, g.reshape(1, d), wq, qn.reshape(1, 128), kn.reshape(1, 128), mkv, wo)


FFN_HALO = 16


def _ffn_kernel(x_ref, xh_ref, g_ref, wua_ref, wub_ref, cwa_ref, cwb_ref, wd_ref, o_ref, h_ref, acc_ref, *, nt, tm):
    i = pl.program_id(0)
    f = pl.program_id(1)

    @pl.when(f == 0)
    def _():
        def norm(x):
            return x * lax.rsqrt(jnp.mean(x * x, axis=-1, keepdims=True) + EPS) * g_ref[...]
        first = (i % nt) == 0
        h_ref[0:FFN_HALO, :] = jnp.where(first, 0.0, norm(xh_ref[...])).astype(BF16)
        h_ref[FFN_HALO:, :] = norm(x_ref[...]).astype(BF16)
        acc_ref[...] = jnp.zeros(acc_ref.shape, F32)

    h = h_ref[...]

    def up_conv(w_ref, cw_ref):
        u = jnp.dot(h, w_ref[...], preferred_element_type=F32)
        cw = cw_ref[...]
        y = cw[2:3, :] * u + cw[1:2, :] * pltpu.roll(u, 1, 0) + cw[0:1, :] * pltpu.roll(u, 2, 0)
        return y[FFN_HALO:, :]

    a = up_conv(wua_ref, cwa_ref)
    bb = up_conv(wub_ref, cwb_ref)
    acc_ref[...] += _dot(_silu(a) * bb, wd_ref[...])

    @pl.when(f == pl.num_programs(1) - 1)
    def _():
        o_ref[...] = x_ref[...] + acc_ref[...]


def _ffn(x2, g, w_up, conv_w, w_down, b, s, tm, tf):
    m, d = x2.shape
    nt = s // tm
    nf = D_FF // tf
    hb = tm // FFN_HALO
    kern = functools.partial(_ffn_kernel, nt=nt, tm=tm)
    return pl.pallas_call(
        kern,
        grid=(m // tm, nf),
        in_specs=[pl.BlockSpec((tm, d), lambda i, f: (i, 0)),
                  pl.BlockSpec((FFN_HALO, d), lambda i, f: (jnp.maximum(i * hb - 1, 0), 0)),
                  pl.BlockSpec((1, d), lambda i, f: (0, 0)),
                  pl.BlockSpec((d, tf), lambda i, f: (0, f)),
                  pl.BlockSpec((d, tf), lambda i, f: (0, nf + f)),
                  pl.BlockSpec((3, tf), lambda i, f: (0, f)),
                  pl.BlockSpec((3, tf), lambda i, f: (0, nf + f)),
                  pl.BlockSpec((tf, d), lambda i, f: (f, 0))],
        out_specs=pl.BlockSpec((tm, d), lambda i, f: (i, 0)),
        out_shape=jax.ShapeDtypeStruct((m, d), F32),
        scratch_shapes=[pltpu.VMEM((tm + FFN_HALO, d), BF16), pltpu.VMEM((tm, d), F32)],
        compiler_params=_cp("parallel", "arbitrary"),
        name="conv_glu_ffn",
    )(x2, x2, g.reshape(1, d), w_up, w_up, conv_w, conv_w, w_down)


def _permute_w_in(w):
    d = w.shape[0]
    gate = w[:, 1280:1304].reshape(d, NSA_HEADS, 3).transpose(0, 2, 1).reshape(d, 24)
    z104 = jnp.zeros((d, 104), w.dtype)
    z120 = jnp.zeros((d, 120), w.dtype)
    return jnp.concatenate([w[:, 3360:6432], w[:, 0:1280], w[:, 1304:3352], gate, z104, w[:, 3352:3360], z120],
                           axis=1).astype(BF16)


def _rope_tables(s):
    inv_freq = 1.0 / (ROPE_THETA ** (jnp.arange(0, HEAD_DIM, 2, dtype=F32) / HEAD_DIM))
    ang = jnp.arange(s, dtype=F32)[:, None] * inv_freq[None, :]
    c, sn = jnp.cos(ang), jnp.sin(ang)
    return jnp.tile(jnp.concatenate([c, c], axis=1), (1, 2)), jnp.tile(jnp.concatenate([-sn, sn], axis=1), (1, 2))


def kernel(x, mem, mem_norm, mem_w_kv, hgrn_lb_logits, norm_mix, w_in, nsa_q_norm, nsa_k_norm, cmp_pos_k, cmp_pos_v, cmp_k_w1, cmp_k_w2, cmp_v_w1, cmp_v_w2, hgrn_out_norm, gdn_conv, gdn_a_log, gdn_dt_bias, gdn_out_norm, w_branch_a, w_branch_b, w_branch_c, w_mix_out, norm_cross, xattn_wq, xattn_q_norm, xattn_k_norm, xattn_wo, norm_ffn, ffn_w_up, ffn_conv, ffn_w_down):
    b, s, d = x.shape
    m = b * s
    depth = w_in.shape[0]
    cos_t, sin_t = _rope_tables(s)
    x2 = x.reshape(m, d)
    mkv = _norm_matmul(mem.reshape(b * N_MEM, d), mem_norm, mem_w_kv.astype(BF16), N_MEM, 512)

    tm_in = min(1024, m)
    for l in range(depth):
        z = _norm_matmul(x2, norm_mix[l], _permute_w_in(w_in[l]), tm_in, 512)
        q_r, kcmp, vcmp, kst, kwt, vs, vw = _nsa_prep(z, cos_t, sin_t, nsa_q_norm[l], nsa_k_norm[l], b, s, 512)
        rk = kcmp.reshape(b, s // CMP_STRIDE, CMP_STRIDE * 128)
        rv = vcmp.reshape(b, s // CMP_STRIDE, CMP_STRIDE * 128)
        kct, vc = _compress(rk, rv, cmp_pos_k[l], cmp_pos_v[l], cmp_k_w1[l], cmp_k_w2[l], cmp_v_w1[l], cmp_v_w2[l])
        o_cmp, sel = _cmp_attn(q_r, kct, vc, b, s, 256)
        o_sel = _flash("sel", q_r, kst, vs, sel, b, s, 256, 512)
        o_win = _flash("win", q_r, kwt, vw, None, b, s, 256, 256)
        yb = _hgrn(z, hgrn_lb_logits, hgrn_out_norm[l], l, b, s, 256)
        yc = _gdn(z, gdn_conv[l], gdn_a_log[l], gdn_dt_bias[l], gdn_out_norm[l], b, s, 256)
        x2 = _merge(x2, o_cmp, o_sel, o_win, z, yb, yc, w_branch_a[l].astype(BF16), w_branch_b[l].astype(BF16),
                    w_branch_c[l].astype(BF16), w_mix_out[l].astype(BF16), 256)
        x2 = _xattn(x2, norm_cross[l], xattn_wq[l].astype(BF16), xattn_q_norm[l], xattn_k_norm[l], mkv,
                    xattn_wo[l].astype(BF16), b, s, 512)
        x2 = _ffn(x2, norm_ffn[l], ffn_w_up[l].astype(BF16), ffn_conv[l], ffn_w_down[l].astype(BF16), b, s, 512, 1408)
    return x2.reshape(b, s, d)
```

```python
import functools
import math

import numpy as np
import jax
import jax.numpy as jnp
from jax import lax
from jax.experimental import pallas as pl
from jax.experimental.pallas import tpu as pltpu

F32 = jnp.float32
BF16 = jnp.bfloat16

EPS = 1e-6
ROPE_THETA = 10000.0
NEG_BIG = -1e30
TINY = 1e-20
FORCE_SCORE = 1e6

D_MODEL = 1024
N_MEM = 256
HEAD_DIM = 64
NSA_HEADS = 8
NSA_GROUPS = 2
NSA_HPG = 4
CMP_BLOCK = 32
CMP_STRIDE = 16
CMP_HIDDEN = 128
SEL_BLOCK = 64
SEL_TOPK = 16
WINDOW = 512
N_HEADS4 = 4
CHUNK = 64
SUB = 16
XATTN_HEAD_DIM = 128
D_FF = 2816

COL_M = 0
COL_NQ = 3072
COL_KV = 3584
COL_NG = 4352
COL_BA = 4480
WIDTH_A = 4608
COLB_HG = 0
COLB_GD = 1024
WIDTH_B = 2048

VMEM_LIMIT = 48 * 1024 * 1024


def _cp(*sem):
    return pltpu.CompilerParams(dimension_semantics=sem, vmem_limit_bytes=VMEM_LIMIT)


def _dot(a, b):
    return jnp.dot(a.astype(BF16), b.astype(BF16), preferred_element_type=F32)


def _dot_nt(a, b):
    return lax.dot_general(a.astype(BF16), b.astype(BF16), (((1,), (1,)), ((), ())),
                           preferred_element_type=F32)


def _split(a, n):
    parts = []
    r = a
    for _ in range(n):
        p = r.astype(BF16)
        parts.append(p)
        r = r - p.astype(F32)
    return parts


def _rows_dot(blocks, rhs):
    if len(blocks) == 1:
        return [jnp.dot(blocks[0], rhs, preferred_element_type=F32)]
    r = jnp.dot(jnp.concatenate(blocks, axis=0), rhs, preferred_element_type=F32)
    out, o = [], 0
    for blk in blocks:
        out.append(r[o:o + blk.shape[0]])
        o += blk.shape[0]
    return out


def _dot_ls(a, b_exact, n=2):
    parts = _rows_dot(_split(a, n), b_exact)
    acc = parts[0]
    for p in parts[1:]:
        acc = acc + p
    return acc


def _chunk_cumsum(x):
    row = lax.broadcasted_iota(jnp.int32, (x.shape[0], 1), 0) % CHUNK
    sh = 1
    while sh < CHUNK:
        x = x + jnp.where(row >= sh, pltpu.roll(x, sh, 0), 0.0)
        sh *= 2
    return x


def _as_column(row):
    return jnp.broadcast_to(row, (8, row.shape[1])).T[:, 0:1]


def _sigmoid(x):
    return 1.0 / (1.0 + jnp.exp(-x))


def _silu(x):
    return x * _sigmoid(x)


def _block_diag4(y, bd_ones):
    yb = y.astype(BF16)
    return jnp.where(bd_ones > 0, jnp.concatenate([yb, yb, yb, yb], axis=0), jnp.zeros((), BF16))


def _bd_mask():
    r = lax.broadcasted_iota(jnp.int32, (256, 256), 0) // 64
    c = lax.broadcasted_iota(jnp.int32, (256, 256), 1) // 64
    return r == c


def _norm_matmul_kernel(x_ref, g_ref, w_ref, o_ref, h_ref):
    @pl.when(pl.program_id(1) == 0)
    def _():
        x = x_ref[...]
        ms = jnp.mean(x * x, axis=-1, keepdims=True)
        h_ref[...] = (x * lax.rsqrt(ms + EPS) * g_ref[...]).astype(BF16)

    o_ref[...] = jnp.dot(h_ref[...], w_ref[...], preferred_element_type=F32).astype(o_ref.dtype)


def _norm_matmul(x, g, w, tm, tn, out_dtype=F32):
    m, d = x.shape
    n = w.shape[1]
    return pl.pallas_call(
        _norm_matmul_kernel,
        grid=(m // tm, n // tn),
        in_specs=[pl.BlockSpec((tm, d), lambda i, j: (i, 0)),
                  pl.BlockSpec((1, d), lambda i, j: (0, 0)),
                  pl.BlockSpec((d, tn), lambda i, j: (0, j))],
        out_specs=pl.BlockSpec((tm, tn), lambda i, j: (i, j)),
        out_shape=jax.ShapeDtypeStruct((m, n), out_dtype),
        scratch_shapes=[pltpu.VMEM((tm, d), BF16)],
        compiler_params=_cp("parallel", "arbitrary"),
        name="norm_matmul",
    )(x, g.reshape(1, d), w)


def _head_norm(x, bd, g):
    ms = _dot_ls(x * x, bd, 2)
    return x * lax.rsqrt(ms + EPS) * g


def _rope(x, c, s):
    w = x.shape[1]
    lane = lax.broadcasted_iota(jnp.int32, x.shape, 1)
    sw = jnp.where((lane & 32) != 0, pltpu.roll(x, 32, 1), pltpu.roll(x, w - 32, 1))
    return x * c + sw * s


def _nsa_prep_kernel(q_ref, kvc_ref, kvs_ref, kvw_ref, cos_ref, sin_ref, qn_ref, kn_ref, bd512_ref, bd128_ref,
                     qo_ref, kc_ref, vc_ref, kst_ref, kwt_ref, vs_ref, vw_ref):
    c = cos_ref[...]
    s = sin_ref[...]
    c4 = jnp.concatenate([c, c, c, c], axis=1)
    s4 = jnp.concatenate([s, s, s, s], axis=1)
    q = _rope(_head_norm(q_ref[...].astype(F32), bd512_ref[...], qn_ref[...]), c4, s4)
    qo_ref[...] = (q * (HEAD_DIM ** -0.5)).astype(BF16)

    def key(ref, row):
        return _rope(_head_norm(ref[:, 0:128].astype(F32), bd128_ref[...], kn_ref[row:row + 1, :]), c, s)

    kc_ref[...] = key(kvc_ref, 0)
    vc_ref[...] = kvc_ref[:, 128:256].astype(F32)
    kst_ref[...] = key(kvs_ref, 1).T.astype(BF16)
    kwt_ref[...] = key(kvw_ref, 2).T.astype(BF16)
    vs = kvs_ref[:, 128:256].astype(BF16)
    vw = kvw_ref[:, 128:256].astype(BF16)
    ones = jnp.ones((vs.shape[0], 64), BF16)
    for g in range(NSA_GROUPS):
        vs_ref[g] = jnp.concatenate([vs[:, g * 64:(g + 1) * 64], ones], axis=1)
        vw_ref[g] = jnp.concatenate([vw[:, g * 64:(g + 1) * 64], ones], axis=1)


def _nsa_prep(z, cos_t, sin_t, q_norm, k_norm, b, s, tm):
    m = b * s
    nt = s // tm
    bd512 = jnp.asarray(np.kron(np.eye(8), np.full((64, 64), 1.0 / 64)), BF16)
    bd128 = jnp.asarray(np.kron(np.eye(2), np.full((64, 64), 1.0 / 64)), BF16)
    qn = jnp.tile(q_norm, 8).reshape(1, 512)
    kn = jnp.tile(k_norm, (1, 2))
    row = lambda i: (i, 0)
    const = lambda i: (0, 0)
    return pl.pallas_call(
        _nsa_prep_kernel,
        grid=(m // tm,),
        in_specs=[pl.BlockSpec((tm, 512), lambda i: (i, COL_NQ // 512)),
                  pl.BlockSpec((tm, 256), lambda i: (i, COL_KV // 256)),
                  pl.BlockSpec((tm, 256), lambda i: (i, COL_KV // 256 + 1)),
                  pl.BlockSpec((tm, 256), lambda i: (i, COL_KV // 256 + 2)),
                  pl.BlockSpec((tm, 128), lambda i: (i % nt, 0)),
                  pl.BlockSpec((tm, 128), lambda i: (i % nt, 0)),
                  pl.BlockSpec((1, 512), const),
                  pl.BlockSpec((3, 128), const),
                  pl.BlockSpec((512, 512), const),
                  pl.BlockSpec((128, 128), const)],
        out_specs=[pl.BlockSpec((tm, 512), row),
                   pl.BlockSpec((tm, 128), row),
                   pl.BlockSpec((tm, 128), row),
                   pl.BlockSpec((None, 128, tm), lambda i: (i // nt, 0, i % nt)),
                   pl.BlockSpec((None, 128, tm), lambda i: (i // nt, 0, i % nt)),
                   pl.BlockSpec((NSA_GROUPS, tm, 128), lambda i: (0, i, 0)),
                   pl.BlockSpec((NSA_GROUPS, tm, 128), lambda i: (0, i, 0))],
        out_shape=[jax.ShapeDtypeStruct((m, 512), BF16),
                   jax.ShapeDtypeStruct((m, 128), F32),
                   jax.ShapeDtypeStruct((m, 128), F32),
                   jax.ShapeDtypeStruct((b, 128, s), BF16),
                   jax.ShapeDtypeStruct((b, 128, s), BF16),
                   jax.ShapeDtypeStruct((NSA_GROUPS, m, 128), BF16),
                   jax.ShapeDtypeStruct((NSA_GROUPS, m, 128), BF16)],
        compiler_params=_cp("parallel"),
        name="nsa_prep",
    )(z, z, z, z, cos_t, sin_t, qn, kn, bd512, bd128)


def _gelu_tanh(x):
    return 0.5 * x * (1.0 + jnp.tanh(math.sqrt(2.0 / math.pi) * (x + 0.044715 * (x * x * x))))


def _compress_kernel(rk_ref, rv_ref, pk_ref, pv_ref, wkt_ref, wkb_ref, wvt_ref, wvb_ref, wk2_ref, wv2_ref,
                     kct_ref, vc_ref):
    def mlp(r_ref, p_ref, wt_ref, wb_ref, w2_ref):
        r = r_ref[...]
        n = r.shape[0]
        top = _dot(r + p_ref[0:1, :], wt_ref[...])
        bot = _dot(r + p_ref[1:2, :], wb_ref[...])
        hid = top + pltpu.roll(bot, n - 1, 0)
        return _dot(_gelu_tanh(hid), w2_ref[...])

    kc = mlp(rk_ref, pk_ref, wkt_ref, wkb_ref, wk2_ref)
    vc = mlp(rv_ref, pv_ref, wvt_ref, wvb_ref, wv2_ref)
    kct_ref[...] = kc.T
    for g in range(NSA_GROUPS):
        vc_ref[g] = vc[:, g * 64:(g + 1) * 64]


def _compress_weights(pos, w1, w2):
    w1r = w1.reshape(2, 16, 64, CMP_HIDDEN)
    zero = jnp.zeros_like(w1r)
    big = jnp.stack([jnp.stack([w1r, zero], axis=-2), jnp.stack([zero, w1r], axis=-2)], axis=2)
    big = big.reshape(2, 16 * 2 * 64, 2 * CMP_HIDDEN).astype(BF16)
    posr = jnp.broadcast_to(pos.reshape(2, 16, 1, 64), (2, 16, 2, 64)).reshape(2, 2048)
    w2bd = jnp.zeros((2, CMP_HIDDEN, 2, 64), F32)
    w2bd = w2bd.at[0, :, 0, :].set(w2).at[1, :, 1, :].set(w2).reshape(2 * CMP_HIDDEN, 128).astype(BF16)
    return posr, big[0], big[1], w2bd


def _compress(rk, rv, pos_k, pos_v, ck_w1, ck_w2, cv_w1, cv_w2):
    b, n, _ = rk.shape
    pk, wkt, wkb, wk2 = _compress_weights(pos_k, ck_w1, ck_w2)
    pv, wvt, wvb, wv2 = _compress_weights(pos_v, cv_w1, cv_w2)
    c2 = lambda i: (0, 0)
    return pl.pallas_call(
        _compress_kernel,
        grid=(b,),
        in_specs=[pl.BlockSpec((None, n, 2048), lambda i: (i, 0, 0)),
                  pl.BlockSpec((None, n, 2048), lambda i: (i, 0, 0)),
                  pl.BlockSpec((2, 2048), c2), pl.BlockSpec((2, 2048), c2),
                  pl.BlockSpec((2048, 256), c2), pl.BlockSpec((2048, 256), c2),
                  pl.BlockSpec((2048, 256), c2), pl.BlockSpec((2048, 256), c2),
                  pl.BlockSpec((256, 128), c2), pl.BlockSpec((256, 128), c2)],
        out_specs=[pl.BlockSpec((None, 128, n), lambda i: (i, 0, 0)),
                   pl.BlockSpec((None, NSA_GROUPS, n, 64), lambda i: (i, 0, 0, 0))],
        out_shape=[jax.ShapeDtypeStruct((b, 128, n), F32),
                   jax.ShapeDtypeStruct((b, NSA_GROUPS, n, 64), F32)],
        compiler_params=_cp("parallel"),
        name="nsa_compress",
    )(rk, rv, pk, pv, wkt, wkb, wvt, wvb, wk2, wv2)


def _cmp_attn_kernel(q_ref, kct_ref, vc_ref, ovt_ref, o_ref, sel_ref, imp_ref, *, tq, ncp, nsel):
    i = pl.program_id(2)
    tpos = i * tq + lax.broadcasted_iota(jnp.int32, (tq, 1), 0)
    nblk = lax.broadcasted_iota(jnp.int32, (1, ncp), 1)
    mask = (nblk * CMP_STRIDE + (CMP_BLOCK - 1)) <= tpos
    kt = kct_ref[...].astype(BF16)
    v = vc_ref[...].astype(BF16)
    psum = jnp.zeros((tq, ncp), F32)
    for h in range(NSA_HPG):
        s = jnp.dot(q_ref[:, h * 64:(h + 1) * 64], kt, preferred_element_type=F32)
        s = jnp.where(mask, s, NEG_BIG)
        mx = jnp.max(s, axis=-1, keepdims=True)
        e = jnp.where(mask, jnp.exp(s - mx), 0.0)
        p = e / jnp.maximum(jnp.sum(e, axis=-1, keepdims=True), 1e-30)
        o_ref[:, h * 64:(h + 1) * 64] = jnp.dot(p.astype(BF16), v, preferred_element_type=F32).astype(o_ref.dtype)
        psum = psum + p
    ph, plo = _split(psum, 2)
    ovt = ovt_ref[...]
    imp = _dot_nt(ovt, ph) + _dot_nt(ovt, plo)
    j = lax.broadcasted_iota(jnp.int32, (nsel, 1), 0)
    cur = (i * tq + lax.broadcasted_iota(jnp.int32, (1, tq), 1)) // SEL_BLOCK
    forced = (j == 0) | (j == cur) | (j == cur - 1)
    val = jnp.where(j <= cur, jnp.where(forced, FORCE_SCORE, imp), -1.0)
    imp_ref[...] = val
    cnt = jnp.zeros((nsel, tq), F32)
    for r in range(nsel):
        vr = imp_ref[r:r + 1, :]
        beats = (vr > val) | ((vr == val) & (j > r))
        cnt = cnt + jnp.where(beats, 1.0, 0.0)
    sel = jnp.where(cnt < float(min(SEL_TOPK, nsel)), 1.0, 0.0)
    sel_ref[...] = sel.T.astype(BF16)


def _cmp_attn(q_r, kct, vc, b, s, tq):
    m = b * s
    nq = s // tq
    ncp = kct.shape[2]
    nsel = s // SEL_BLOCK
    n_cmp = (s - CMP_BLOCK) // CMP_STRIDE + 1
    tok = np.arange(n_cmp)[:, None] * CMP_STRIDE + np.arange(CMP_BLOCK)[None, :]
    ov = np.zeros((ncp, nsel), np.float32)
    for l in range(CMP_BLOCK):
        ov[np.arange(n_cmp), tok[:, l] // SEL_BLOCK] += 1.0
    ovt = jnp.asarray(ov.T, BF16)
    kern = functools.partial(_cmp_attn_kernel, tq=tq, ncp=ncp, nsel=nsel)
    return pl.pallas_call(
        kern,
        grid=(b, NSA_GROUPS, nq),
        in_specs=[pl.BlockSpec((tq, 256), lambda bb, g, i: (bb * nq + i, g)),
                  pl.BlockSpec((None, 64, ncp), lambda bb, g, i: (bb, g, 0)),
                  pl.BlockSpec((None, None, ncp, 64), lambda bb, g, i: (bb, g, 0, 0)),
                  pl.BlockSpec((nsel, ncp), lambda bb, g, i: (0, 0))],
        out_specs=[pl.BlockSpec((tq, 256), lambda bb, g, i: (bb * nq + i, g)),
                   pl.BlockSpec((None, tq, nsel), lambda bb, g, i: (g, bb * nq + i, 0))],
        out_shape=[jax.ShapeDtypeStruct((m, 512), BF16),
                   jax.ShapeDtypeStruct((NSA_GROUPS, m, nsel), BF16)],
        scratch_shapes=[pltpu.VMEM((nsel, tq), F32)],
        compiler_params=_cp("parallel", "parallel", "parallel"),
        name="nsa_cmp_attn",
    )(q_r, kct, vc, ovt)


def _sel_attn_kernel(qi_ref, kt_ref, first_ref, last_ref, q_ref, k_ref, v_ref, sel_ref, o_ref, q4_ref, m_ref, acc_ref,
                     *, tq, tk, nsel):
    n = pl.program_id(2)
    qi = qi_ref[n]
    ktile = kt_ref[n]

    @pl.when(first_ref[n] == 1)
    def _():
        for h in range(NSA_HPG):
            q4_ref[h * tq:(h + 1) * tq, :] = q_ref[:, h * 64:(h + 1) * 64]
        m_ref[...] = jnp.full(m_ref.shape, NEG_BIG, F32)
        acc_ref[...] = jnp.zeros(acc_ref.shape, F32)

    kpos = ktile * tk + lax.broadcasted_iota(jnp.int32, (1, tk), 1)
    qpos = qi * tq + lax.broadcasted_iota(jnp.int32, (tq, 1), 0)
    blk = lax.broadcasted_iota(jnp.int32, (nsel, 1), 0)
    expand = jnp.where(blk == kpos // SEL_BLOCK, 1.0, 0.0).astype(BF16)
    chosen = jnp.dot(sel_ref[...], expand, preferred_element_type=F32) > 0.5
    bias = jnp.where(chosen & (kpos <= qpos), 0.0, NEG_BIG)
    s = jnp.dot(q4_ref[...], k_ref[...], preferred_element_type=F32)
    s = (s.reshape(NSA_HPG, tq, tk) + bias[None]).reshape(NSA_HPG * tq, tk)
    m_prev = m_ref[...]
    m_next = jnp.maximum(m_prev, jnp.max(s, axis=-1, keepdims=True))
    p = jnp.exp(s - jnp.tile(m_next, (1, tk // 128)))
    pv = jnp.dot(p.astype(BF16), v_ref[...], preferred_element_type=F32)
    acc_ref[...] = jnp.exp(m_prev - m_next) * acc_ref[...] + pv
    m_ref[...] = m_next

    @pl.when(last_ref[n] == 1)
    def _():
        a = acc_ref[...]
        o = a / pltpu.roll(a, 64, 1)
        for h in range(NSA_HPG):
            o_ref[:, h * 64:(h + 1) * 64] = o[h * tq:(h + 1) * tq, 0:64].astype(o_ref.dtype)


def _sel_attn(q_r, kt, v, sel, b, s, tq, tk):
    m = b * s
    nq = s // tq
    nkb = s // tk
    nsel = s // SEL_BLOCK
    pairs = []
    for qi in range(nq):
        kts = [k for k in range(nkb) if k * tk <= qi * tq + tq - 1]
        pairs += [(qi, k, int(j == 0), int(j == len(kts) - 1)) for j, k in enumerate(kts)]
    tabs = [jnp.asarray(np.array([pr[c] for pr in pairs], np.int32)) for c in range(4)]
    kern = functools.partial(_sel_attn_kernel, tq=tq, tk=tk, nsel=nsel)
    return pl.pallas_call(
        kern,
        grid_spec=pltpu.PrefetchScalarGridSpec(
            num_scalar_prefetch=4,
            grid=(b, NSA_GROUPS, len(pairs)),
            in_specs=[pl.BlockSpec((tq, 256), lambda bb, g, n, qt, kt_, f, l: (bb * nq + qt[n], g)),
                      pl.BlockSpec((None, 64, tk), lambda bb, g, n, qt, kt_, f, l: (bb, g, kt_[n])),
                      pl.BlockSpec((None, tk, 128), lambda bb, g, n, qt, kt_, f, l: (g, bb * nkb + kt_[n], 0)),
                      pl.BlockSpec((None, tq, nsel), lambda bb, g, n, qt, kt_, f, l: (g, bb * nq + qt[n], 0))],
            out_specs=pl.BlockSpec((tq, 256), lambda bb, g, n, qt, kt_, f, l: (bb * nq + qt[n], g)),
            scratch_shapes=[pltpu.VMEM((NSA_HPG * tq, 64), BF16), pltpu.VMEM((NSA_HPG * tq, 128), F32),
                            pltpu.VMEM((NSA_HPG * tq, 128), F32)]),
        out_shape=jax.ShapeDtypeStruct((m, 512), BF16),
        compiler_params=_cp("parallel", "parallel", "arbitrary"),
        name="nsa_sel_attn",
    )(*tabs, q_r, kt, v, sel)


def _win_attn_kernel(q_ref, *refs, tq, nblk):
    k_refs, v_refs, o_ref = refs[:nblk], refs[nblk:2 * nblk], refs[2 * nblk]
    qi = pl.program_id(2)
    q4 = jnp.concatenate([q_ref[:, h * 64:(h + 1) * 64] for h in range(NSA_HPG)], axis=0)
    kt = jnp.concatenate([r[...] for r in k_refs], axis=1)
    v = jnp.concatenate([r[...] for r in v_refs], axis=0)
    kpos = (qi - (nblk - 1)) * tq + lax.broadcasted_iota(jnp.int32, (1, nblk * tq), 1)
    dist = qi * tq + lax.broadcasted_iota(jnp.int32, (tq, 1), 0) - kpos
    bias = jnp.where((dist >= 0) & (dist < WINDOW) & (kpos >= 0), 0.0, NEG_BIG)
    s = jnp.dot(q4, kt, preferred_element_type=F32)
    s = (s.reshape(NSA_HPG, tq, nblk * tq) + bias[None]).reshape(NSA_HPG * tq, nblk * tq)
    p = jnp.exp(s - jnp.max(s, axis=-1, keepdims=True))
    pv = jnp.dot(p.astype(BF16), v, preferred_element_type=F32)
    o = pv / pltpu.roll(pv, 64, 1)
    for h in range(NSA_HPG):
        o_ref[:, h * 64:(h + 1) * 64] = o[h * tq:(h + 1) * tq, 0:64].astype(o_ref.dtype)


def _win_attn(q_r, kt, v, b, s, tq):
    m = b * s
    nq = s // tq
    nblk = WINDOW // tq + 1
    kidx = lambda qi, j: jnp.maximum(qi - (nblk - 1) + j, 0)
    k_specs = [pl.BlockSpec((None, 64, tq), functools.partial(lambda bb, g, qi, j: (bb, g, kidx(qi, j)), j=j))
               for j in range(nblk)]
    v_specs = [pl.BlockSpec((None, tq, 128), functools.partial(lambda bb, g, qi, j: (g, bb * nq + kidx(qi, j), 0), j=j))
               for j in range(nblk)]
    kern = functools.partial(_win_attn_kernel, tq=tq, nblk=nblk)
    return pl.pallas_call(
        kern,
        grid=(b, NSA_GROUPS, nq),
        in_specs=[pl.BlockSpec((tq, 256), lambda bb, g, qi: (bb * nq + qi, g))] + k_specs + v_specs,
        out_specs=pl.BlockSpec((tq, 256), lambda bb, g, qi: (bb * nq + qi, g)),
        out_shape=jax.ShapeDtypeStruct((m, 512), BF16),
        compiler_params=_cp("parallel", "parallel", "parallel"),
        name="nsa_win_attn",
    )(q_r, *([kt] * nblk), *([v] * nblk))


def _consts_recurrent():
    bd_mean = np.kron(np.eye(4), np.full((64, 64), 1.0 / 64))
    bd_ones = np.kron(np.eye(4), np.ones((64, 64)))
    return jnp.asarray(bd_mean, BF16), jnp.asarray(bd_ones, BF16)


def _group_rmsnorm(o, bd_mean, g):
    ms = _dot_ls(o * o, bd_mean, 2)
    return o * lax.rsqrt(ms + EPS) * g


def _hgrn_kernel(q_ref, f_ref, i_ref, g_ref, lbl_ref, on_ref, bdm_ref, bdo_ref, selp_ref,
                 o_ref, st_ref, z_ref, *, layer, depth, t):
    @pl.when(pl.program_id(1) == 0)
    def _():
        st_ref[...] = jnp.zeros(st_ref.shape, F32)

    lg = lbl_ref[...]
    mx = jnp.max(lg, axis=0, keepdims=True)
    ex = jnp.exp(lg - mx)
    pr = ex / jnp.sum(ex, axis=0, keepdims=True)
    cs = pr[0:1, :]
    for r in range(1, layer + 1):
        cs = cs + pr[r:r + 1, :]
    lb = cs - pr[0:1, :]

    q = _silu(q_ref[...])
    fz = f_ref[...]
    f = lb + (1.0 - lb) * _sigmoid(fz)
    logf = jnp.log(jnp.maximum(f, TINY))
    k = (1.0 - lb) * _sigmoid(-fz)
    v = i_ref[...]
    b = _chunk_cumsum(logf)

    bdm = _bd_mask()
    bdo = bdo_ref[...]
    nsub = t // SUB
    tl = lax.broadcasted_iota(jnp.int32, (1, SUB, 1), 1)
    for hp in range(2):
        ls = slice(hp * 128, (hp + 1) * 128)
        q3 = q[:, ls].reshape(nsub, SUB, 128)
        k3 = k[:, ls].reshape(nsub, SUB, 128)
        b3 = b[:, ls].reshape(nsub, SUB, 128)
        for sl in range(SUB):
            msk = tl >= sl
            e = jnp.exp(jnp.where(msk, b3 - b3[:, sl:sl + 1, :], 0.0))
            zz = jnp.where(msk, q3 * k3[:, sl:sl + 1, :] * e, 0.0)
            z_ref[hp * t:(hp + 1) * t, sl * 128:(sl + 1) * 128] = zz.reshape(t, 128)
    a_pairs = _dot_ls(z_ref[...], selp_ref[...], 2)
    a_all = jnp.concatenate([a_pairs[0:t], a_pairs[t:2 * t]], axis=1)
    trow = lax.broadcasted_iota(jnp.int32, (CHUNK, 256), 0)
    scol = lax.broadcasted_iota(jnp.int32, (CHUNK, 256), 1) % CHUNK
    diag_mask = (trow // SUB) == (scol // SUB)
    m1 = (trow >= 32) & (scol < 32)
    m2 = ((trow >= 16) & (trow < 32) & (scol < 16)) | ((trow >= 48) & (scol >= 32) & (scol < 48))
    rowi = lax.broadcasted_iota(jnp.int32, (CHUNK, 1), 0)

    chunks = [slice(c * CHUNK, (c + 1) * CHUNK) for c in range(t // CHUNK)]
    lvl1, lvl2 = [], []
    for rs in chunks:
        qc, kc, bc = q[rs], k[rs], b[rs]
        r1 = bc[31:32, :]
        q1 = qc * jnp.exp(jnp.minimum(bc - r1, 0.0))
        k1 = jnp.where(rowi < 32, kc * jnp.exp(jnp.minimum(r1 - bc, 0.0)), 0.0)
        r2 = jnp.where(rowi < 32, bc[15:16, :], bc[47:48, :])
        q2 = qc * jnp.exp(jnp.minimum(bc - r2, 0.0))
        k2 = kc * jnp.exp(jnp.minimum(r2 - bc, 0.0))
        lvl1.append(_dot_nt(q1, _block_diag4(k1, bdo)))
        lvl2.append(_dot_nt(q2, _block_diag4(k2, bdo)))
    o_intra, s_add, s_dec, q_dec = [], [], [], []
    for rs, a1, a2 in zip(chunks, lvl1, lvl2):
        kc, vc, bc = k[rs], v[rs], b[rs]
        attn = jnp.where(diag_mask, a_all[rs], 0.0) + jnp.where(m1, a1, 0.0) + jnp.where(m2, a2, 0.0)
        o_intra.append(_dot(attn, _block_diag4(vc, bdo)))
        bl = bc[CHUNK - 1:CHUNK, :]
        kdt = (kc * jnp.exp(bl - bc)).T
        s_add.append(jnp.where(bdm, _dot(kdt, vc), 0.0))
        s_dec.append(_as_column(jnp.exp(bl)))
        q_dec.append((q[rs] * jnp.exp(bc)).astype(BF16))

    st = st_ref[...]
    outs = []
    for oi, sa, sd, qd in zip(o_intra, s_add, s_dec, q_dec):
        outs.append(oi + jnp.dot(qd, st.astype(BF16), preferred_element_type=F32))
        st = sd * st + sa
    st_ref[...] = st
    o = jnp.concatenate(outs, axis=0)
    o_ref[...] = (_group_rmsnorm(o, bdm_ref[...], on_ref[...]) * _sigmoid(g_ref[...])).astype(BF16)


def _hgrn(z, lb_logits, out_norm, layer, b, s, t):
    m = b * s
    nt = s // t
    depth = lb_logits.shape[0]
    bd_mean, bd_ones = _consts_recurrent()
    selp = np.zeros((SUB, 2, 64, 2, CHUNK), np.float32)
    for sl in range(SUB):
        for h2 in range(2):
            selp[sl, h2, :, h2, sl::SUB] = 1.0
    selp = jnp.asarray(selp.reshape(SUB * 128, 128), BF16)
    col = COLB_HG // 256
    c2 = lambda bb, i: (0, 0)
    kern = functools.partial(_hgrn_kernel, layer=layer, depth=depth, t=t)
    return pl.pallas_call(
        kern,
        grid=(b, nt),
        in_specs=[pl.BlockSpec((t, 256), lambda bb, i: (bb * nt + i, col)),
                  pl.BlockSpec((t, 256), lambda bb, i: (bb * nt + i, col + 1)),
                  pl.BlockSpec((t, 256), lambda bb, i: (bb * nt + i, col + 2)),
                  pl.BlockSpec((t, 256), lambda bb, i: (bb * nt + i, col + 3)),
                  pl.BlockSpec((depth, 256), c2),
                  pl.BlockSpec((1, 256), c2),
                  pl.BlockSpec((256, 256), c2),
                  pl.BlockSpec((256, 256), c2),
                  pl.BlockSpec((SUB * 128, 128), c2)],
        out_specs=pl.BlockSpec((t, 256), lambda bb, i: (bb * nt + i, 0)),
        out_shape=jax.ShapeDtypeStruct((m, 256), BF16),
        scratch_shapes=[pltpu.VMEM((256, 256), F32), pltpu.VMEM((2 * t, SUB * 128), F32)],
        compiler_params=_cp("parallel", "arbitrary"),
        name="hgrn2",
    )(z, z, z, z, lb_logits, jnp.tile(out_norm, 4).reshape(1, 256), bd_mean, bd_ones, selp)


def _gdn_kernel(q_ref, k_ref, v_ref, z_ref, ba_ref, cw_ref, al_ref, dt_ref, on_ref, bdm_ref, bdo_ref,
                eb_ref, ea_ref, o_ref, st_ref, prev_ref, *, t):
    @pl.when(pl.program_id(1) == 0)
    def _():
        st_ref[...] = jnp.zeros(st_ref.shape, F32)
        prev_ref[...] = jnp.zeros(prev_ref.shape, F32)

    def conv_silu(x_ref, p):
        x = x_ref[...]
        xp = jnp.concatenate([prev_ref[:, p * 256:(p + 1) * 256], x], axis=0)
        w = cw_ref[:, p * 256:(p + 1) * 256]
        y = w[3:4, :] * xp[8:, :]
        for j in range(1, 4):
            y = y + w[3 - j:4 - j, :] * pltpu.roll(xp, j, 0)[8:, :]
        return _silu(y), x[t - 8:, :]

    qa, qtail = conv_silu(q_ref, 0)
    ka, ktail = conv_silu(k_ref, 1)
    va, vtail = conv_silu(v_ref, 2)
    prev_ref[:, 0:256] = qtail
    prev_ref[:, 256:512] = ktail
    prev_ref[:, 512:768] = vtail

    bdo = bdo_ref[...]
    ss = _dot_ls(jnp.concatenate([qa * qa, ka * ka], axis=0), bdo, 2)
    q = qa * lax.rsqrt(ss[0:t] + EPS) * (HEAD_DIM ** -0.5)
    k = ka * lax.rsqrt(ss[t:2 * t] + EPS)
    v = va

    gl = ba_ref[...].astype(F32)
    beta = _dot_ls(_sigmoid(gl), eb_ref[...], 2)
    xs = gl + dt_ref[...]
    softplus = jnp.maximum(xs, 0.0) + jnp.log(1.0 + jnp.exp(-jnp.abs(xs)))
    b = _dot_ls(_chunk_cumsum(-jnp.exp(al_ref[...]) * softplus), ea_ref[...], 3)

    bdm = _bd_mask()
    trow = lax.broadcasted_iota(jnp.int32, (CHUNK, 256), 0)
    scol = lax.broadcasted_iota(jnp.int32, (CHUNK, 256), 1) % CHUNK
    incl = scol <= trow
    strict = scol < trow
    eye = jnp.where(scol == trow, 1.0, 0.0)

    chunks = [slice(c * CHUNK, (c + 1) * CHUNK) for c in range(t // CHUNK)]
    nch = len(chunks)
    aw, aq = [], []
    for rs in chunks:
        qc, kc, bc = q[rs], k[rs], b[rs]
        brow = jnp.concatenate([bc[:, h * 64:(h + 1) * 64].T for h in range(N_HEADS4)], axis=1)
        e = jnp.exp(jnp.where(incl, bc - brow, 0.0))
        gram = _dot_nt(jnp.concatenate([kc * beta[rs], qc], axis=0), _block_diag4(kc, bdo))
        aw.append(jnp.where(strict, gram[:CHUNK] * e, 0.0))
        aq.append(jnp.where(incl, gram[CHUNK:] * e, 0.0).astype(BF16))
    pw = [None] * nch
    tinv = [None] * nch
    for c in range(nch):
        bh, bl_ = _split(-aw[c], 2)
        r1 = _rows_dot([bh, bl_], _block_diag4(bh, bdo))
        pw[c] = r1[0] + r1[1] + jnp.dot(bh, _block_diag4(bl_, bdo), preferred_element_type=F32)
        tinv[c] = eye - aw[c]
    for c in range(nch):
        ph, pl_ = _split(pw[c], 2)
        th, tl_ = _split(tinv[c], 2)
        r1 = _rows_dot([ph, pl_, th, tl_], _block_diag4(ph, bdo))
        r2 = _rows_dot([ph, th], _block_diag4(pl_, bdo))
        pw[c] = r1[0] + r1[1] + r2[0]
        tinv[c] = tinv[c] + r1[2] + r1[3] + r2[1]
    for _ in range(3):
        for c in range(nch):
            ph = pw[c].astype(BF16)
            r1 = _rows_dot([ph, tinv[c].astype(BF16)], _block_diag4(ph, bdo))
            pw[c] = r1[0]
            tinv[c] = tinv[c] + r1[1]
    for c in range(nch):
        tinv[c] = tinv[c] + jnp.dot(tinv[c].astype(BF16), _block_diag4(pw[c], bdo), preferred_element_type=F32)
    u, w = [], []
    for c, rs in enumerate(chunks):
        t2 = _split(tinv[c], 2)
        kbe = k[rs] * beta[rs] * jnp.exp(b[rs])
        r1 = _rows_dot(t2, _block_diag4(v[rs] * beta[rs], bdo))
        r2 = _rows_dot(t2, _block_diag4(kbe, bdo))
        u.append(r1[0] + r1[1])
        w.append(r2[0] + r2[1])
    lhs, s_add, s_dec, o0 = [], [], [], []
    for c, rs in enumerate(chunks):
        bc = b[rs]
        bl = bc[CHUNK - 1:CHUNK, :]
        kdt = (k[rs] * jnp.exp(bl - bc)).T.astype(BF16)
        rn = jnp.dot(kdt, jnp.concatenate([w[c], u[c]], axis=1).astype(BF16), preferred_element_type=F32)
        qeff = q[rs] * jnp.exp(bc) - jnp.dot(aq[c], _block_diag4(w[c], bdo), preferred_element_type=F32)
        lhs.append(jnp.concatenate([qeff.astype(BF16), jnp.where(bdm, rn[:, 0:256], 0.0).astype(BF16)], axis=0))
        s_add.append(jnp.where(bdm, rn[:, 256:512], 0.0))
        s_dec.append(_as_column(jnp.exp(bl)))
        o0.append(jnp.dot(aq[c], _block_diag4(u[c], bdo), preferred_element_type=F32))

    st = st_ref[...]
    outs = []
    for c in range(nch):
        r = jnp.dot(lhs[c], st.astype(BF16), preferred_element_type=F32)
        outs.append(o0[c] + r[0:CHUNK])
        st = s_dec[c] * st - r[CHUNK:] + s_add[c]
    st_ref[...] = st
    o = jnp.concatenate(outs, axis=0)
    o_ref[...] = (_group_rmsnorm(o, bdm_ref[...], on_ref[...]) * _silu(z_ref[...])).astype(BF16)


def _gdn(za, zb, conv_w, a_log, dt_bias, out_norm, b, s, t):
    m = b * s
    nt = s // t
    bd_mean, bd_ones = _consts_recurrent()
    eb = np.zeros((128, 256), np.float32)
    ea = np.zeros((128, 256), np.float32)
    for h in range(N_HEADS4):
        eb[h, h * 64:(h + 1) * 64] = 1.0
        ea[4 + h, h * 64:(h + 1) * 64] = 1.0
    al = jnp.zeros((1, 128), F32).at[0, 4:8].set(a_log)
    dt = jnp.zeros((1, 128), F32).at[0, 4:8].set(dt_bias)
    col = COLB_GD // 256
    c2 = lambda bb, i: (0, 0)
    kern = functools.partial(_gdn_kernel, t=t)
    return pl.pallas_call(
        kern,
        grid=(b, nt),
        in_specs=[pl.BlockSpec((t, 256), lambda bb, i: (bb * nt + i, col)),
                  pl.BlockSpec((t, 256), lambda bb, i: (bb * nt + i, col + 1)),
                  pl.BlockSpec((t, 256), lambda bb, i: (bb * nt + i, col + 2)),
                  pl.BlockSpec((t, 256), lambda bb, i: (bb * nt + i, col + 3)),
                  pl.BlockSpec((t, 128), lambda bb, i: (bb * nt + i, COL_BA // 128)),
                  pl.BlockSpec((4, 768), c2),
                  pl.BlockSpec((1, 128), c2), pl.BlockSpec((1, 128), c2),
                  pl.BlockSpec((1, 256), c2),
                  pl.BlockSpec((256, 256), c2), pl.BlockSpec((256, 256), c2),
                  pl.BlockSpec((128, 256), c2), pl.BlockSpec((128, 256), c2)],
        out_specs=pl.BlockSpec((t, 256), lambda bb, i: (bb * nt + i, 0)),
        out_shape=jax.ShapeDtypeStruct((m, 256), BF16),
        scratch_shapes=[pltpu.VMEM((256, 256), F32), pltpu.VMEM((8, 768), F32)],
        compiler_params=_cp("parallel", "arbitrary"),
        name="gdn",
    )(zb, zb, zb, zb, za, conv_w, al, dt, jnp.tile(out_norm, 4).reshape(1, 256), bd_mean, bd_ones,
      jnp.asarray(eb, BF16), jnp.asarray(ea, BF16))


def _merge_kernel(x_ref, oc_ref, os_ref, ow_ref, ng_ref, yb_ref, yc_ref, ma_ref, mb_ref, mc_ref,
                  ex_ref, wa_ref, wb_ref, wc_ref, wo_ref, o_ref):
    sg = _sigmoid(ng_ref[...].astype(F32))
    ya = (_dot_ls(sg, ex_ref[0], 2) * oc_ref[...].astype(F32) + _dot_ls(sg, ex_ref[1], 2) * os_ref[...].astype(F32)
          + _dot_ls(sg, ex_ref[2], 2) * ow_ref[...].astype(F32))
    merged = (_sigmoid(ma_ref[...].astype(F32)) * _dot(ya, wa_ref[...])
              + _sigmoid(mb_ref[...].astype(F32)) * jnp.dot(yb_ref[...], wb_ref[...], preferred_element_type=F32)
              + _sigmoid(mc_ref[...].astype(F32)) * jnp.dot(yc_ref[...], wc_ref[...], preferred_element_type=F32))
    o_ref[...] = x_ref[...] + _dot(merged, wo_ref[...])


def _merge(x2, o_cmp, o_sel, o_win, z, yb, yc, wa, wb, wc, wo, tm):
    m, d = x2.shape
    ex = np.zeros((3, 128, 512), np.float32)
    for c in range(3):
        for h in range(NSA_HEADS):
            ex[c, c * NSA_HEADS + h, h * 64:(h + 1) * 64] = 1.0
    row = lambda i: (i, 0)
    c2 = lambda i: (0, 0)
    return pl.pallas_call(
        _merge_kernel,
        grid=(m // tm,),
        in_specs=[pl.BlockSpec((tm, d), row),
                  pl.BlockSpec((tm, 512), row), pl.BlockSpec((tm, 512), row), pl.BlockSpec((tm, 512), row),
                  pl.BlockSpec((tm, 128), lambda i: (i, COL_NG // 128)),
                  pl.BlockSpec((tm, 256), row), pl.BlockSpec((tm, 256), row),
                  pl.BlockSpec((tm, 1024), lambda i: (i, 0)),
                  pl.BlockSpec((tm, 1024), lambda i: (i, 1)),
                  pl.BlockSpec((tm, 1024), lambda i: (i, 2)),
                  pl.BlockSpec((3, 128, 512), lambda i: (0, 0, 0)),
                  pl.BlockSpec((512, d), c2), pl.BlockSpec((256, d), c2), pl.BlockSpec((256, d), c2),
                  pl.BlockSpec((d, d), c2)],
        out_specs=pl.BlockSpec((tm, d), row),
        out_shape=jax.ShapeDtypeStruct((m, d), F32),
        compiler_params=_cp("parallel"),
        name="merge_out",
    )(x2, o_cmp, o_sel, o_win, z, yb, yc, z, z, z, jnp.asarray(ex, BF16), wa, wb, wc, wo)


def _xattn_kernel(x_ref, g_ref, wq_ref, qn_ref, kn_ref, mkv_ref, wo_ref, o_ref):
    x = x_ref[...]
    h = x * lax.rsqrt(jnp.mean(x * x, axis=-1, keepdims=True) + EPS) * g_ref[...]
    q = _dot(h, wq_ref[...])
    outs = []
    for hd in range(N_HEADS4):
        ls = slice(hd * 128, (hd + 1) * 128)
        qh = q[:, ls]
        qh = qh * lax.rsqrt(jnp.mean(qh * qh, axis=-1, keepdims=True) + EPS) * qn_ref[...]
        kh = mkv_ref[:, ls]
        kh = kh * lax.rsqrt(jnp.mean(kh * kh, axis=-1, keepdims=True) + EPS) * kn_ref[...]
        s = _dot_nt(qh, kh) * (XATTN_HEAD_DIM ** -0.5)
        e = jnp.exp(s - jnp.max(s, axis=-1, keepdims=True))
        p = e / jnp.sum(e, axis=-1, keepdims=True)
        outs.append(_dot(p, mkv_ref[:, 512 + hd * 128:512 + (hd + 1) * 128]))
    o = jnp.concatenate(outs, axis=1)
    o_ref[...] = x + _dot(o, wo_ref[...])


def _xattn(x2, g, wq, qn, kn, mkv, wo, b, s, tm):
    m, d = x2.shape
    nt = s // tm
    c2 = lambda i: (0, 0)
    return pl.pallas_call(
        _xattn_kernel,
        grid=(m // tm,),
        in_specs=[pl.BlockSpec((tm, d), lambda i: (i, 0)),
                  pl.BlockSpec((1, d), c2),
                  pl.BlockSpec((d, 512), c2),
                  pl.BlockSpec((1, 128), c2), pl.BlockSpec((1, 128), c2),
                  pl.BlockSpec((N_MEM, 1024), lambda i: (i // nt, 0)),
                  pl.BlockSpec((512, d), c2)],
        out_specs=pl.BlockSpec((tm, d), lambda i: (i, 0)),
        out_shape=jax.ShapeDtypeStruct((m, d), F32),
        compiler_params=_cp("parallel"),
        name="mem_xattn",
    )(x2, g.reshape(1, d), wq, qn.reshape(1, 128), kn.reshape(1, 128), mkv, wo)


FFN_HALO = 16


def _ffn_kernel(x_ref, xh_ref, g_ref, wua_ref, wub_ref, cwa_ref, cwb_ref, wd_ref, o_ref, h_ref, acc_ref, *, nt, tm):
    i = pl.program_id(0)
    f = pl.program_id(1)

    @pl.when(f == 0)
    def _():
        def norm(x):
            return x * lax.rsqrt(jnp.mean(x * x, axis=-1, keepdims=True) + EPS) * g_ref[...]
        first = (i % nt) == 0
        h_ref[0:FFN_HALO, :] = jnp.where(first, 0.0, norm(xh_ref[...])).astype(BF16)
        h_ref[FFN_HALO:, :] = norm(x_ref[...]).astype(BF16)
        acc_ref[...] = jnp.zeros(acc_ref.shape, F32)

    h = h_ref[...]

    def up_conv(w_ref, cw_ref):
        u = jnp.dot(h, w_ref[...], preferred_element_type=F32)
        cw = cw_ref[...]
        y = cw[2:3, :] * u + cw[1:2, :] * pltpu.roll(u, 1, 0) + cw[0:1, :] * pltpu.roll(u, 2, 0)
        return y[FFN_HALO:, :]

    a = up_conv(wua_ref, cwa_ref)
    bb = up_conv(wub_ref, cwb_ref)
    acc_ref[...] += _dot(_silu(a) * bb, wd_ref[...])

    @pl.when(f == pl.num_programs(1) - 1)
    def _():
        o_ref[...] = x_ref[...] + acc_ref[...]


def _ffn(x2, g, w_up, conv_w, w_down, b, s, tm, tf):
    m, d = x2.shape
    nt = s // tm
    nf = D_FF // tf
    hb = tm // FFN_HALO
    kern = functools.partial(_ffn_kernel, nt=nt, tm=tm)
    return pl.pallas_call(
        kern,
        grid=(m // tm, nf),
        in_specs=[pl.BlockSpec((tm, d), lambda i, f: (i, 0)),
                  pl.BlockSpec((FFN_HALO, d), lambda i, f: (jnp.maximum(i * hb - 1, 0), 0)),
                  pl.BlockSpec((1, d), lambda i, f: (0, 0)),
                  pl.BlockSpec((d, tf), lambda i, f: (0, f)),
                  pl.BlockSpec((d, tf), lambda i, f: (0, nf + f)),
                  pl.BlockSpec((3, tf), lambda i, f: (0, f)),
                  pl.BlockSpec((3, tf), lambda i, f: (0, nf + f)),
                  pl.BlockSpec((tf, d), lambda i, f: (f, 0))],
        out_specs=pl.BlockSpec((tm, d), lambda i, f: (i, 0)),
        out_shape=jax.ShapeDtypeStruct((m, d), F32),
        scratch_shapes=[pltpu.VMEM((tm + FFN_HALO, d), BF16), pltpu.VMEM((tm, d), F32)],
        compiler_params=_cp("parallel", "arbitrary"),
        name="conv_glu_ffn",
    )(x2, x2, g.reshape(1, d), w_up, w_up, conv_w, conv_w, w_down)


def _permute_w_in(w):
    d = w.shape[0]
    gate = w[:, 1280:1304].reshape(d, NSA_HEADS, 3).transpose(0, 2, 1).reshape(d, 24)
    z104 = jnp.zeros((d, 104), w.dtype)
    z120 = jnp.zeros((d, 120), w.dtype)
    wa = jnp.concatenate([w[:, 3360:6432], w[:, 0:1280], gate, z104, w[:, 3352:3360], z120], axis=1)
    return wa.astype(BF16), w[:, 1304:3352].astype(BF16)


def _rope_tables(s):
    inv_freq = 1.0 / (ROPE_THETA ** (jnp.arange(0, HEAD_DIM, 2, dtype=F32) / HEAD_DIM))
    ang = jnp.arange(s, dtype=F32)[:, None] * inv_freq[None, :]
    c, sn = jnp.cos(ang), jnp.sin(ang)
    return jnp.tile(jnp.concatenate([c, c], axis=1), (1, 2)), jnp.tile(jnp.concatenate([-sn, sn], axis=1), (1, 2))


def kernel(x, mem, mem_norm, mem_w_kv, hgrn_lb_logits, norm_mix, w_in, nsa_q_norm, nsa_k_norm, cmp_pos_k, cmp_pos_v, cmp_k_w1, cmp_k_w2, cmp_v_w1, cmp_v_w2, hgrn_out_norm, gdn_conv, gdn_a_log, gdn_dt_bias, gdn_out_norm, w_branch_a, w_branch_b, w_branch_c, w_mix_out, norm_cross, xattn_wq, xattn_q_norm, xattn_k_norm, xattn_wo, norm_ffn, ffn_w_up, ffn_conv, ffn_w_down):
    b, s, d = x.shape
    m = b * s
    depth = w_in.shape[0]
    cos_t, sin_t = _rope_tables(s)
    x2 = x.reshape(m, d)
    mkv = _norm_matmul(mem.reshape(b * N_MEM, d), mem_norm, mem_w_kv.astype(BF16), N_MEM, 512)

    tm_in = min(1024, m)
    for l in range(depth):
        w_a, w_b = _permute_w_in(w_in[l])
        z = _norm_matmul(x2, norm_mix[l], w_a, tm_in, WIDTH_A // 3, BF16)
        zb = _norm_matmul(x2, norm_mix[l], w_b, tm_in, WIDTH_B // 2)
        q_r, kcmp, vcmp, kst, kwt, vs, vw = _nsa_prep(z, cos_t, sin_t, nsa_q_norm[l], nsa_k_norm[l], b, s, 512)
        rk = kcmp.reshape(b, s // CMP_STRIDE, CMP_STRIDE * 128)
        rv = vcmp.reshape(b, s // CMP_STRIDE, CMP_STRIDE * 128)
        kct, vc = _compress(rk, rv, cmp_pos_k[l], cmp_pos_v[l], cmp_k_w1[l], cmp_k_w2[l], cmp_v_w1[l], cmp_v_w2[l])
        o_cmp, sel = _cmp_attn(q_r, kct, vc, b, s, 256)
        o_sel = _sel_attn(q_r, kst, vs, sel, b, s, 256, 512)
        o_win = _win_attn(q_r, kwt, vw, b, s, 256)
        yb = _hgrn(zb, hgrn_lb_logits, hgrn_out_norm[l], l, b, s, 256)
        yc = _gdn(z, zb, gdn_conv[l], gdn_a_log[l], gdn_dt_bias[l], gdn_out_norm[l], b, s, 256)
        x2 = _merge(x2, o_cmp, o_sel, o_win, z, yb, yc, w_branch_a[l].astype(BF16), w_branch_b[l].astype(BF16),
                    w_branch_c[l].astype(BF16), w_mix_out[l].astype(BF16), 256)
        x2 = _xattn(x2, norm_cross[l], xattn_wq[l].astype(BF16), xattn_q_norm[l], xattn_k_norm[l], mkv,
                    xattn_wo[l].astype(BF16), b, s, 512)
        x2 = _ffn(x2, norm_ffn[l], ffn_w_up[l].astype(BF16), ffn_conv[l], ffn_w_down[l].astype(BF16), b, s, 512, 1408)
    return x2.reshape(b, s, d)
```

```python
import functools
import math

import numpy as np
import jax
import jax.numpy as jnp
from jax import lax
from jax.experimental import pallas as pl
from jax.experimental.pallas import tpu as pltpu

F32 = jnp.float32
BF16 = jnp.bfloat16

EPS = 1e-6
ROPE_THETA = 10000.0
NEG_BIG = -1e30
TINY = 1e-20
FORCE_SCORE = 1e6

D_MODEL = 1024
N_MEM = 256
HEAD_DIM = 64
NSA_HEADS = 8
NSA_GROUPS = 2
NSA_HPG = 4
CMP_BLOCK = 32
CMP_STRIDE = 16
CMP_HIDDEN = 128
SEL_BLOCK = 64
SEL_TOPK = 16
WINDOW = 512
N_HEADS4 = 4
CHUNK = 64
SUB = 16
XATTN_HEAD_DIM = 128
D_FF = 2816

COL_M = 0
COL_NQ = 3072
COL_KV = 3584
COL_NG = 4352
COL_BA = 4480
WIDTH_A = 4608
COLB_HG = 0
COLB_GD = 1024
WIDTH_B = 2048

VMEM_LIMIT = 48 * 1024 * 1024


def _cp(*sem):
    return pltpu.CompilerParams(dimension_semantics=sem, vmem_limit_bytes=VMEM_LIMIT)


def _dot(a, b):
    return jnp.dot(a.astype(BF16), b.astype(BF16), preferred_element_type=F32)


def _dot_nt(a, b):
    return lax.dot_general(a.astype(BF16), b.astype(BF16), (((1,), (1,)), ((), ())),
                           preferred_element_type=F32)


def _split(a, n):
    parts = []
    r = a
    for _ in range(n):
        p = r.astype(BF16)
        parts.append(p)
        r = r - p.astype(F32)
    return parts


def _rows_dot(blocks, rhs):
    if len(blocks) == 1:
        return [jnp.dot(blocks[0], rhs, preferred_element_type=F32)]
    r = jnp.dot(jnp.concatenate(blocks, axis=0), rhs, preferred_element_type=F32)
    out, o = [], 0
    for blk in blocks:
        out.append(r[o:o + blk.shape[0]])
        o += blk.shape[0]
    return out


def _dot_ls(a, b_exact, n=2):
    parts = _rows_dot(_split(a, n), b_exact)
    acc = parts[0]
    for p in parts[1:]:
        acc = acc + p
    return acc


def _chunk_cumsum(x):
    row = lax.broadcasted_iota(jnp.int32, (x.shape[0], 1), 0) % CHUNK
    sh = 1
    while sh < CHUNK:
        x = x + jnp.where(row >= sh, pltpu.roll(x, sh, 0), 0.0)
        sh *= 2
    return x


def _as_column(row):
    return jnp.broadcast_to(row, (8, row.shape[1])).T[:, 0:1]


def _sigmoid(x):
    return 1.0 / (1.0 + jnp.exp(-x))


def _silu(x):
    return x * _sigmoid(x)


def _block_diag4(y, bd_ones):
    yb = y.astype(BF16)
    return jnp.where(bd_ones > 0, jnp.concatenate([yb, yb, yb, yb], axis=0), jnp.zeros((), BF16))


def _bd_mask():
    r = lax.broadcasted_iota(jnp.int32, (256, 256), 0) // 64
    c = lax.broadcasted_iota(jnp.int32, (256, 256), 1) // 64
    return r == c


def _norm_matmul_kernel(x_ref, g_ref, w_ref, o_ref, h_ref):
    @pl.when(pl.program_id(1) == 0)
    def _():
        x = x_ref[...]
        ms = jnp.mean(x * x, axis=-1, keepdims=True)
        h_ref[...] = (x * lax.rsqrt(ms + EPS) * g_ref[...]).astype(BF16)

    o_ref[...] = jnp.dot(h_ref[...], w_ref[...], preferred_element_type=F32).astype(o_ref.dtype)


def _norm_matmul(x, g, w, tm, tn, out_dtype=F32):
    m, d = x.shape
    n = w.shape[1]
    return pl.pallas_call(
        _norm_matmul_kernel,
        grid=(m // tm, n // tn),
        in_specs=[pl.BlockSpec((tm, d), lambda i, j: (i, 0)),
                  pl.BlockSpec((1, d), lambda i, j: (0, 0)),
                  pl.BlockSpec((d, tn), lambda i, j: (0, j))],
        out_specs=pl.BlockSpec((tm, tn), lambda i, j: (i, j)),
        out_shape=jax.ShapeDtypeStruct((m, n), out_dtype),
        scratch_shapes=[pltpu.VMEM((tm, d), BF16)],
        compiler_params=_cp("parallel", "arbitrary"),
        name="norm_matmul",
    )(x, g.reshape(1, d), w)


def _head_norm(x, bd, g):
    ms = _dot_ls(x * x, bd, 2)
    return x * lax.rsqrt(ms + EPS) * g


def _rope(x, c, s):
    w = x.shape[1]
    lane = lax.broadcasted_iota(jnp.int32, x.shape, 1)
    sw = jnp.where((lane & 32) != 0, pltpu.roll(x, 32, 1), pltpu.roll(x, w - 32, 1))
    return x * c + sw * s


def _nsa_prep_kernel(q_ref, kvc_ref, kvs_ref, kvw_ref, cos_ref, sin_ref, qn_ref, kn_ref, bd512_ref, bd128_ref,
                     qo_ref, kc_ref, vc_ref, kst_ref, kwt_ref, vs_ref, vw_ref):
    c = cos_ref[...]
    s = sin_ref[...]
    c4 = jnp.concatenate([c, c, c, c], axis=1)
    s4 = jnp.concatenate([s, s, s, s], axis=1)
    q = _rope(_head_norm(q_ref[...].astype(F32), bd512_ref[...], qn_ref[...]), c4, s4)
    qo_ref[...] = (q * (HEAD_DIM ** -0.5)).astype(BF16)

    def key(ref, row):
        return _rope(_head_norm(ref[:, 0:128].astype(F32), bd128_ref[...], kn_ref[row:row + 1, :]), c, s)

    kc_ref[...] = key(kvc_ref, 0)
    vc_ref[...] = kvc_ref[:, 128:256].astype(F32)
    kst_ref[...] = key(kvs_ref, 1).T.astype(BF16)
    kwt_ref[...] = key(kvw_ref, 2).T.astype(BF16)
    vs = kvs_ref[:, 128:256].astype(BF16)
    vw = kvw_ref[:, 128:256].astype(BF16)
    ones = jnp.ones((vs.shape[0], 64), BF16)
    for g in range(NSA_GROUPS):
        vs_ref[g] = jnp.concatenate([vs[:, g * 64:(g + 1) * 64], ones], axis=1)
        vw_ref[g] = jnp.concatenate([vw[:, g * 64:(g + 1) * 64], ones], axis=1)


def _nsa_prep(z, cos_t, sin_t, q_norm, k_norm, b, s, tm):
    m = b * s
    nt = s // tm
    bd512 = jnp.asarray(np.kron(np.eye(8), np.full((64, 64), 1.0 / 64)), BF16)
    bd128 = jnp.asarray(np.kron(np.eye(2), np.full((64, 64), 1.0 / 64)), BF16)
    qn = jnp.tile(q_norm, 8).reshape(1, 512)
    kn = jnp.tile(k_norm, (1, 2))
    row = lambda i: (i, 0)
    const = lambda i: (0, 0)
    return pl.pallas_call(
        _nsa_prep_kernel,
        grid=(m // tm,),
        in_specs=[pl.BlockSpec((tm, 512), lambda i: (i, COL_NQ // 512)),
                  pl.BlockSpec((tm, 256), lambda i: (i, COL_KV // 256)),
                  pl.BlockSpec((tm, 256), lambda i: (i, COL_KV // 256 + 1)),
                  pl.BlockSpec((tm, 256), lambda i: (i, COL_KV // 256 + 2)),
                  pl.BlockSpec((tm, 128), lambda i: (i % nt, 0)),
                  pl.BlockSpec((tm, 128), lambda i: (i % nt, 0)),
                  pl.BlockSpec((1, 512), const),
                  pl.BlockSpec((3, 128), const),
                  pl.BlockSpec((512, 512), const),
                  pl.BlockSpec((128, 128), const)],
        out_specs=[pl.BlockSpec((tm, 512), row),
                   pl.BlockSpec((tm, 128), row),
                   pl.BlockSpec((tm, 128), row),
                   pl.BlockSpec((None, 128, tm), lambda i: (i // nt, 0, i % nt)),
                   pl.BlockSpec((None, 128, tm), lambda i: (i // nt, 0, i % nt)),
                   pl.BlockSpec((NSA_GROUPS, tm, 128), lambda i: (0, i, 0)),
                   pl.BlockSpec((NSA_GROUPS, tm, 128), lambda i: (0, i, 0))],
        out_shape=[jax.ShapeDtypeStruct((m, 512), BF16),
                   jax.ShapeDtypeStruct((m, 128), F32),
                   jax.ShapeDtypeStruct((m, 128), F32),
                   jax.ShapeDtypeStruct((b, 128, s), BF16),
                   jax.ShapeDtypeStruct((b, 128, s), BF16),
                   jax.ShapeDtypeStruct((NSA_GROUPS, m, 128), BF16),
                   jax.ShapeDtypeStruct((NSA_GROUPS, m, 128), BF16)],
        compiler_params=_cp("parallel"),
        name="nsa_prep",
    )(z, z, z, z, cos_t, sin_t, qn, kn, bd512, bd128)


def _gelu_tanh(x):
    return 0.5 * x * (1.0 + jnp.tanh(math.sqrt(2.0 / math.pi) * (x + 0.044715 * (x * x * x))))


def _compress_kernel(rk_ref, rv_ref, pk_ref, pv_ref, wkt_ref, wkb_ref, wvt_ref, wvb_ref, wk2_ref, wv2_ref,
                     kct_ref, vc_ref):
    def mlp(r_ref, p_ref, wt_ref, wb_ref, w2_ref):
        r = r_ref[...]
        n = r.shape[0]
        top = _dot(r + p_ref[0:1, :], wt_ref[...])
        bot = _dot(r + p_ref[1:2, :], wb_ref[...])
        hid = top + pltpu.roll(bot, n - 1, 0)
        return _dot(_gelu_tanh(hid), w2_ref[...])

    kc = mlp(rk_ref, pk_ref, wkt_ref, wkb_ref, wk2_ref)
    vc = mlp(rv_ref, pv_ref, wvt_ref, wvb_ref, wv2_ref)
    kct_ref[...] = kc.T
    for g in range(NSA_GROUPS):
        vc_ref[g] = vc[:, g * 64:(g + 1) * 64]


def _compress_weights(pos, w1, w2):
    w1r = w1.reshape(2, 16, 64, CMP_HIDDEN)
    zero = jnp.zeros_like(w1r)
    big = jnp.stack([jnp.stack([w1r, zero], axis=-2), jnp.stack([zero, w1r], axis=-2)], axis=2)
    big = big.reshape(2, 16 * 2 * 64, 2 * CMP_HIDDEN).astype(BF16)
    posr = jnp.broadcast_to(pos.reshape(2, 16, 1, 64), (2, 16, 2, 64)).reshape(2, 2048)
    w2bd = jnp.zeros((2, CMP_HIDDEN, 2, 64), F32)
    w2bd = w2bd.at[0, :, 0, :].set(w2).at[1, :, 1, :].set(w2).reshape(2 * CMP_HIDDEN, 128).astype(BF16)
    return posr, big[0], big[1], w2bd


def _compress(rk, rv, pos_k, pos_v, ck_w1, ck_w2, cv_w1, cv_w2):
    b, n, _ = rk.shape
    pk, wkt, wkb, wk2 = _compress_weights(pos_k, ck_w1, ck_w2)
    pv, wvt, wvb, wv2 = _compress_weights(pos_v, cv_w1, cv_w2)
    c2 = lambda i: (0, 0)
    return pl.pallas_call(
        _compress_kernel,
        grid=(b,),
        in_specs=[pl.BlockSpec((None, n, 2048), lambda i: (i, 0, 0)),
                  pl.BlockSpec((None, n, 2048), lambda i: (i, 0, 0)),
                  pl.BlockSpec((2, 2048), c2), pl.BlockSpec((2, 2048), c2),
                  pl.BlockSpec((2048, 256), c2), pl.BlockSpec((2048, 256), c2),
                  pl.BlockSpec((2048, 256), c2), pl.BlockSpec((2048, 256), c2),
                  pl.BlockSpec((256, 128), c2), pl.BlockSpec((256, 128), c2)],
        out_specs=[pl.BlockSpec((None, 128, n), lambda i: (i, 0, 0)),
                   pl.BlockSpec((None, NSA_GROUPS, n, 64), lambda i: (i, 0, 0, 0))],
        out_shape=[jax.ShapeDtypeStruct((b, 128, n), F32),
                   jax.ShapeDtypeStruct((b, NSA_GROUPS, n, 64), F32)],
        compiler_params=_cp("parallel"),
        name="nsa_compress",
    )(rk, rv, pk, pv, wkt, wkb, wvt, wvb, wk2, wv2)


def _cmp_attn_kernel(q_ref, kct_ref, vc_ref, ovt_ref, o_ref, sel_ref, imp_ref, *, tq, ncp, nsel):
    i = pl.program_id(2)
    tpos = i * tq + lax.broadcasted_iota(jnp.int32, (tq, 1), 0)
    nblk = lax.broadcasted_iota(jnp.int32, (1, ncp), 1)
    bias = jnp.where((nblk * CMP_STRIDE + (CMP_BLOCK - 1)) <= tpos, 0.0, NEG_BIG)
    any_valid = jnp.where(tpos >= CMP_BLOCK - 1, 1.0, 0.0)
    q4 = jnp.concatenate([q_ref[:, h * 64:(h + 1) * 64] for h in range(NSA_HPG)], axis=0)
    s = jnp.dot(q4, kct_ref[...].astype(BF16), preferred_element_type=F32)
    s = s.reshape(NSA_HPG, tq, ncp) + bias[None]
    e = jnp.exp(s - jnp.max(s, axis=-1, keepdims=True))
    p = e * (any_valid / jnp.maximum(jnp.sum(e, axis=-1, keepdims=True), 1e-30))
    o = jnp.dot(p.reshape(NSA_HPG * tq, ncp).astype(BF16), vc_ref[...].astype(BF16), preferred_element_type=F32)
    for h in range(NSA_HPG):
        o_ref[:, h * 64:(h + 1) * 64] = o[h * tq:(h + 1) * tq].astype(o_ref.dtype)
    psum = p[0] + p[1] + p[2] + p[3]
    ph, plo = _split(psum, 2)
    ovt = ovt_ref[...]
    imp = _dot_nt(ovt, ph) + _dot_nt(ovt, plo)
    j = lax.broadcasted_iota(jnp.int32, (nsel, 1), 0)
    cur = (i * tq + lax.broadcasted_iota(jnp.int32, (1, tq), 1)) // SEL_BLOCK
    forced = (j == 0) | (j == cur) | (j == cur - 1)
    val = jnp.where(j <= cur, jnp.where(forced, FORCE_SCORE, imp), -1.0)
    imp_ref[...] = val
    groups = [val[g * 8:(g + 1) * 8] for g in range(nsel // 8)]
    cnt = [jnp.zeros((8, tq), F32) for _ in groups]
    jl = lax.broadcasted_iota(jnp.int32, (8, 1), 0)
    for r in range(nsel):
        vr = imp_ref[r:r + 1, :]
        for g, vg in enumerate(groups):
            if g * 8 > r:
                beats = vr >= vg
            elif g * 8 + 7 < r:
                beats = vr > vg
            else:
                beats = (vr > vg) | ((vr == vg) & (jl + g * 8 > r))
            cnt[g] = cnt[g] + jnp.where(beats, 1.0, 0.0)
    sel = jnp.where(jnp.concatenate(cnt, axis=0) < float(min(SEL_TOPK, nsel)), 1.0, 0.0)
    sel_ref[...] = sel.T.astype(BF16)


def _cmp_attn(q_r, kct, vc, b, s, tq):
    m = b * s
    nq = s // tq
    ncp = kct.shape[2]
    nsel = s // SEL_BLOCK
    n_cmp = (s - CMP_BLOCK) // CMP_STRIDE + 1
    tok = np.arange(n_cmp)[:, None] * CMP_STRIDE + np.arange(CMP_BLOCK)[None, :]
    ov = np.zeros((ncp, nsel), np.float32)
    for l in range(CMP_BLOCK):
        ov[np.arange(n_cmp), tok[:, l] // SEL_BLOCK] += 1.0
    ovt = jnp.asarray(ov.T, BF16)
    kern = functools.partial(_cmp_attn_kernel, tq=tq, ncp=ncp, nsel=nsel)
    return pl.pallas_call(
        kern,
        grid=(b, NSA_GROUPS, nq),
        in_specs=[pl.BlockSpec((tq, 256), lambda bb, g, i: (bb * nq + i, g)),
                  pl.BlockSpec((None, 64, ncp), lambda bb, g, i: (bb, g, 0)),
                  pl.BlockSpec((None, None, ncp, 64), lambda bb, g, i: (bb, g, 0, 0)),
                  pl.BlockSpec((nsel, ncp), lambda bb, g, i: (0, 0))],
        out_specs=[pl.BlockSpec((tq, 256), lambda bb, g, i: (bb * nq + i, g)),
                   pl.BlockSpec((None, tq, nsel), lambda bb, g, i: (g, bb * nq + i, 0))],
        out_shape=[jax.ShapeDtypeStruct((m, 512), BF16),
                   jax.ShapeDtypeStruct((NSA_GROUPS, m, nsel), BF16)],
        scratch_shapes=[pltpu.VMEM((nsel, tq), F32)],
        compiler_params=_cp("parallel", "parallel", "parallel"),
        name="nsa_cmp_attn",
    )(q_r, kct, vc, ovt)


def _sel_attn_kernel(qi_ref, kt_ref, first_ref, last_ref, q_ref, k_ref, v_ref, sel_ref, o_ref, q4_ref, m_ref, acc_ref,
                     *, tq, tk, nsel):
    n = pl.program_id(2)
    qi = qi_ref[n]
    ktile = kt_ref[n]

    @pl.when(first_ref[n] == 1)
    def _():
        for h in range(NSA_HPG):
            q4_ref[h * tq:(h + 1) * tq, :] = q_ref[:, h * 64:(h + 1) * 64]
        m_ref[...] = jnp.full(m_ref.shape, NEG_BIG, F32)
        acc_ref[...] = jnp.zeros(acc_ref.shape, F32)

    kpos = ktile * tk + lax.broadcasted_iota(jnp.int32, (1, tk), 1)
    qpos = qi * tq + lax.broadcasted_iota(jnp.int32, (tq, 1), 0)
    blk = lax.broadcasted_iota(jnp.int32, (nsel, 1), 0)
    expand = jnp.where(blk == kpos // SEL_BLOCK, 1.0, 0.0).astype(BF16)
    chosen = jnp.dot(sel_ref[...], expand, preferred_element_type=F32) > 0.5
    bias = jnp.where(chosen & (kpos <= qpos), 0.0, NEG_BIG)
    s = jnp.dot(q4_ref[...], k_ref[...], preferred_element_type=F32)
    s = (s.reshape(NSA_HPG, tq, tk) + bias[None]).reshape(NSA_HPG * tq, tk)
    m_prev = m_ref[...]
    m_next = jnp.maximum(m_prev, jnp.max(s, axis=-1, keepdims=True))
    p = jnp.exp(s - jnp.tile(m_next, (1, tk // 128)))
    pv = jnp.dot(p.astype(BF16), v_ref[...], preferred_element_type=F32)
    acc_ref[...] = jnp.exp(m_prev - m_next) * acc_ref[...] + pv
    m_ref[...] = m_next

    @pl.when(last_ref[n] == 1)
    def _():
        a = acc_ref[...]
        o = a / pltpu.roll(a, 64, 1)
        for h in range(NSA_HPG):
            o_ref[:, h * 64:(h + 1) * 64] = o[h * tq:(h + 1) * tq, 0:64].astype(o_ref.dtype)


def _sel_attn(q_r, kt, v, sel, b, s, tq, tk):
    m = b * s
    nq = s // tq
    nkb = s // tk
    nsel = s // SEL_BLOCK
    pairs = []
    for qi in range(nq):
        kts = [k for k in range(nkb) if k * tk <= qi * tq + tq - 1]
        pairs += [(qi, k, int(j == 0), int(j == len(kts) - 1)) for j, k in enumerate(kts)]
    tabs = [jnp.asarray(np.array([pr[c] for pr in pairs], np.int32)) for c in range(4)]
    kern = functools.partial(_sel_attn_kernel, tq=tq, tk=tk, nsel=nsel)
    return pl.pallas_call(
        kern,
        grid_spec=pltpu.PrefetchScalarGridSpec(
            num_scalar_prefetch=4,
            grid=(b, NSA_GROUPS, len(pairs)),
            in_specs=[pl.BlockSpec((tq, 256), lambda bb, g, n, qt, kt_, f, l: (bb * nq + qt[n], g)),
                      pl.BlockSpec((None, 64, tk), lambda bb, g, n, qt, kt_, f, l: (bb, g, kt_[n])),
                      pl.BlockSpec((None, tk, 128), lambda bb, g, n, qt, kt_, f, l: (g, bb * nkb + kt_[n], 0)),
                      pl.BlockSpec((None, tq, nsel), lambda bb, g, n, qt, kt_, f, l: (g, bb * nq + qt[n], 0))],
            out_specs=pl.BlockSpec((tq, 256), lambda bb, g, n, qt, kt_, f, l: (bb * nq + qt[n], g)),
            scratch_shapes=[pltpu.VMEM((NSA_HPG * tq, 64), BF16), pltpu.VMEM((NSA_HPG * tq, 128), F32),
                            pltpu.VMEM((NSA_HPG * tq, 128), F32)]),
        out_shape=jax.ShapeDtypeStruct((m, 512), BF16),
        compiler_params=_cp("parallel", "parallel", "arbitrary"),
        name="nsa_sel_attn",
    )(*tabs, q_r, kt, v, sel)


def _win_attn_kernel(q_ref, *refs, tq, nblk):
    k_refs, v_refs, o_ref = refs[:nblk], refs[nblk:2 * nblk], refs[2 * nblk]
    qi = pl.program_id(2)
    q4 = jnp.concatenate([q_ref[:, h * 64:(h + 1) * 64] for h in range(NSA_HPG)], axis=0)
    kt = jnp.concatenate([r[...] for r in k_refs], axis=1)
    v = jnp.concatenate([r[...] for r in v_refs], axis=0)
    kpos = (qi - (nblk - 1)) * tq + lax.broadcasted_iota(jnp.int32, (1, nblk * tq), 1)
    dist = qi * tq + lax.broadcasted_iota(jnp.int32, (tq, 1), 0) - kpos
    bias = jnp.where((dist >= 0) & (dist < WINDOW) & (kpos >= 0), 0.0, NEG_BIG)
    s = jnp.dot(q4, kt, preferred_element_type=F32)
    s = (s.reshape(NSA_HPG, tq, nblk * tq) + bias[None]).reshape(NSA_HPG * tq, nblk * tq)
    p = jnp.exp(s - jnp.max(s, axis=-1, keepdims=True))
    pv = jnp.dot(p.astype(BF16), v, preferred_element_type=F32)
    o = pv / pltpu.roll(pv, 64, 1)
    for h in range(NSA_HPG):
        o_ref[:, h * 64:(h + 1) * 64] = o[h * tq:(h + 1) * tq, 0:64].astype(o_ref.dtype)


def _win_attn(q_r, kt, v, b, s, tq):
    m = b * s
    nq = s // tq
    nblk = WINDOW // tq + 1
    kidx = lambda qi, j: jnp.maximum(qi - (nblk - 1) + j, 0)
    k_specs = [pl.BlockSpec((None, 64, tq), functools.partial(lambda bb, g, qi, j: (bb, g, kidx(qi, j)), j=j))
               for j in range(nblk)]
    v_specs = [pl.BlockSpec((None, tq, 128), functools.partial(lambda bb, g, qi, j: (g, bb * nq + kidx(qi, j), 0), j=j))
               for j in range(nblk)]
    kern = functools.partial(_win_attn_kernel, tq=tq, nblk=nblk)
    return pl.pallas_call(
        kern,
        grid=(b, NSA_GROUPS, nq),
        in_specs=[pl.BlockSpec((tq, 256), lambda bb, g, qi: (bb * nq + qi, g))] + k_specs + v_specs,
        out_specs=pl.BlockSpec((tq, 256), lambda bb, g, qi: (bb * nq + qi, g)),
        out_shape=jax.ShapeDtypeStruct((m, 512), BF16),
        compiler_params=_cp("parallel", "parallel", "parallel"),
        name="nsa_win_attn",
    )(q_r, *([kt] * nblk), *([v] * nblk))


def _consts_recurrent():
    bd_mean = np.kron(np.eye(4), np.full((64, 64), 1.0 / 64))
    bd_ones = np.kron(np.eye(4), np.ones((64, 64)))
    return jnp.asarray(bd_mean, BF16), jnp.asarray(bd_ones, BF16)


def _group_rmsnorm(o, bd_mean, g):
    ms = _dot_ls(o * o, bd_mean, 2)
    return o * lax.rsqrt(ms + EPS) * g


def _hgrn_kernel(q_ref, f_ref, i_ref, g_ref, lbl_ref, on_ref, bdm_ref, bdo_ref, selp_ref,
                 o_ref, st_ref, z_ref, *, layer, depth, t):
    @pl.when(pl.program_id(1) == 0)
    def _():
        st_ref[...] = jnp.zeros(st_ref.shape, F32)

    lg = lbl_ref[...]
    mx = jnp.max(lg, axis=0, keepdims=True)
    ex = jnp.exp(lg - mx)
    pr = ex / jnp.sum(ex, axis=0, keepdims=True)
    cs = pr[0:1, :]
    for r in range(1, layer + 1):
        cs = cs + pr[r:r + 1, :]
    lb = cs - pr[0:1, :]

    q = _silu(q_ref[...])
    fz = f_ref[...]
    f = lb + (1.0 - lb) * _sigmoid(fz)
    logf = jnp.log(jnp.maximum(f, TINY))
    k = (1.0 - lb) * _sigmoid(-fz)
    v = i_ref[...]
    b = _chunk_cumsum(logf)

    bdm = _bd_mask()
    bdo = bdo_ref[...]
    nsub = t // SUB
    tl = lax.broadcasted_iota(jnp.int32, (1, SUB, 1), 1)
    for hp in range(2):
        ls = slice(hp * 128, (hp + 1) * 128)
        q3 = q[:, ls].reshape(nsub, SUB, 128)
        k3 = k[:, ls].reshape(nsub, SUB, 128)
        b3 = b[:, ls].reshape(nsub, SUB, 128)
        for sl in range(SUB):
            msk = tl >= sl
            e = jnp.exp(jnp.where(msk, b3 - b3[:, sl:sl + 1, :], 0.0))
            zz = jnp.where(msk, q3 * k3[:, sl:sl + 1, :] * e, 0.0)
            z_ref[hp * t:(hp + 1) * t, sl * 128:(sl + 1) * 128] = zz.reshape(t, 128)
    a_pairs = _dot_ls(z_ref[...], selp_ref[...], 2)
    a_all = jnp.concatenate([a_pairs[0:t], a_pairs[t:2 * t]], axis=1)
    trow = lax.broadcasted_iota(jnp.int32, (CHUNK, 256), 0)
    scol = lax.broadcasted_iota(jnp.int32, (CHUNK, 256), 1) % CHUNK
    diag_mask = (trow // SUB) == (scol // SUB)
    m1 = (trow >= 32) & (scol < 32)
    m2 = ((trow >= 16) & (trow < 32) & (scol < 16)) | ((trow >= 48) & (scol >= 32) & (scol < 48))
    rowi = lax.broadcasted_iota(jnp.int32, (CHUNK, 1), 0)

    chunks = [slice(c * CHUNK, (c + 1) * CHUNK) for c in range(t // CHUNK)]
    lvl1, lvl2 = [], []
    for rs in chunks:
        qc, kc, bc = q[rs], k[rs], b[rs]
        r1 = bc[31:32, :]
        q1 = qc * jnp.exp(jnp.minimum(bc - r1, 0.0))
        k1 = jnp.where(rowi < 32, kc * jnp.exp(jnp.minimum(r1 - bc, 0.0)), 0.0)
        r2 = jnp.where(rowi < 32, bc[15:16, :], bc[47:48, :])
        q2 = qc * jnp.exp(jnp.minimum(bc - r2, 0.0))
        k2 = kc * jnp.exp(jnp.minimum(r2 - bc, 0.0))
        lvl1.append(_dot_nt(q1, _block_diag4(k1, bdo)))
        lvl2.append(_dot_nt(q2, _block_diag4(k2, bdo)))
    o_intra, s_add, s_dec, q_dec = [], [], [], []
    for rs, a1, a2 in zip(chunks, lvl1, lvl2):
        kc, vc, bc = k[rs], v[rs], b[rs]
        attn = jnp.where(diag_mask, a_all[rs], 0.0) + jnp.where(m1, a1, 0.0) + jnp.where(m2, a2, 0.0)
        o_intra.append(_dot(attn, _block_diag4(vc, bdo)))
        bl = bc[CHUNK - 1:CHUNK, :]
        kdt = (kc * jnp.exp(bl - bc)).T
        s_add.append(jnp.where(bdm, _dot(kdt, vc), 0.0))
        s_dec.append(_as_column(jnp.exp(bl)))
        q_dec.append((q[rs] * jnp.exp(bc)).astype(BF16))

    st = st_ref[...]
    outs = []
    for oi, sa, sd, qd in zip(o_intra, s_add, s_dec, q_dec):
        outs.append(oi + jnp.dot(qd, st.astype(BF16), preferred_element_type=F32))
        st = sd * st + sa
    st_ref[...] = st
    o = jnp.concatenate(outs, axis=0)
    o_ref[...] = (_group_rmsnorm(o, bdm_ref[...], on_ref[...]) * _sigmoid(g_ref[...])).astype(BF16)


def _hgrn(z, lb_logits, out_norm, layer, b, s, t):
    m = b * s
    nt = s // t
    depth = lb_logits.shape[0]
    bd_mean, bd_ones = _consts_recurrent()
    selp = np.zeros((SUB, 2, 64, 2, CHUNK), np.float32)
    for sl in range(SUB):
        for h2 in range(2):
            selp[sl, h2, :, h2, sl::SUB] = 1.0
    selp = jnp.asarray(selp.reshape(SUB * 128, 128), BF16)
    col = COLB_HG // 256
    c2 = lambda bb, i: (0, 0)
    kern = functools.partial(_hgrn_kernel, layer=layer, depth=depth, t=t)
    return pl.pallas_call(
        kern,
        grid=(b, nt),
        in_specs=[pl.BlockSpec((t, 256), lambda bb, i: (bb * nt + i, col)),
                  pl.BlockSpec((t, 256), lambda bb, i: (bb * nt + i, col + 1)),
                  pl.BlockSpec((t, 256), lambda bb, i: (bb * nt + i, col + 2)),
                  pl.BlockSpec((t, 256), lambda bb, i: (bb * nt + i, col + 3)),
                  pl.BlockSpec((depth, 256), c2),
                  pl.BlockSpec((1, 256), c2),
                  pl.BlockSpec((256, 256), c2),
                  pl.BlockSpec((256, 256), c2),
                  pl.BlockSpec((SUB * 128, 128), c2)],
        out_specs=pl.BlockSpec((t, 256), lambda bb, i: (bb * nt + i, 0)),
        out_shape=jax.ShapeDtypeStruct((m, 256), BF16),
        scratch_shapes=[pltpu.VMEM((256, 256), F32), pltpu.VMEM((2 * t, SUB * 128), F32)],
        compiler_params=_cp("parallel", "arbitrary"),
        name="hgrn2",
    )(z, z, z, z, lb_logits, jnp.tile(out_norm, 4).reshape(1, 256), bd_mean, bd_ones, selp)


def _gdn_kernel(q_ref, k_ref, v_ref, z_ref, ba_ref, cw_ref, al_ref, dt_ref, on_ref, bdm_ref, bdo_ref,
                eb_ref, ea_ref, o_ref, st_ref, prev_ref, *, t):
    @pl.when(pl.program_id(1) == 0)
    def _():
        st_ref[...] = jnp.zeros(st_ref.shape, F32)
        prev_ref[...] = jnp.zeros(prev_ref.shape, F32)

    def conv_silu(x_ref, p):
        x = x_ref[...]
        xp = jnp.concatenate([prev_ref[:, p * 256:(p + 1) * 256], x], axis=0)
        w = cw_ref[:, p * 256:(p + 1) * 256]
        y = w[3:4, :] * xp[8:, :]
        for j in range(1, 4):
            y = y + w[3 - j:4 - j, :] * pltpu.roll(xp, j, 0)[8:, :]
        return _silu(y), x[t - 8:, :]

    qa, qtail = conv_silu(q_ref, 0)
    ka, ktail = conv_silu(k_ref, 1)
    va, vtail = conv_silu(v_ref, 2)
    prev_ref[:, 0:256] = qtail
    prev_ref[:, 256:512] = ktail
    prev_ref[:, 512:768] = vtail

    bdo = bdo_ref[...]
    ss = _dot_ls(jnp.concatenate([qa * qa, ka * ka], axis=0), bdo, 2)
    q = qa * lax.rsqrt(ss[0:t] + EPS) * (HEAD_DIM ** -0.5)
    k = ka * lax.rsqrt(ss[t:2 * t] + EPS)
    v = va

    gl = ba_ref[...].astype(F32)
    beta = _dot_ls(_sigmoid(gl), eb_ref[...], 2)
    xs = gl + dt_ref[...]
    softplus = jnp.maximum(xs, 0.0) + jnp.log(1.0 + jnp.exp(-jnp.abs(xs)))
    b = _dot_ls(_chunk_cumsum(-jnp.exp(al_ref[...]) * softplus), ea_ref[...], 3)

    bdm = _bd_mask()
    trow = lax.broadcasted_iota(jnp.int32, (CHUNK, 256), 0)
    scol = lax.broadcasted_iota(jnp.int32, (CHUNK, 256), 1) % CHUNK
    incl = scol <= trow
    strict = scol < trow
    eye = jnp.where(scol == trow, 1.0, 0.0)

    chunks = [slice(c * CHUNK, (c + 1) * CHUNK) for c in range(t // CHUNK)]
    nch = len(chunks)
    aw, aq = [], []
    for rs in chunks:
        qc, kc, bc = q[rs], k[rs], b[rs]
        brow = jnp.concatenate([bc[:, h * 64:(h + 1) * 64].T for h in range(N_HEADS4)], axis=1)
        e = jnp.exp(jnp.where(incl, bc - brow, 0.0))
        gram = _dot_nt(jnp.concatenate([kc * beta[rs], qc], axis=0), _block_diag4(kc, bdo))
        aw.append(jnp.where(strict, gram[:CHUNK] * e, 0.0))
        aq.append(jnp.where(incl, gram[CHUNK:] * e, 0.0).astype(BF16))
    pw = [None] * nch
    tinv = [None] * nch
    for c in range(nch):
        bh, bl_ = _split(-aw[c], 2)
        r1 = _rows_dot([bh, bl_], _block_diag4(bh, bdo))
        pw[c] = r1[0] + r1[1] + jnp.dot(bh, _block_diag4(bl_, bdo), preferred_element_type=F32)
        tinv[c] = eye - aw[c]
    for c in range(nch):
        ph, pl_ = _split(pw[c], 2)
        th, tl_ = _split(tinv[c], 2)
        r1 = _rows_dot([ph, pl_, th, tl_], _block_diag4(ph, bdo))
        r2 = _rows_dot([ph, th], _block_diag4(pl_, bdo))
        pw[c] = r1[0] + r1[1] + r2[0]
        tinv[c] = tinv[c] + r1[2] + r1[3] + r2[1]
    for _ in range(3):
        for c in range(nch):
            ph = pw[c].astype(BF16)
            r1 = _rows_dot([ph, tinv[c].astype(BF16)], _block_diag4(ph, bdo))
            pw[c] = r1[0]
            tinv[c] = tinv[c] + r1[1]
    for c in range(nch):
        tinv[c] = tinv[c] + jnp.dot(tinv[c].astype(BF16), _block_diag4(pw[c], bdo), preferred_element_type=F32)
    u, w = [], []
    for c, rs in enumerate(chunks):
        t2 = _split(tinv[c], 2)
        kbe = k[rs] * beta[rs] * jnp.exp(b[rs])
        r1 = _rows_dot(t2, _block_diag4(v[rs] * beta[rs], bdo))
        r2 = _rows_dot(t2, _block_diag4(kbe, bdo))
        u.append(r1[0] + r1[1])
        w.append(r2[0] + r2[1])
    lhs, s_add, s_dec, o0 = [], [], [], []
    for c, rs in enumerate(chunks):
        bc = b[rs]
        bl = bc[CHUNK - 1:CHUNK, :]
        kdt = (k[rs] * jnp.exp(bl - bc)).T.astype(BF16)
        rn = jnp.dot(kdt, jnp.concatenate([w[c], u[c]], axis=1).astype(BF16), preferred_element_type=F32)
        qeff = q[rs] * jnp.exp(bc) - jnp.dot(aq[c], _block_diag4(w[c], bdo), preferred_element_type=F32)
        lhs.append(jnp.concatenate([qeff.astype(BF16), jnp.where(bdm, rn[:, 0:256], 0.0).astype(BF16)], axis=0))
        s_add.append(jnp.where(bdm, rn[:, 256:512], 0.0))
        s_dec.append(_as_column(jnp.exp(bl)))
        o0.append(jnp.dot(aq[c], _block_diag4(u[c], bdo), preferred_element_type=F32))

    st = st_ref[...]
    outs = []
    for c in range(nch):
        r = jnp.dot(lhs[c], st.astype(BF16), preferred_element_type=F32)
        outs.append(o0[c] + r[0:CHUNK])
        st = s_dec[c] * st - r[CHUNK:] + s_add[c]
    st_ref[...] = st
    o = jnp.concatenate(outs, axis=0)
    o_ref[...] = (_group_rmsnorm(o, bdm_ref[...], on_ref[...]) * _silu(z_ref[...])).astype(BF16)


def _gdn(za, zb, conv_w, a_log, dt_bias, out_norm, b, s, t):
    m = b * s
    nt = s // t
    bd_mean, bd_ones = _consts_recurrent()
    eb = np.zeros((128, 256), np.float32)
    ea = np.zeros((128, 256), np.float32)
    for h in range(N_HEADS4):
        eb[h, h * 64:(h + 1) * 64] = 1.0
        ea[4 + h, h * 64:(h + 1) * 64] = 1.0
    al = jnp.zeros((1, 128), F32).at[0, 4:8].set(a_log)
    dt = jnp.zeros((1, 128), F32).at[0, 4:8].set(dt_bias)
    col = COLB_GD // 256
    c2 = lambda bb, i: (0, 0)
    kern = functools.partial(_gdn_kernel, t=t)
    return pl.pallas_call(
        kern,
        grid=(b, nt),
        in_specs=[pl.BlockSpec((t, 256), lambda bb, i: (bb * nt + i, col)),
                  pl.BlockSpec((t, 256), lambda bb, i: (bb * nt + i, col + 1)),
                  pl.BlockSpec((t, 256), lambda bb, i: (bb * nt + i, col + 2)),
                  pl.BlockSpec((t, 256), lambda bb, i: (bb * nt + i, col + 3)),
                  pl.BlockSpec((t, 128), lambda bb, i: (bb * nt + i, COL_BA // 128)),
                  pl.BlockSpec((4, 768), c2),
                  pl.BlockSpec((1, 128), c2), pl.BlockSpec((1, 128), c2),
                  pl.BlockSpec((1, 256), c2),
                  pl.BlockSpec((256, 256), c2), pl.BlockSpec((256, 256), c2),
                  pl.BlockSpec((128, 256), c2), pl.BlockSpec((128, 256), c2)],
        out_specs=pl.BlockSpec((t, 256), lambda bb, i: (bb * nt + i, 0)),
        out_shape=jax.ShapeDtypeStruct((m, 256), BF16),
        scratch_shapes=[pltpu.VMEM((256, 256), F32), pltpu.VMEM((8, 768), F32)],
        compiler_params=_cp("parallel", "arbitrary"),
        name="gdn",
    )(zb, zb, zb, zb, za, conv_w, al, dt, jnp.tile(out_norm, 4).reshape(1, 256), bd_mean, bd_ones,
      jnp.asarray(eb, BF16), jnp.asarray(ea, BF16))


def _merge_xattn_kernel(x_ref, oc_ref, os_ref, ow_ref, ng_ref, yb_ref, yc_ref, ma_ref, mb_ref, mc_ref,
                        ex_ref, wa_ref, wb_ref, wc_ref, wmix_ref, g_ref, wq_ref, qn_ref, kn_ref, mkv_ref, wo_ref,
                        o_ref):
    sg = _sigmoid(ng_ref[...].astype(F32))
    ya = (_dot_ls(sg, ex_ref[0], 2) * oc_ref[...].astype(F32) + _dot_ls(sg, ex_ref[1], 2) * os_ref[...].astype(F32)
          + _dot_ls(sg, ex_ref[2], 2) * ow_ref[...].astype(F32))
    merged = (_sigmoid(ma_ref[...].astype(F32)) * _dot(ya, wa_ref[...])
              + _sigmoid(mb_ref[...].astype(F32)) * jnp.dot(yb_ref[...], wb_ref[...], preferred_element_type=F32)
              + _sigmoid(mc_ref[...].astype(F32)) * jnp.dot(yc_ref[...], wc_ref[...], preferred_element_type=F32))
    x = x_ref[...] + _dot(merged, wmix_ref[...])

    h = x * lax.rsqrt(jnp.mean(x * x, axis=-1, keepdims=True) + EPS) * g_ref[...]
    q = _dot(h, wq_ref[...])
    scores = []
    for hd in range(N_HEADS4):
        ls = slice(hd * 128, (hd + 1) * 128)
        qh = q[:, ls]
        qh = qh * lax.rsqrt(jnp.mean(qh * qh, axis=-1, keepdims=True) + EPS) * qn_ref[...]
        kh = mkv_ref[:, ls]
        kh = kh * lax.rsqrt(jnp.mean(kh * kh, axis=-1, keepdims=True) + EPS) * kn_ref[...]
        scores.append(_dot_nt(qh, kh) * (XATTN_HEAD_DIM ** -0.5))
    outs = []
    for hd, s in enumerate(scores):
        e = jnp.exp(s - jnp.max(s, axis=-1, keepdims=True))
        pv = _dot(e, mkv_ref[:, 512 + hd * 128:512 + (hd + 1) * 128])
        outs.append(pv / jnp.sum(e, axis=-1, keepdims=True))
    o = jnp.concatenate(outs, axis=1)
    o_ref[...] = x + _dot(o, wo_ref[...])


def _merge_xattn(x2, o_cmp, o_sel, o_win, z, yb, yc, wa, wb, wc, wmix, g, wq, qn, kn, mkv, wo, s, tm):
    m, d = x2.shape
    nt = s // tm
    ex = np.zeros((3, 128, 512), np.float32)
    for c in range(3):
        for h in range(NSA_HEADS):
            ex[c, c * NSA_HEADS + h, h * 64:(h + 1) * 64] = 1.0
    row = lambda i: (i, 0)
    c2 = lambda i: (0, 0)
    return pl.pallas_call(
        _merge_xattn_kernel,
        grid=(m // tm,),
        in_specs=[pl.BlockSpec((tm, d), row),
                  pl.BlockSpec((tm, 512), row), pl.BlockSpec((tm, 512), row), pl.BlockSpec((tm, 512), row),
                  pl.BlockSpec((tm, 128), lambda i: (i, COL_NG // 128)),
                  pl.BlockSpec((tm, 256), row), pl.BlockSpec((tm, 256), row),
                  pl.BlockSpec((tm, 1024), lambda i: (i, 0)),
                  pl.BlockSpec((tm, 1024), lambda i: (i, 1)),
                  pl.BlockSpec((tm, 1024), lambda i: (i, 2)),
                  pl.BlockSpec((3, 128, 512), lambda i: (0, 0, 0)),
                  pl.BlockSpec((512, d), c2), pl.BlockSpec((256, d), c2), pl.BlockSpec((256, d), c2),
                  pl.BlockSpec((d, d), c2),
                  pl.BlockSpec((1, d), c2),
                  pl.BlockSpec((d, 512), c2),
                  pl.BlockSpec((1, 128), c2), pl.BlockSpec((1, 128), c2),
                  pl.BlockSpec((N_MEM, 1024), lambda i: (i // nt, 0)),
                  pl.BlockSpec((512, d), c2)],
        out_specs=pl.BlockSpec((tm, d), row),
        out_shape=jax.ShapeDtypeStruct((m, d), F32),
        compiler_params=_cp("parallel"),
        name="merge_xattn",
    )(x2, o_cmp, o_sel, o_win, z, yb, yc, z, z, z, jnp.asarray(ex, BF16), wa, wb, wc, wmix,
      g.reshape(1, d), wq, qn.reshape(1, 128), kn.reshape(1, 128), mkv, wo)


FFN_HALO = 16


def _ffn_kernel(x_ref, xh_ref, g_ref, wu_ref, cw_ref, wd_ref, o_ref, *, nt, sub):
    def norm(x):
        return x * lax.rsqrt(jnp.mean(x * x, axis=-1, keepdims=True) + EPS) * g_ref[...]

    x = x_ref[...]
    first = (pl.program_id(0) % nt) == 0
    h = jnp.concatenate([jnp.where(first, 0.0, norm(xh_ref[...])).astype(BF16), norm(x).astype(BF16)], axis=0)

    def up(c):
        return tuple(jnp.dot(h, wu_ref[:, o + c * sub:o + (c + 1) * sub], preferred_element_type=F32)
                     for o in (0, D_FF))

    def conv(u, o, c):
        cw = cw_ref[:, o + c * sub:o + (c + 1) * sub]
        y = cw[2:3, :] * u + cw[1:2, :] * pltpu.roll(u, 1, 0) + cw[0:1, :] * pltpu.roll(u, 2, 0)
        return y[FFN_HALO:, :]

    nsub = D_FF // sub
    acc = x
    pending = up(0)
    for c in range(nsub):
        ua, ub = pending
        if c + 1 < nsub:
            pending = up(c + 1)
        act = _silu(conv(ua, 0, c)) * conv(ub, D_FF, c)
        acc = acc + jnp.dot(act.astype(BF16), wd_ref[c * sub:(c + 1) * sub, :], preferred_element_type=F32)
    o_ref[...] = acc


def _ffn(x2, g, w_up, conv_w, w_down, s, tm, sub):
    m, d = x2.shape
    nt = s // tm
    hb = tm // FFN_HALO
    c2 = lambda i: (0, 0)
    resident = dict(pipeline_mode=pl.Buffered(1))
    kern = functools.partial(_ffn_kernel, nt=nt, sub=sub)
    return pl.pallas_call(
        kern,
        grid=(m // tm,),
        in_specs=[pl.BlockSpec((tm, d), lambda i: (i, 0)),
                  pl.BlockSpec((FFN_HALO, d), lambda i: (jnp.maximum(i * hb - 1, 0), 0)),
                  pl.BlockSpec((1, d), c2),
                  pl.BlockSpec((d, 2 * D_FF), c2, **resident),
                  pl.BlockSpec((3, 2 * D_FF), c2),
                  pl.BlockSpec((D_FF, d), c2, **resident)],
        out_specs=pl.BlockSpec((tm, d), lambda i: (i, 0)),
        out_shape=jax.ShapeDtypeStruct((m, d), F32),
        compiler_params=_cp("parallel"),
        name="conv_glu_ffn",
    )(x2, x2, g.reshape(1, d), w_up, conv_w, w_down)


def _permute_w_in(w):
    d = w.shape[0]
    gate = w[:, 1280:1304].reshape(d, NSA_HEADS, 3).transpose(0, 2, 1).reshape(d, 24)
    z104 = jnp.zeros((d, 104), w.dtype)
    z120 = jnp.zeros((d, 120), w.dtype)
    wa = jnp.concatenate([w[:, 3360:6432], w[:, 0:1280], gate, z104, w[:, 3352:3360], z120], axis=1)
    return wa.astype(BF16), w[:, 1304:3352].astype(BF16)


def _rope_tables(s):
    inv_freq = (1.0 / (ROPE_THETA ** (np.arange(0, HEAD_DIM, 2, dtype=np.float32) / HEAD_DIM))).astype(np.float32)
    ang = np.arange(s, dtype=np.float32)[:, None] * inv_freq[None, :]
    c, sn = np.cos(ang), np.sin(ang)
    return (jnp.asarray(np.tile(np.concatenate([c, c], axis=1), (1, 2)), F32),
            jnp.asarray(np.tile(np.concatenate([-sn, sn], axis=1), (1, 2)), F32))


def kernel(x, mem, mem_norm, mem_w_kv, hgrn_lb_logits, norm_mix, w_in, nsa_q_norm, nsa_k_norm, cmp_pos_k, cmp_pos_v, cmp_k_w1, cmp_k_w2, cmp_v_w1, cmp_v_w2, hgrn_out_norm, gdn_conv, gdn_a_log, gdn_dt_bias, gdn_out_norm, w_branch_a, w_branch_b, w_branch_c, w_mix_out, norm_cross, xattn_wq, xattn_q_norm, xattn_k_norm, xattn_wo, norm_ffn, ffn_w_up, ffn_conv, ffn_w_down):
    b, s, d = x.shape
    m = b * s
    depth = w_in.shape[0]
    cos_t, sin_t = _rope_tables(s)
    x2 = x.reshape(m, d)
    mkv = _norm_matmul(mem.reshape(b * N_MEM, d), mem_norm, mem_w_kv.astype(BF16), N_MEM, 512)

    tm_in = min(1024, m)
    for l in range(depth):
        w_a, w_b = _permute_w_in(w_in[l])
        z = _norm_matmul(x2, norm_mix[l], w_a, tm_in, WIDTH_A // 3, BF16)
        zb = _norm_matmul(x2, norm_mix[l], w_b, tm_in, WIDTH_B // 2)
        q_r, kcmp, vcmp, kst, kwt, vs, vw = _nsa_prep(z, cos_t, sin_t, nsa_q_norm[l], nsa_k_norm[l], b, s, 512)
        rk = kcmp.reshape(b, s // CMP_STRIDE, CMP_STRIDE * 128)
        rv = vcmp.reshape(b, s // CMP_STRIDE, CMP_STRIDE * 128)
        kct, vc = _compress(rk, rv, cmp_pos_k[l], cmp_pos_v[l], cmp_k_w1[l], cmp_k_w2[l], cmp_v_w1[l], cmp_v_w2[l])
        o_cmp, sel = _cmp_attn(q_r, kct, vc, b, s, 256)
        o_sel = _sel_attn(q_r, kst, vs, sel, b, s, 512, 512)
        o_win = _win_attn(q_r, kwt, vw, b, s, 256)
        yb = _hgrn(zb, hgrn_lb_logits, hgrn_out_norm[l], l, b, s, 512)
        yc = _gdn(z, zb, gdn_conv[l], gdn_a_log[l], gdn_dt_bias[l], gdn_out_norm[l], b, s, 512)
        x2 = _merge_xattn(x2, o_cmp, o_sel, o_win, z, yb, yc, w_branch_a[l].astype(BF16),
                          w_branch_b[l].astype(BF16), w_branch_c[l].astype(BF16), w_mix_out[l].astype(BF16),
                          norm_cross[l], xattn_wq[l].astype(BF16), xattn_q_norm[l], xattn_k_norm[l], mkv,
                          xattn_wo[l].astype(BF16), s, 256)
        x2 = _ffn(x2, norm_ffn[l], ffn_w_up[l].astype(BF16), ffn_conv[l], ffn_w_down[l].astype(BF16), s, 512, 256)
    return x2.reshape(b, s, d)
```

```python
import functools
import math

import numpy as np
import jax
import jax.numpy as jnp
from jax import lax
from jax.experimental import pallas as pl
from jax.experimental.pallas import tpu as pltpu

F32 = jnp.float32
BF16 = jnp.bfloat16

EPS = 1e-6
ROPE_THETA = 10000.0
NEG_BIG = -1e30
TINY = 1e-20
FORCE_SCORE = 1e6
SEL_MASK = 2.0 ** 30

D_MODEL = 1024
N_MEM = 256
HEAD_DIM = 64
NSA_HEADS = 8
NSA_GROUPS = 2
NSA_HPG = 4
CMP_BLOCK = 32
CMP_STRIDE = 16
CMP_HIDDEN = 128
SEL_BLOCK = 64
SEL_TOPK = 16
WINDOW = 512
N_HEADS4 = 4
CHUNK = 64
SUB = 16
XATTN_HEAD_DIM = 128
D_FF = 2816

COL_M = 0
COL_NQ = 3072
COL_KV = 3584
COL_NG = 4352
COL_BA = 4480
WIDTH_A = 4608
COLB_HG = 0
COLB_GD = 1024
WIDTH_B = 2048

VMEM_LIMIT = 48 * 1024 * 1024


def _cp(*sem):
    return pltpu.CompilerParams(dimension_semantics=sem, vmem_limit_bytes=VMEM_LIMIT)


def _dot(a, b):
    return jnp.dot(a.astype(BF16), b.astype(BF16), preferred_element_type=F32)


def _dot_nt(a, b):
    return lax.dot_general(a.astype(BF16), b.astype(BF16), (((1,), (1,)), ((), ())),
                           preferred_element_type=F32)


def _split(a, n):
    parts = []
    r = a
    for _ in range(n):
        p = r.astype(BF16)
        parts.append(p)
        r = r - p.astype(F32)
    return parts


def _rows_dot(blocks, rhs):
    if len(blocks) == 1:
        return [jnp.dot(blocks[0], rhs, preferred_element_type=F32)]
    r = jnp.dot(jnp.concatenate(blocks, axis=0), rhs, preferred_element_type=F32)
    out, o = [], 0
    for blk in blocks:
        out.append(r[o:o + blk.shape[0]])
        o += blk.shape[0]
    return out


def _dot_ls(a, b_exact, n=2):
    parts = _rows_dot(_split(a, n), b_exact)
    acc = parts[0]
    for p in parts[1:]:
        acc = acc + p
    return acc


def _chunk_cumsum(x):
    row = lax.broadcasted_iota(jnp.int32, (x.shape[0], 1), 0) % CHUNK
    sh = 1
    while sh < CHUNK:
        x = x + jnp.where(row >= sh, pltpu.roll(x, sh, 0), 0.0)
        sh *= 2
    return x


def _as_column(row):
    return jnp.broadcast_to(row, (8, row.shape[1])).T[:, 0:1]


def _sigmoid(x):
    return 1.0 / (1.0 + jnp.exp(-x))


def _silu(x):
    return x * _sigmoid(x)


def _block_diag4(y, bd_ones):
    yb = y.astype(BF16)
    return jnp.where(bd_ones > 0, jnp.concatenate([yb, yb, yb, yb], axis=0), jnp.zeros((), BF16))


def _bd_mask():
    r = lax.broadcasted_iota(jnp.int32, (256, 256), 0) // 64
    c = lax.broadcasted_iota(jnp.int32, (256, 256), 1) // 64
    return r == c


def _norm_matmul_kernel(x_ref, g_ref, w_ref, o_ref, h_ref):
    @pl.when(pl.program_id(1) == 0)
    def _():
        x = x_ref[...]
        ms = jnp.mean(x * x, axis=-1, keepdims=True)
        h_ref[...] = (x * lax.rsqrt(ms + EPS) * g_ref[...]).astype(BF16)

    o_ref[...] = jnp.dot(h_ref[...], w_ref[...], preferred_element_type=F32).astype(o_ref.dtype)


def _norm_matmul(x, g, w, tm, tn, out_dtype=F32):
    m, d = x.shape
    n = w.shape[1]
    return pl.pallas_call(
        _norm_matmul_kernel,
        grid=(m // tm, n // tn),
        in_specs=[pl.BlockSpec((tm, d), lambda i, j: (i, 0)),
                  pl.BlockSpec((1, d), lambda i, j: (0, 0)),
                  pl.BlockSpec((d, tn), lambda i, j: (0, j))],
        out_specs=pl.BlockSpec((tm, tn), lambda i, j: (i, j)),
        out_shape=jax.ShapeDtypeStruct((m, n), out_dtype),
        scratch_shapes=[pltpu.VMEM((tm, d), BF16)],
        compiler_params=_cp("parallel", "arbitrary"),
        name="norm_matmul",
    )(x, g.reshape(1, d), w)


def _head_norm(x, bd, g):
    ms = _dot_ls(x * x, bd, 2)
    return x * lax.rsqrt(ms + EPS) * g


def _rope(x, c, s):
    w = x.shape[1]
    lane = lax.broadcasted_iota(jnp.int32, x.shape, 1)
    sw = jnp.where((lane & 32) != 0, pltpu.roll(x, 32, 1), pltpu.roll(x, w - 32, 1))
    return x * c + sw * s


def _nsa_prep_kernel(q_ref, kvc_ref, kvs_ref, kvw_ref, cos_ref, sin_ref, qn_ref, kn_ref, bd512_ref, bd128_ref,
                     qo_ref, kc_ref, vc_ref, kst_ref, kwt_ref, vs_ref, vw_ref):
    c = cos_ref[...]
    s = sin_ref[...]
    c4 = jnp.concatenate([c, c, c, c], axis=1)
    s4 = jnp.concatenate([s, s, s, s], axis=1)
    q = _rope(_head_norm(q_ref[...].astype(F32), bd512_ref[...], qn_ref[...]), c4, s4)
    qo_ref[...] = (q * (HEAD_DIM ** -0.5)).astype(BF16)

    def key(ref, row):
        return _rope(_head_norm(ref[:, 0:128].astype(F32), bd128_ref[...], kn_ref[row:row + 1, :]), c, s)

    kc_ref[...] = key(kvc_ref, 0)
    vc_ref[...] = kvc_ref[:, 128:256].astype(F32)
    kst_ref[...] = key(kvs_ref, 1).T.astype(BF16)
    kwt_ref[...] = key(kvw_ref, 2).T.astype(BF16)
    vs = kvs_ref[:, 128:256].astype(BF16)
    vw = kvw_ref[:, 128:256].astype(BF16)
    ones = jnp.ones((vs.shape[0], 64), BF16)
    for g in range(NSA_GROUPS):
        vs_ref[g] = jnp.concatenate([vs[:, g * 64:(g + 1) * 64], ones], axis=1)
        vw_ref[g] = jnp.concatenate([vw[:, g * 64:(g + 1) * 64], ones], axis=1)


def _nsa_prep(z, cos_t, sin_t, q_norm, k_norm, b, s, tm):
    m = b * s
    nt = s // tm
    bd512 = jnp.asarray(np.kron(np.eye(8), np.full((64, 64), 1.0 / 64)), BF16)
    bd128 = jnp.asarray(np.kron(np.eye(2), np.full((64, 64), 1.0 / 64)), BF16)
    qn = jnp.tile(q_norm, 8).reshape(1, 512)
    kn = jnp.tile(k_norm, (1, 2))
    row = lambda i: (i, 0)
    const = lambda i: (0, 0)
    return pl.pallas_call(
        _nsa_prep_kernel,
        grid=(m // tm,),
        in_specs=[pl.BlockSpec((tm, 512), lambda i: (i, COL_NQ // 512)),
                  pl.BlockSpec((tm, 256), lambda i: (i, COL_KV // 256)),
                  pl.BlockSpec((tm, 256), lambda i: (i, COL_KV // 256 + 1)),
                  pl.BlockSpec((tm, 256), lambda i: (i, COL_KV // 256 + 2)),
                  pl.BlockSpec((tm, 128), lambda i: (i % nt, 0)),
                  pl.BlockSpec((tm, 128), lambda i: (i % nt, 0)),
                  pl.BlockSpec((1, 512), const),
                  pl.BlockSpec((3, 128), const),
                  pl.BlockSpec((512, 512), const),
                  pl.BlockSpec((128, 128), const)],
        out_specs=[pl.BlockSpec((tm, 512), row),
                   pl.BlockSpec((tm, 128), row),
                   pl.BlockSpec((tm, 128), row),
                   pl.BlockSpec((None, 128, tm), lambda i: (i // nt, 0, i % nt)),
                   pl.BlockSpec((None, 128, tm), lambda i: (i // nt, 0, i % nt)),
                   pl.BlockSpec((NSA_GROUPS, tm, 128), lambda i: (0, i, 0)),
                   pl.BlockSpec((NSA_GROUPS, tm, 128), lambda i: (0, i, 0))],
        out_shape=[jax.ShapeDtypeStruct((m, 512), BF16),
                   jax.ShapeDtypeStruct((m, 128), F32),
                   jax.ShapeDtypeStruct((m, 128), F32),
                   jax.ShapeDtypeStruct((b, 128, s), BF16),
                   jax.ShapeDtypeStruct((b, 128, s), BF16),
                   jax.ShapeDtypeStruct((NSA_GROUPS, m, 128), BF16),
                   jax.ShapeDtypeStruct((NSA_GROUPS, m, 128), BF16)],
        compiler_params=_cp("parallel"),
        name="nsa_prep",
    )(z, z, z, z, cos_t, sin_t, qn, kn, bd512, bd128)


def _gelu_tanh(x):
    return 0.5 * x * (1.0 + jnp.tanh(math.sqrt(2.0 / math.pi) * (x + 0.044715 * (x * x * x))))


def _compress_kernel(rk_ref, rv_ref, pk_ref, pv_ref, wkt_ref, wkb_ref, wvt_ref, wvb_ref, wk2_ref, wv2_ref,
                     kct_ref, vc_ref):
    def mlp(r_ref, p_ref, wt_ref, wb_ref, w2_ref):
        r = r_ref[...]
        n = r.shape[0]
        top = _dot(r + p_ref[0:1, :], wt_ref[...])
        bot = _dot(r + p_ref[1:2, :], wb_ref[...])
        hid = top + pltpu.roll(bot, n - 1, 0)
        return _dot(_gelu_tanh(hid), w2_ref[...])

    kc = mlp(rk_ref, pk_ref, wkt_ref, wkb_ref, wk2_ref)
    vc = mlp(rv_ref, pv_ref, wvt_ref, wvb_ref, wv2_ref)
    kct_ref[...] = kc.T
    for g in range(NSA_GROUPS):
        vc_ref[g] = vc[:, g * 64:(g + 1) * 64]


def _compress_weights(pos, w1, w2):
    w1r = w1.reshape(2, 16, 64, CMP_HIDDEN)
    zero = jnp.zeros_like(w1r)
    big = jnp.stack([jnp.stack([w1r, zero], axis=-2), jnp.stack([zero, w1r], axis=-2)], axis=2)
    big = big.reshape(2, 16 * 2 * 64, 2 * CMP_HIDDEN).astype(BF16)
    posr = jnp.broadcast_to(pos.reshape(2, 16, 1, 64), (2, 16, 2, 64)).reshape(2, 2048)
    w2bd = jnp.zeros((2, CMP_HIDDEN, 2, 64), F32)
    w2bd = w2bd.at[0, :, 0, :].set(w2).at[1, :, 1, :].set(w2).reshape(2 * CMP_HIDDEN, 128).astype(BF16)
    return posr, big[0], big[1], w2bd


def _compress(rk, rv, pos_k, pos_v, ck_w1, ck_w2, cv_w1, cv_w2):
    b, n, _ = rk.shape
    pk, wkt, wkb, wk2 = _compress_weights(pos_k, ck_w1, ck_w2)
    pv, wvt, wvb, wv2 = _compress_weights(pos_v, cv_w1, cv_w2)
    c2 = lambda i: (0, 0)
    return pl.pallas_call(
        _compress_kernel,
        grid=(b,),
        in_specs=[pl.BlockSpec((None, n, 2048), lambda i: (i, 0, 0)),
                  pl.BlockSpec((None, n, 2048), lambda i: (i, 0, 0)),
                  pl.BlockSpec((2, 2048), c2), pl.BlockSpec((2, 2048), c2),
                  pl.BlockSpec((2048, 256), c2), pl.BlockSpec((2048, 256), c2),
                  pl.BlockSpec((2048, 256), c2), pl.BlockSpec((2048, 256), c2),
                  pl.BlockSpec((256, 128), c2), pl.BlockSpec((256, 128), c2)],
        out_specs=[pl.BlockSpec((None, 128, n), lambda i: (i, 0, 0)),
                   pl.BlockSpec((None, NSA_GROUPS, n, 64), lambda i: (i, 0, 0, 0))],
        out_shape=[jax.ShapeDtypeStruct((b, 128, n), F32),
                   jax.ShapeDtypeStruct((b, NSA_GROUPS, n, 64), F32)],
        compiler_params=_cp("parallel"),
        name="nsa_compress",
    )(rk, rv, pk, pv, wkt, wkb, wvt, wvb, wk2, wv2)


def _cmp_attn_kernel(q_ref, kct_ref, vc_ref, ovt_ref, o_ref, sel_ref, imp_ref, *, tq, ncp, nsel):
    i = pl.program_id(2)
    tpos = i * tq + lax.broadcasted_iota(jnp.int32, (tq, 1), 0)
    nblk = lax.broadcasted_iota(jnp.int32, (1, ncp), 1)
    bias = jnp.where((nblk * CMP_STRIDE + (CMP_BLOCK - 1)) <= tpos, 0.0, NEG_BIG)
    any_valid = jnp.where(tpos >= CMP_BLOCK - 1, 1.0, 0.0)
    q4 = jnp.concatenate([q_ref[:, h * 64:(h + 1) * 64] for h in range(NSA_HPG)], axis=0)
    s = jnp.dot(q4, kct_ref[...].astype(BF16), preferred_element_type=F32)
    s = s.reshape(NSA_HPG, tq, ncp) + bias[None]
    e = jnp.exp(s - jnp.max(s, axis=-1, keepdims=True))
    p = e * (any_valid / jnp.maximum(jnp.sum(e, axis=-1, keepdims=True), 1e-30))
    o = jnp.dot(p.reshape(NSA_HPG * tq, ncp).astype(BF16), vc_ref[...].astype(BF16), preferred_element_type=F32)
    for h in range(NSA_HPG):
        o_ref[:, h * 64:(h + 1) * 64] = o[h * tq:(h + 1) * tq].astype(o_ref.dtype)
    psum = p[0] + p[1] + p[2] + p[3]
    ph, plo = _split(psum, 2)
    ovt = ovt_ref[...]
    imp = _dot_nt(ovt, ph) + _dot_nt(ovt, plo)
    j = lax.broadcasted_iota(jnp.int32, (nsel, 1), 0)
    cur = (i * tq + lax.broadcasted_iota(jnp.int32, (1, tq), 1)) // SEL_BLOCK
    forced = (j == 0) | (j == cur) | (j == cur - 1)
    val = jnp.where(j <= cur, jnp.where(forced, FORCE_SCORE, imp), -1.0)
    imp_ref[...] = val
    groups = [val[g * 8:(g + 1) * 8] for g in range(nsel // 8)]
    cnt = [jnp.zeros((8, tq), F32) for _ in groups]
    jl = lax.broadcasted_iota(jnp.int32, (8, 1), 0)
    for r in range(nsel):
        vr = imp_ref[r:r + 1, :]
        for g, vg in enumerate(groups):
            if g * 8 > r:
                beats = vr >= vg
            elif g * 8 + 7 < r:
                beats = vr > vg
            else:
                beats = (vr > vg) | ((vr == vg) & (jl + g * 8 > r))
            cnt[g] = cnt[g] + jnp.where(beats, 1.0, 0.0)
    sel = jnp.where(jnp.concatenate(cnt, axis=0) < float(min(SEL_TOPK, nsel)), 1.0, 0.0)
    sel_ref[...] = sel.T.astype(BF16)


def _cmp_attn(q_r, kct, vc, b, s, tq):
    m = b * s
    nq = s // tq
    ncp = kct.shape[2]
    nsel = s // SEL_BLOCK
    n_cmp = (s - CMP_BLOCK) // CMP_STRIDE + 1
    tok = np.arange(n_cmp)[:, None] * CMP_STRIDE + np.arange(CMP_BLOCK)[None, :]
    ov = np.zeros((ncp, nsel), np.float32)
    for l in range(CMP_BLOCK):
        ov[np.arange(n_cmp), tok[:, l] // SEL_BLOCK] += 1.0
    ovt = jnp.asarray(ov.T, BF16)
    kern = functools.partial(_cmp_attn_kernel, tq=tq, ncp=ncp, nsel=nsel)
    return pl.pallas_call(
        kern,
        grid=(b, NSA_GROUPS, nq),
        in_specs=[pl.BlockSpec((tq, 256), lambda bb, g, i: (bb * nq + i, g)),
                  pl.BlockSpec((None, 64, ncp), lambda bb, g, i: (bb, g, 0)),
                  pl.BlockSpec((None, None, ncp, 64), lambda bb, g, i: (bb, g, 0, 0)),
                  pl.BlockSpec((nsel, ncp), lambda bb, g, i: (0, 0))],
        out_specs=[pl.BlockSpec((tq, 256), lambda bb, g, i: (bb * nq + i, g)),
                   pl.BlockSpec((None, tq, nsel), lambda bb, g, i: (g, bb * nq + i, 0))],
        out_shape=[jax.ShapeDtypeStruct((m, 512), BF16),
                   jax.ShapeDtypeStruct((NSA_GROUPS, m, nsel), BF16)],
        scratch_shapes=[pltpu.VMEM((nsel, tq), F32)],
        compiler_params=_cp("parallel", "parallel", "parallel"),
        name="nsa_cmp_attn",
    )(q_r, kct, vc, ovt)


def _sel_attn_kernel(qi_ref, kt_ref, first_ref, last_ref, q_ref, k_ref, v_ref, sel_ref, o_ref, q4_ref, m_ref, acc_ref,
                     *, tq, tk, nsel):
    n = pl.program_id(2)
    qi = qi_ref[n]
    ktile = kt_ref[n]

    @pl.when(first_ref[n] == 1)
    def _():
        unsel = ((sel_ref[...].astype(F32) - 1.0) * SEL_MASK).astype(BF16)
        for h in range(NSA_HPG):
            q4_ref[h * tq:(h + 1) * tq, 0:64] = q_ref[:, h * 64:(h + 1) * 64]
            q4_ref[h * tq:(h + 1) * tq, 64:64 + nsel] = unsel
        m_ref[...] = jnp.full(m_ref.shape, NEG_BIG, F32)
        acc_ref[...] = jnp.zeros(acc_ref.shape, F32)

    def step(with_causal):
        kpos = ktile * tk + lax.broadcasted_iota(jnp.int32, (1, tk), 1)
        blk = lax.broadcasted_iota(jnp.int32, (nsel, 1), 0)
        own_block = jnp.where(blk == kpos // SEL_BLOCK, 1.0, 0.0).astype(BF16)
        s = jnp.dot(q4_ref[...], jnp.concatenate([k_ref[...], own_block], axis=0),
                    preferred_element_type=F32)
        if with_causal:
            qpos = qi * tq + lax.broadcasted_iota(jnp.int32, (tq, 1), 0)
            causal = jnp.where(kpos <= qpos, 0.0, NEG_BIG)
            s = (s.reshape(NSA_HPG, tq, tk) + causal[None]).reshape(NSA_HPG * tq, tk)
        m_prev = m_ref[...]
        m_next = jnp.maximum(m_prev, jnp.max(s, axis=-1, keepdims=True))
        p = jnp.exp(s - jnp.tile(m_next, (1, tk // 128)))
        pv = jnp.dot(p.astype(BF16), v_ref[...], preferred_element_type=F32)
        acc_ref[...] = jnp.exp(m_prev - m_next) * acc_ref[...] + pv
        m_ref[...] = m_next

    reaches_past = ktile * tk + tk - 1 > qi * tq
    pl.when(reaches_past)(functools.partial(step, True))
    pl.when(jnp.logical_not(reaches_past))(functools.partial(step, False))

    @pl.when(last_ref[n] == 1)
    def _():
        a = acc_ref[...]
        o = a / pltpu.roll(a, 64, 1)
        for h in range(NSA_HPG):
            o_ref[:, h * 64:(h + 1) * 64] = o[h * tq:(h + 1) * tq, 0:64].astype(o_ref.dtype)


def _sel_attn(q_r, kt, v, sel, b, s, tq, tk):
    m = b * s
    nq = s // tq
    nkb = s // tk
    nsel = s // SEL_BLOCK
    pairs = []
    for qi in range(nq):
        kts = [k for k in range(nkb) if k * tk <= qi * tq + tq - 1]
        pairs += [(qi, k, int(j == 0), int(j == len(kts) - 1)) for j, k in enumerate(kts)]
    tabs = [jnp.asarray(np.array([pr[c] for pr in pairs], np.int32)) for c in range(4)]
    kern = functools.partial(_sel_attn_kernel, tq=tq, tk=tk, nsel=nsel)
    return pl.pallas_call(
        kern,
        grid_spec=pltpu.PrefetchScalarGridSpec(
            num_scalar_prefetch=4,
            grid=(b, NSA_GROUPS, len(pairs)),
            in_specs=[pl.BlockSpec((tq, 256), lambda bb, g, n, qt, kt_, f, l: (bb * nq + qt[n], g)),
                      pl.BlockSpec((None, 64, tk), lambda bb, g, n, qt, kt_, f, l: (bb, g, kt_[n])),
                      pl.BlockSpec((None, tk, 128), lambda bb, g, n, qt, kt_, f, l: (g, bb * nkb + kt_[n], 0)),
                      pl.BlockSpec((None, tq, nsel), lambda bb, g, n, qt, kt_, f, l: (g, bb * nq + qt[n], 0))],
            out_specs=pl.BlockSpec((tq, 256), lambda bb, g, n, qt, kt_, f, l: (bb * nq + qt[n], g)),
            scratch_shapes=[pltpu.VMEM((NSA_HPG * tq, 64 + nsel), BF16), pltpu.VMEM((NSA_HPG * tq, 128), F32),
                            pltpu.VMEM((NSA_HPG * tq, 128), F32)]),
        out_shape=jax.ShapeDtypeStruct((m, 512), BF16),
        compiler_params=_cp("parallel", "parallel", "arbitrary"),
        name="nsa_sel_attn",
    )(*tabs, q_r, kt, v, sel)


def _win_attn_kernel(q_ref, *refs, tq, nblk):
    k_refs, v_refs, o_ref = refs[:nblk], refs[nblk:2 * nblk], refs[2 * nblk]
    qi = pl.program_id(2)
    q4 = jnp.concatenate([q_ref[:, h * 64:(h + 1) * 64] for h in range(NSA_HPG)], axis=0)
    kt = jnp.concatenate([r[...] for r in k_refs], axis=1)
    v = jnp.concatenate([r[...] for r in v_refs], axis=0)
    kpos = (qi - (nblk - 1)) * tq + lax.broadcasted_iota(jnp.int32, (1, nblk * tq), 1)
    dist = qi * tq + lax.broadcasted_iota(jnp.int32, (tq, 1), 0) - kpos
    bias = jnp.where((dist >= 0) & (dist < WINDOW) & (kpos >= 0), 0.0, NEG_BIG)
    s = jnp.dot(q4, kt, preferred_element_type=F32)
    s = (s.reshape(NSA_HPG, tq, nblk * tq) + bias[None]).reshape(NSA_HPG * tq, nblk * tq)
    p = jnp.exp(s - jnp.max(s, axis=-1, keepdims=True))
    pv = jnp.dot(p.astype(BF16), v, preferred_element_type=F32)
    o = pv / pltpu.roll(pv, 64, 1)
    for h in range(NSA_HPG):
        o_ref[:, h * 64:(h + 1) * 64] = o[h * tq:(h + 1) * tq, 0:64].astype(o_ref.dtype)


def _win_attn(q_r, kt, v, b, s, tq):
    m = b * s
    nq = s // tq
    nblk = WINDOW // tq + 1
    kidx = lambda qi, j: jnp.maximum(qi - (nblk - 1) + j, 0)
    k_specs = [pl.BlockSpec((None, 64, tq), functools.partial(lambda bb, g, qi, j: (bb, g, kidx(qi, j)), j=j))
               for j in range(nblk)]
    v_specs = [pl.BlockSpec((None, tq, 128), functools.partial(lambda bb, g, qi, j: (g, bb * nq + kidx(qi, j), 0), j=j))
               for j in range(nblk)]
    kern = functools.partial(_win_attn_kernel, tq=tq, nblk=nblk)
    return pl.pallas_call(
        kern,
        grid=(b, NSA_GROUPS, nq),
        in_specs=[pl.BlockSpec((tq, 256), lambda bb, g, qi: (bb * nq + qi, g))] + k_specs + v_specs,
        out_specs=pl.BlockSpec((tq, 256), lambda bb, g, qi: (bb * nq + qi, g)),
        out_shape=jax.ShapeDtypeStruct((m, 512), BF16),
        compiler_params=_cp("parallel", "parallel", "parallel"),
        name="nsa_win_attn",
    )(q_r, *([kt] * nblk), *([v] * nblk))


def _consts_recurrent():
    bd_mean = np.kron(np.eye(4), np.full((64, 64), 1.0 / 64))
    bd_ones = np.kron(np.eye(4), np.ones((64, 64)))
    return jnp.asarray(bd_mean, BF16), jnp.asarray(bd_ones, BF16)


def _group_rmsnorm(o, bd_mean, g):
    ms = _dot_ls(o * o, bd_mean, 2)
    return o * lax.rsqrt(ms + EPS) * g


def _hgrn_kernel(q_ref, f_ref, i_ref, g_ref, lbl_ref, on_ref, bdm_ref, bdo_ref, selp_ref,
                 o_ref, st_ref, z_ref, *, layer, depth, t):
    @pl.when(pl.program_id(1) == 0)
    def _():
        st_ref[...] = jnp.zeros(st_ref.shape, F32)

    lg = lbl_ref[...]
    mx = jnp.max(lg, axis=0, keepdims=True)
    ex = jnp.exp(lg - mx)
    pr = ex / jnp.sum(ex, axis=0, keepdims=True)
    cs = pr[0:1, :]
    for r in range(1, layer + 1):
        cs = cs + pr[r:r + 1, :]
    lb = cs - pr[0:1, :]

    q = _silu(q_ref[...])
    fz = f_ref[...]
    f = lb + (1.0 - lb) * _sigmoid(fz)
    logf = jnp.log(jnp.maximum(f, TINY))
    k = (1.0 - lb) * _sigmoid(-fz)
    v = i_ref[...]
    b = _chunk_cumsum(logf)

    bdm = _bd_mask()
    bdo = bdo_ref[...]
    nsub = t // SUB
    tl = lax.broadcasted_iota(jnp.int32, (1, SUB, 1), 1)
    for hp in range(2):
        ls = slice(hp * 128, (hp + 1) * 128)
        q3 = q[:, ls].reshape(nsub, SUB, 128)
        k3 = k[:, ls].reshape(nsub, SUB, 128)
        b3 = b[:, ls].reshape(nsub, SUB, 128)
        for sl in range(SUB):
            msk = tl >= sl
            e = jnp.exp(jnp.where(msk, b3 - b3[:, sl:sl + 1, :], 0.0))
            zz = jnp.where(msk, q3 * k3[:, sl:sl + 1, :] * e, 0.0)
            z_ref[hp * t:(hp + 1) * t, sl * 128:(sl + 1) * 128] = zz.reshape(t, 128)
    a_pairs = _dot_ls(z_ref[...], selp_ref[...], 2)
    a_all = jnp.concatenate([a_pairs[0:t], a_pairs[t:2 * t]], axis=1)
    trow = lax.broadcasted_iota(jnp.int32, (CHUNK, 256), 0)
    scol = lax.broadcasted_iota(jnp.int32, (CHUNK, 256), 1) % CHUNK
    diag_mask = (trow // SUB) == (scol // SUB)
    m1 = (trow >= 32) & (scol < 32)
    m2 = ((trow >= 16) & (trow < 32) & (scol < 16)) | ((trow >= 48) & (scol >= 32) & (scol < 48))
    rowi = lax.broadcasted_iota(jnp.int32, (CHUNK, 1), 0)

    chunks = [slice(c * CHUNK, (c + 1) * CHUNK) for c in range(t // CHUNK)]
    lvl1, lvl2 = [], []
    for rs in chunks:
        qc, kc, bc = q[rs], k[rs], b[rs]
        r1 = bc[31:32, :]
        q1 = qc * jnp.exp(jnp.minimum(bc - r1, 0.0))
        k1 = jnp.where(rowi < 32, kc * jnp.exp(jnp.minimum(r1 - bc, 0.0)), 0.0)
        r2 = jnp.where(rowi < 32, bc[15:16, :], bc[47:48, :])
        q2 = qc * jnp.exp(jnp.minimum(bc - r2, 0.0))
        k2 = kc * jnp.exp(jnp.minimum(r2 - bc, 0.0))
        lvl1.append(_dot_nt(q1, _block_diag4(k1, bdo)))
        lvl2.append(_dot_nt(q2, _block_diag4(k2, bdo)))
    o_intra, s_add, s_dec, q_dec = [], [], [], []
    for rs, a1, a2 in zip(chunks, lvl1, lvl2):
        kc, vc, bc = k[rs], v[rs], b[rs]
        attn = jnp.where(diag_mask, a_all[rs], 0.0) + jnp.where(m1, a1, 0.0) + jnp.where(m2, a2, 0.0)
        o_intra.append(_dot(attn, _block_diag4(vc, bdo)))
        bl = bc[CHUNK - 1:CHUNK, :]
        kdt = (kc * jnp.exp(bl - bc)).T
        s_add.append(jnp.where(bdm, _dot(kdt, vc), 0.0))
        s_dec.append(_as_column(jnp.exp(bl)))
        q_dec.append((q[rs] * jnp.exp(bc)).astype(BF16))

    st = st_ref[...]
    outs = []
    for oi, sa, sd, qd in zip(o_intra, s_add, s_dec, q_dec):
        outs.append(oi + jnp.dot(qd, st.astype(BF16), preferred_element_type=F32))
        st = sd * st + sa
    st_ref[...] = st
    o = jnp.concatenate(outs, axis=0)
    o_ref[...] = (_group_rmsnorm(o, bdm_ref[...], on_ref[...]) * _sigmoid(g_ref[...])).astype(BF16)


def _hgrn(z, lb_logits, out_norm, layer, b, s, t):
    m = b * s
    nt = s // t
    depth = lb_logits.shape[0]
    bd_mean, bd_ones = _consts_recurrent()
    selp = np.zeros((SUB, 2, 64, 2, CHUNK), np.float32)
    for sl in range(SUB):
        for h2 in range(2):
            selp[sl, h2, :, h2, sl::SUB] = 1.0
    selp = jnp.asarray(selp.reshape(SUB * 128, 128), BF16)
    col = COLB_HG // 256
    c2 = lambda bb, i: (0, 0)
    kern = functools.partial(_hgrn_kernel, layer=layer, depth=depth, t=t)
    return pl.pallas_call(
        kern,
        grid=(b, nt),
        in_specs=[pl.BlockSpec((t, 256), lambda bb, i: (bb * nt + i, col)),
                  pl.BlockSpec((t, 256), lambda bb, i: (bb * nt + i, col + 1)),
                  pl.BlockSpec((t, 256), lambda bb, i: (bb * nt + i, col + 2)),
                  pl.BlockSpec((t, 256), lambda bb, i: (bb * nt + i, col + 3)),
                  pl.BlockSpec((depth, 256), c2),
                  pl.BlockSpec((1, 256), c2),
                  pl.BlockSpec((256, 256), c2),
                  pl.BlockSpec((256, 256), c2),
                  pl.BlockSpec((SUB * 128, 128), c2)],
        out_specs=pl.BlockSpec((t, 256), lambda bb, i: (bb * nt + i, 0)),
        out_shape=jax.ShapeDtypeStruct((m, 256), BF16),
        scratch_shapes=[pltpu.VMEM((256, 256), F32), pltpu.VMEM((2 * t, SUB * 128), F32)],
        compiler_params=_cp("parallel", "arbitrary"),
        name="hgrn2",
    )(z, z, z, z, lb_logits, jnp.tile(out_norm, 4).reshape(1, 256), bd_mean, bd_ones, selp)


def _gdn_kernel(q_ref, k_ref, v_ref, z_ref, ba_ref, cw_ref, al_ref, dt_ref, on_ref, bdm_ref, bdo_ref,
                eb_ref, ea_ref, o_ref, st_ref, prev_ref, *, t):
    @pl.when(pl.program_id(1) == 0)
    def _():
        st_ref[...] = jnp.zeros(st_ref.shape, F32)
        prev_ref[...] = jnp.zeros(prev_ref.shape, F32)

    def conv_silu(x_ref, p):
        x = x_ref[...]
        xp = jnp.concatenate([prev_ref[:, p * 256:(p + 1) * 256], x], axis=0)
        w = cw_ref[:, p * 256:(p + 1) * 256]
        y = w[3:4, :] * xp[8:, :]
        for j in range(1, 4):
            y = y + w[3 - j:4 - j, :] * pltpu.roll(xp, j, 0)[8:, :]
        return _silu(y), x[t - 8:, :]

    qa, qtail = conv_silu(q_ref, 0)
    ka, ktail = conv_silu(k_ref, 1)
    va, vtail = conv_silu(v_ref, 2)
    prev_ref[:, 0:256] = qtail
    prev_ref[:, 256:512] = ktail
    prev_ref[:, 512:768] = vtail

    bdo = bdo_ref[...]
    ss = _dot_ls(jnp.concatenate([qa * qa, ka * ka], axis=0), bdo, 2)
    q = qa * lax.rsqrt(ss[0:t] + EPS) * (HEAD_DIM ** -0.5)
    k = ka * lax.rsqrt(ss[t:2 * t] + EPS)
    v = va

    gl = ba_ref[...].astype(F32)
    beta = _dot_ls(_sigmoid(gl), eb_ref[...], 2)
    xs = gl + dt_ref[...]
    softplus = jnp.maximum(xs, 0.0) + jnp.log(1.0 + jnp.exp(-jnp.abs(xs)))
    b = _dot_ls(_chunk_cumsum(-jnp.exp(al_ref[...]) * softplus), ea_ref[...], 3)

    bdm = _bd_mask()
    trow = lax.broadcasted_iota(jnp.int32, (CHUNK, 256), 0)
    scol = lax.broadcasted_iota(jnp.int32, (CHUNK, 256), 1) % CHUNK
    incl = scol <= trow
    strict = scol < trow
    eye = jnp.where(scol == trow, 1.0, 0.0)

    chunks = [slice(c * CHUNK, (c + 1) * CHUNK) for c in range(t // CHUNK)]
    nch = len(chunks)
    aw, aq = [], []
    for rs in chunks:
        qc, kc, bc = q[rs], k[rs], b[rs]
        brow = jnp.concatenate([bc[:, h * 64:(h + 1) * 64].T for h in range(N_HEADS4)], axis=1)
        e = jnp.exp(jnp.where(incl, bc - brow, 0.0))
        gram = _dot_nt(jnp.concatenate([kc * beta[rs], qc], axis=0), _block_diag4(kc, bdo))
        aw.append(jnp.where(strict, gram[:CHUNK] * e, 0.0))
        aq.append(jnp.where(incl, gram[CHUNK:] * e, 0.0).astype(BF16))
    pw = [None] * nch
    tinv = [None] * nch
    for c in range(nch):
        bh, bl_ = _split(-aw[c], 2)
        r1 = _rows_dot([bh, bl_], _block_diag4(bh, bdo))
        pw[c] = r1[0] + r1[1] + jnp.dot(bh, _block_diag4(bl_, bdo), preferred_element_type=F32)
        tinv[c] = eye - aw[c]
    for c in range(nch):
        ph, pl_ = _split(pw[c], 2)
        th, tl_ = _split(tinv[c], 2)
        r1 = _rows_dot([ph, pl_, th, tl_], _block_diag4(ph, bdo))
        r2 = _rows_dot([ph, th], _block_diag4(pl_, bdo))
        pw[c] = r1[0] + r1[1] + r2[0]
        tinv[c] = tinv[c] + r1[2] + r1[3] + r2[1]
    for _ in range(3):
        for c in range(nch):
            ph = pw[c].astype(BF16)
            r1 = _rows_dot([ph, tinv[c].astype(BF16)], _block_diag4(ph, bdo))
            pw[c] = r1[0]
            tinv[c] = tinv[c] + r1[1]
    for c in range(nch):
        tinv[c] = tinv[c] + jnp.dot(tinv[c].astype(BF16), _block_diag4(pw[c], bdo), preferred_element_type=F32)
    u, w = [], []
    for c, rs in enumerate(chunks):
        t2 = _split(tinv[c], 2)
        kbe = k[rs] * beta[rs] * jnp.exp(b[rs])
        r1 = _rows_dot(t2, _block_diag4(v[rs] * beta[rs], bdo))
        r2 = _rows_dot(t2, _block_diag4(kbe, bdo))
        u.append(r1[0] + r1[1])
        w.append(r2[0] + r2[1])
    lhs, s_add, s_dec, o0 = [], [], [], []
    for c, rs in enumerate(chunks):
        bc = b[rs]
        bl = bc[CHUNK - 1:CHUNK, :]
        kdt = (k[rs] * jnp.exp(bl - bc)).T.astype(BF16)
        rn = jnp.dot(kdt, jnp.concatenate([w[c], u[c]], axis=1).astype(BF16), preferred_element_type=F32)
        qeff = q[rs] * jnp.exp(bc) - jnp.dot(aq[c], _block_diag4(w[c], bdo), preferred_element_type=F32)
        lhs.append(jnp.concatenate([qeff.astype(BF16), jnp.where(bdm, rn[:, 0:256], 0.0).astype(BF16)], axis=0))
        s_add.append(jnp.where(bdm, rn[:, 256:512], 0.0))
        s_dec.append(_as_column(jnp.exp(bl)))
        o0.append(jnp.dot(aq[c], _block_diag4(u[c], bdo), preferred_element_type=F32))

    st = st_ref[...]
    outs = []
    for c in range(nch):
        r = jnp.dot(lhs[c], st.astype(BF16), preferred_element_type=F32)
        outs.append(o0[c] + r[0:CHUNK])
        st = s_dec[c] * st - r[CHUNK:] + s_add[c]
    st_ref[...] = st
    o = jnp.concatenate(outs, axis=0)
    o_ref[...] = (_group_rmsnorm(o, bdm_ref[...], on_ref[...]) * _silu(z_ref[...])).astype(BF16)


def _gdn(za, zb, conv_w, a_log, dt_bias, out_norm, b, s, t):
    m = b * s
    nt = s // t
    bd_mean, bd_ones = _consts_recurrent()
    eb = np.zeros((128, 256), np.float32)
    ea = np.zeros((128, 256), np.float32)
    for h in range(N_HEADS4):
        eb[h, h * 64:(h + 1) * 64] = 1.0
        ea[4 + h, h * 64:(h + 1) * 64] = 1.0
    al = jnp.zeros((1, 128), F32).at[0, 4:8].set(a_log)
    dt = jnp.zeros((1, 128), F32).at[0, 4:8].set(dt_bias)
    col = COLB_GD // 256
    c2 = lambda bb, i: (0, 0)
    kern = functools.partial(_gdn_kernel, t=t)
    return pl.pallas_call(
        kern,
        grid=(b, nt),
        in_specs=[pl.BlockSpec((t, 256), lambda bb, i: (bb * nt + i, col)),
                  pl.BlockSpec((t, 256), lambda bb, i: (bb * nt + i, col + 1)),
                  pl.BlockSpec((t, 256), lambda bb, i: (bb * nt + i, col + 2)),
                  pl.BlockSpec((t, 256), lambda bb, i: (bb * nt + i, col + 3)),
                  pl.BlockSpec((t, 128), lambda bb, i: (bb * nt + i, COL_BA // 128)),
                  pl.BlockSpec((4, 768), c2),
                  pl.BlockSpec((1, 128), c2), pl.BlockSpec((1, 128), c2),
                  pl.BlockSpec((1, 256), c2),
                  pl.BlockSpec((256, 256), c2), pl.BlockSpec((256, 256), c2),
                  pl.BlockSpec((128, 256), c2), pl.BlockSpec((128, 256), c2)],
        out_specs=pl.BlockSpec((t, 256), lambda bb, i: (bb * nt + i, 0)),
        out_shape=jax.ShapeDtypeStruct((m, 256), BF16),
        scratch_shapes=[pltpu.VMEM((256, 256), F32), pltpu.VMEM((8, 768), F32)],
        compiler_params=_cp("parallel", "arbitrary"),
        name="gdn",
    )(zb, zb, zb, zb, za, conv_w, al, dt, jnp.tile(out_norm, 4).reshape(1, 256), bd_mean, bd_ones,
      jnp.asarray(eb, BF16), jnp.asarray(ea, BF16))


def _merge_xattn_kernel(x_ref, oc_ref, os_ref, ow_ref, ng_ref, yb_ref, yc_ref, ma_ref, mb_ref, mc_ref,
                        ex_ref, wa_ref, wb_ref, wc_ref, wmix_ref, g_ref, wq_ref, qn_ref, kn_ref, mkv_ref, wo_ref,
                        o_ref):
    sg = _sigmoid(ng_ref[...].astype(F32))
    ya = (_dot_ls(sg, ex_ref[0], 2) * oc_ref[...].astype(F32) + _dot_ls(sg, ex_ref[1], 2) * os_ref[...].astype(F32)
          + _dot_ls(sg, ex_ref[2], 2) * ow_ref[...].astype(F32))
    merged = (_sigmoid(ma_ref[...].astype(F32)) * _dot(ya, wa_ref[...])
              + _sigmoid(mb_ref[...].astype(F32)) * jnp.dot(yb_ref[...], wb_ref[...], preferred_element_type=F32)
              + _sigmoid(mc_ref[...].astype(F32)) * jnp.dot(yc_ref[...], wc_ref[...], preferred_element_type=F32))
    x = x_ref[...] + _dot(merged, wmix_ref[...])

    h = x * lax.rsqrt(jnp.mean(x * x, axis=-1, keepdims=True) + EPS) * g_ref[...]
    q = _dot(h, wq_ref[...])
    scores = []
    for hd in range(N_HEADS4):
        ls = slice(hd * 128, (hd + 1) * 128)
        qh = q[:, ls]
        qh = qh * lax.rsqrt(jnp.mean(qh * qh, axis=-1, keepdims=True) + EPS) * qn_ref[...]
        kh = mkv_ref[:, ls]
        kh = kh * lax.rsqrt(jnp.mean(kh * kh, axis=-1, keepdims=True) + EPS) * kn_ref[...]
        scores.append(_dot_nt(qh, kh) * (XATTN_HEAD_DIM ** -0.5))
    outs = []
    for hd, s in enumerate(scores):
        e = jnp.exp(s - jnp.max(s, axis=-1, keepdims=True))
        pv = _dot(e, mkv_ref[:, 512 + hd * 128:512 + (hd + 1) * 128])
        outs.append(pv / jnp.sum(e, axis=-1, keepdims=True))
    o = jnp.concatenate(outs, axis=1)
    o_ref[...] = x + _dot(o, wo_ref[...])


def _merge_xattn(x2, o_cmp, o_sel, o_win, z, yb, yc, wa, wb, wc, wmix, g, wq, qn, kn, mkv, wo, s, tm):
    m, d = x2.shape
    nt = s // tm
    ex = np.zeros((3, 128, 512), np.float32)
    for c in range(3):
        for h in range(NSA_HEADS):
            ex[c, c * NSA_HEADS + h, h * 64:(h + 1) * 64] = 1.0
    row = lambda i: (i, 0)
    c2 = lambda i: (0, 0)
    once = dict(pipeline_mode=pl.Buffered(1))
    return pl.pallas_call(
        _merge_xattn_kernel,
        grid=(m // tm,),
        in_specs=[pl.BlockSpec((tm, d), row),
                  pl.BlockSpec((tm, 512), row), pl.BlockSpec((tm, 512), row), pl.BlockSpec((tm, 512), row),
                  pl.BlockSpec((tm, 128), lambda i: (i, COL_NG // 128)),
                  pl.BlockSpec((tm, 256), row), pl.BlockSpec((tm, 256), row),
                  pl.BlockSpec((tm, 1024), lambda i: (i, 0)),
                  pl.BlockSpec((tm, 1024), lambda i: (i, 1)),
                  pl.BlockSpec((tm, 1024), lambda i: (i, 2)),
                  pl.BlockSpec((3, 128, 512), lambda i: (0, 0, 0)),
                  pl.BlockSpec((512, d), c2, **once), pl.BlockSpec((256, d), c2, **once),
                  pl.BlockSpec((256, d), c2, **once),
                  pl.BlockSpec((d, d), c2, **once),
                  pl.BlockSpec((1, d), c2),
                  pl.BlockSpec((d, 512), c2, **once),
                  pl.BlockSpec((1, 128), c2), pl.BlockSpec((1, 128), c2),
                  pl.BlockSpec((N_MEM, 1024), lambda i: (i // nt, 0)),
                  pl.BlockSpec((512, d), c2, **once)],
        out_specs=pl.BlockSpec((tm, d), row),
        out_shape=jax.ShapeDtypeStruct((m, d), F32),
        compiler_params=_cp("parallel"),
        name="merge_xattn",
    )(x2, o_cmp, o_sel, o_win, z, yb, yc, z, z, z, jnp.asarray(ex, BF16), wa, wb, wc, wmix,
      g.reshape(1, d), wq, qn.reshape(1, 128), kn.reshape(1, 128), mkv, wo)


FFN_HALO = 16


def _ffn_kernel(x_ref, xh_ref, g_ref, wua_ref, wub_ref, cwa_ref, cwb_ref, wd_ref, o_ref, h_ref, acc_ref, *, nt, tm):
    i = pl.program_id(0)
    f = pl.program_id(1)

    @pl.when(f == 0)
    def _():
        def norm(x):
            return x * lax.rsqrt(jnp.mean(x * x, axis=-1, keepdims=True) + EPS) * g_ref[...]
        first = (i % nt) == 0
        h_ref[0:FFN_HALO, :] = jnp.where(first, 0.0, norm(xh_ref[...])).astype(BF16)
        h_ref[FFN_HALO:, :] = norm(x_ref[...]).astype(BF16)
        acc_ref[...] = jnp.zeros(acc_ref.shape, F32)

    h = h_ref[...]

    def up_conv(w_ref, cw_ref):
        u = jnp.dot(h, w_ref[...], preferred_element_type=F32)
        cw = cw_ref[...]
        y = cw[2:3, :] * u + cw[1:2, :] * pltpu.roll(u, 1, 0) + cw[0:1, :] * pltpu.roll(u, 2, 0)
        return y[FFN_HALO:, :]

    a = up_conv(wua_ref, cwa_ref)
    bb = up_conv(wub_ref, cwb_ref)
    acc_ref[...] += _dot(_silu(a) * bb, wd_ref[...])

    @pl.when(f == pl.num_programs(1) - 1)
    def _():
        o_ref[...] = x_ref[...] + acc_ref[...]


def _ffn(x2, g, w_up, conv_w, w_down, s, tm, tf):
    m, d = x2.shape
    nt = s // tm
    nf = D_FF // tf
    hb = tm // FFN_HALO
    kern = functools.partial(_ffn_kernel, nt=nt, tm=tm)
    return pl.pallas_call(
        kern,
        grid=(m // tm, nf),
        in_specs=[pl.BlockSpec((tm, d), lambda i, f: (i, 0)),
                  pl.BlockSpec((FFN_HALO, d), lambda i, f: (jnp.maximum(i * hb - 1, 0), 0)),
                  pl.BlockSpec((1, d), lambda i, f: (0, 0)),
                  pl.BlockSpec((d, tf), lambda i, f: (0, f)),
                  pl.BlockSpec((d, tf), lambda i, f: (0, nf + f)),
                  pl.BlockSpec((3, tf), lambda i, f: (0, f)),
                  pl.BlockSpec((3, tf), lambda i, f: (0, nf + f)),
                  pl.BlockSpec((tf, d), lambda i, f: (f, 0))],
        out_specs=pl.BlockSpec((tm, d), lambda i, f: (i, 0)),
        out_shape=jax.ShapeDtypeStruct((m, d), F32),
        scratch_shapes=[pltpu.VMEM((tm + FFN_HALO, d), BF16), pltpu.VMEM((tm, d), F32)],
        compiler_params=_cp("parallel", "arbitrary"),
        name="conv_glu_ffn",
    )(x2, x2, g.reshape(1, d), w_up, w_up, conv_w, conv_w, w_down)


def _permute_w_in(w):
    d = w.shape[0]
    gate = w[:, 1280:1304].reshape(d, NSA_HEADS, 3).transpose(0, 2, 1).reshape(d, 24)
    z104 = jnp.zeros((d, 104), w.dtype)
    z120 = jnp.zeros((d, 120), w.dtype)
    wa = jnp.concatenate([w[:, 3360:6432], w[:, 0:1280], gate, z104, w[:, 3352:3360], z120], axis=1)
    return wa.astype(BF16), w[:, 1304:3352].astype(BF16)


def _rope_tables(s):
    inv_freq = (1.0 / (ROPE_THETA ** (np.arange(0, HEAD_DIM, 2, dtype=np.float32) / HEAD_DIM))).astype(np.float32)
    ang = np.arange(s, dtype=np.float32)[:, None] * inv_freq[None, :]
    c, sn = np.cos(ang), np.sin(ang)
    return (jnp.asarray(np.tile(np.concatenate([c, c], axis=1), (1, 2)), F32),
            jnp.asarray(np.tile(np.concatenate([-sn, sn], axis=1), (1, 2)), F32))


def kernel(x, mem, mem_norm, mem_w_kv, hgrn_lb_logits, norm_mix, w_in, nsa_q_norm, nsa_k_norm, cmp_pos_k, cmp_pos_v, cmp_k_w1, cmp_k_w2, cmp_v_w1, cmp_v_w2, hgrn_out_norm, gdn_conv, gdn_a_log, gdn_dt_bias, gdn_out_norm, w_branch_a, w_branch_b, w_branch_c, w_mix_out, norm_cross, xattn_wq, xattn_q_norm, xattn_k_norm, xattn_wo, norm_ffn, ffn_w_up, ffn_conv, ffn_w_down):
    b, s, d = x.shape
    m = b * s
    depth = w_in.shape[0]
    cos_t, sin_t = _rope_tables(s)
    x2 = x.reshape(m, d)
    mkv = _norm_matmul(mem.reshape(b * N_MEM, d), mem_norm, mem_w_kv.astype(BF16), N_MEM, 512)

    tm_in = min(1024, m)
    for l in range(depth):
        w_a, w_b = _permute_w_in(w_in[l])
        z = _norm_matmul(x2, norm_mix[l], w_a, tm_in, WIDTH_A // 3, BF16)
        zb = _norm_matmul(x2, norm_mix[l], w_b, tm_in, WIDTH_B // 2)
        q_r, kcmp, vcmp, kst, kwt, vs, vw = _nsa_prep(z, cos_t, sin_t, nsa_q_norm[l], nsa_k_norm[l], b, s, 512)
        rk = kcmp.reshape(b, s // CMP_STRIDE, CMP_STRIDE * 128)
        rv = vcmp.reshape(b, s // CMP_STRIDE, CMP_STRIDE * 128)
        kct, vc = _compress(rk, rv, cmp_pos_k[l], cmp_pos_v[l], cmp_k_w1[l], cmp_k_w2[l], cmp_v_w1[l], cmp_v_w2[l])
        o_cmp, sel = _cmp_attn(q_r, kct, vc, b, s, 256)
        o_sel = _sel_attn(q_r, kst, vs, sel, b, s, 512, 512)
        o_win = _win_attn(q_r, kwt, vw, b, s, 256)
        yb = _hgrn(zb, hgrn_lb_logits, hgrn_out_norm[l], l, b, s, 512)
        yc = _gdn(z, zb, gdn_conv[l], gdn_a_log[l], gdn_dt_bias[l], gdn_out_norm[l], b, s, 512)
        x2 = _merge_xattn(x2, o_cmp, o_sel, o_win, z, yb, yc, w_branch_a[l].astype(BF16),
                          w_branch_b[l].astype(BF16), w_branch_c[l].astype(BF16), w_mix_out[l].astype(BF16),
                          norm_cross[l], xattn_wq[l].astype(BF16), xattn_q_norm[l], xattn_k_norm[l], mkv,
                          xattn_wo[l].astype(BF16), s, 512)
        x2 = _ffn(x2, norm_ffn[l], ffn_w_up[l].astype(BF16), ffn_conv[l], ffn_w_down[l].astype(BF16), s, 512, 1408)
    return x2.reshape(b, s, d)
```

```python
import functools
import math

import numpy as np
import jax
import jax.numpy as jnp
from jax import lax
from jax.experimental import pallas as pl
from jax.experimental.pallas import tpu as pltpu

F32 = jnp.float32
BF16 = jnp.bfloat16

EPS = 1e-6
ROPE_THETA = 10000.0
NEG_BIG = -1e30
TINY = 1e-20
FORCE_SCORE = 1e6
SEL_MASK = 2.0 ** 30

D_MODEL = 1024
N_MEM = 256
HEAD_DIM = 64
NSA_HEADS = 8
NSA_GROUPS = 2
NSA_HPG = 4
CMP_BLOCK = 32
CMP_STRIDE = 16
CMP_HIDDEN = 128
SEL_BLOCK = 64
SEL_TOPK = 16
WINDOW = 512
N_HEADS4 = 4
CHUNK = 64
SUB = 16
XATTN_HEAD_DIM = 128
D_FF = 2816

COL_M = 0
COL_NQ = 3072
COL_KV = 3584
COL_NG = 4352
COL_BA = 4480
WIDTH_A = 4608
COLB_HG = 0
COLB_GD = 1024
WIDTH_B = 2048

VMEM_LIMIT = 48 * 1024 * 1024


def _cp(*sem):
    return pltpu.CompilerParams(dimension_semantics=sem, vmem_limit_bytes=VMEM_LIMIT)


def _dot(a, b):
    return jnp.dot(a.astype(BF16), b.astype(BF16), preferred_element_type=F32)


def _dot_nt(a, b):
    return lax.dot_general(a.astype(BF16), b.astype(BF16), (((1,), (1,)), ((), ())),
                           preferred_element_type=F32)


def _split(a, n):
    parts = []
    r = a
    for _ in range(n):
        p = r.astype(BF16)
        parts.append(p)
        r = r - p.astype(F32)
    return parts


def _rows_dot(blocks, rhs):
    if len(blocks) == 1:
        return [jnp.dot(blocks[0], rhs, preferred_element_type=F32)]
    r = jnp.dot(jnp.concatenate(blocks, axis=0), rhs, preferred_element_type=F32)
    out, o = [], 0
    for blk in blocks:
        out.append(r[o:o + blk.shape[0]])
        o += blk.shape[0]
    return out


def _dot_ls(a, b_exact, n=2):
    parts = _rows_dot(_split(a, n), b_exact)
    acc = parts[0]
    for p in parts[1:]:
        acc = acc + p
    return acc


def _chunk_cumsum(x):
    row = lax.broadcasted_iota(jnp.int32, (x.shape[0], 1), 0) % CHUNK
    sh = 1
    while sh < CHUNK:
        x = x + jnp.where(row >= sh, pltpu.roll(x, sh, 0), 0.0)
        sh *= 2
    return x


def _as_column(row):
    return jnp.broadcast_to(row, (8, row.shape[1])).T[:, 0:1]


def _sigmoid(x):
    return 1.0 / (1.0 + jnp.exp(-x))


def _silu(x):
    return x * _sigmoid(x)


def _block_diag4(y, keep):
    yb = y.astype(BF16)
    return jnp.where(keep, jnp.concatenate([yb, yb, yb, yb], axis=0), jnp.zeros((), BF16))


def _bd_mask():
    r = lax.broadcasted_iota(jnp.int32, (256, 256), 0) // 64
    c = lax.broadcasted_iota(jnp.int32, (256, 256), 1) // 64
    return r == c


def _norm_matmul_kernel(x_ref, g_ref, w_ref, o_ref, h_ref):
    @pl.when(pl.program_id(1) == 0)
    def _():
        x = x_ref[...]
        ms = jnp.mean(x * x, axis=-1, keepdims=True)
        h_ref[...] = (x * lax.rsqrt(ms + EPS) * g_ref[...]).astype(BF16)

    o_ref[...] = jnp.dot(h_ref[...], w_ref[...], preferred_element_type=F32).astype(o_ref.dtype)


def _norm_matmul(x, g, w, tm, tn, out_dtype=F32):
    m, d = x.shape
    n = w.shape[1]
    return pl.pallas_call(
        _norm_matmul_kernel,
        grid=(m // tm, n // tn),
        in_specs=[pl.BlockSpec((tm, d), lambda i, j: (i, 0)),
                  pl.BlockSpec((1, d), lambda i, j: (0, 0)),
                  pl.BlockSpec((d, tn), lambda i, j: (0, j))],
        out_specs=pl.BlockSpec((tm, tn), lambda i, j: (i, j)),
        out_shape=jax.ShapeDtypeStruct((m, n), out_dtype),
        scratch_shapes=[pltpu.VMEM((tm, d), BF16)],
        compiler_params=_cp("parallel", "arbitrary"),
        name="norm_matmul",
    )(x, g.reshape(1, d), w)


def _head_norm(x, bd, g):
    ms = _dot_ls(x * x, bd, 2)
    return x * lax.rsqrt(ms + EPS) * g


def _rope(x, c, s):
    w = x.shape[1]
    lane = lax.broadcasted_iota(jnp.int32, x.shape, 1)
    sw = jnp.where((lane & 32) != 0, pltpu.roll(x, 32, 1), pltpu.roll(x, w - 32, 1))
    return x * c + sw * s


def _nsa_prep_kernel(q_ref, kvc_ref, kvs_ref, kvw_ref, cos_ref, sin_ref, qn_ref, kn_ref, bd512_ref, bd128_ref,
                     qo_ref, kc_ref, vc_ref, kst_ref, kwt_ref, vs_ref, vw_ref):
    c = cos_ref[...]
    s = sin_ref[...]
    c4 = jnp.concatenate([c, c, c, c], axis=1)
    s4 = jnp.concatenate([s, s, s, s], axis=1)
    q = _rope(_head_norm(q_ref[...].astype(F32), bd512_ref[...], qn_ref[...]), c4, s4)
    qo_ref[...] = (q * (HEAD_DIM ** -0.5)).astype(BF16)

    def key(ref, row):
        return _rope(_head_norm(ref[:, 0:128].astype(F32), bd128_ref[...], kn_ref[row:row + 1, :]), c, s)

    kc_ref[...] = key(kvc_ref, 0)
    vc_ref[...] = kvc_ref[:, 128:256].astype(F32)
    kst_ref[...] = key(kvs_ref, 1).T.astype(BF16)
    kwt_ref[...] = key(kvw_ref, 2).T.astype(BF16)
    vs = kvs_ref[:, 128:256].astype(BF16)
    vw = kvw_ref[:, 128:256].astype(BF16)
    ones = jnp.ones((vs.shape[0], 64), BF16)
    for g in range(NSA_GROUPS):
        vs_ref[g] = jnp.concatenate([vs[:, g * 64:(g + 1) * 64], ones], axis=1)
        vw_ref[g] = jnp.concatenate([vw[:, g * 64:(g + 1) * 64], ones], axis=1)


def _nsa_prep(z, cos_t, sin_t, q_norm, k_norm, b, s, tm):
    m = b * s
    nt = s // tm
    bd512 = jnp.asarray(np.kron(np.eye(8), np.full((64, 64), 1.0 / 64)), BF16)
    bd128 = jnp.asarray(np.kron(np.eye(2), np.full((64, 64), 1.0 / 64)), BF16)
    qn = jnp.tile(q_norm, 8).reshape(1, 512)
    kn = jnp.tile(k_norm, (1, 2))
    row = lambda i: (i, 0)
    const = lambda i: (0, 0)
    return pl.pallas_call(
        _nsa_prep_kernel,
        grid=(m // tm,),
        in_specs=[pl.BlockSpec((tm, 512), lambda i: (i, COL_NQ // 512)),
                  pl.BlockSpec((tm, 256), lambda i: (i, COL_KV // 256)),
                  pl.BlockSpec((tm, 256), lambda i: (i, COL_KV // 256 + 1)),
                  pl.BlockSpec((tm, 256), lambda i: (i, COL_KV // 256 + 2)),
                  pl.BlockSpec((tm, 128), lambda i: (i % nt, 0)),
                  pl.BlockSpec((tm, 128), lambda i: (i % nt, 0)),
                  pl.BlockSpec((1, 512), const),
                  pl.BlockSpec((3, 128), const),
                  pl.BlockSpec((512, 512), const),
                  pl.BlockSpec((128, 128), const)],
        out_specs=[pl.BlockSpec((tm, 512), row),
                   pl.BlockSpec((tm, 128), row),
                   pl.BlockSpec((tm, 128), row),
                   pl.BlockSpec((None, 128, tm), lambda i: (i // nt, 0, i % nt)),
                   pl.BlockSpec((None, 128, tm), lambda i: (i // nt, 0, i % nt)),
                   pl.BlockSpec((NSA_GROUPS, tm, 128), lambda i: (0, i, 0)),
                   pl.BlockSpec((NSA_GROUPS, tm, 128), lambda i: (0, i, 0))],
        out_shape=[jax.ShapeDtypeStruct((m, 512), BF16),
                   jax.ShapeDtypeStruct((m, 128), F32),
                   jax.ShapeDtypeStruct((m, 128), F32),
                   jax.ShapeDtypeStruct((b, 128, s), BF16),
                   jax.ShapeDtypeStruct((b, 128, s), BF16),
                   jax.ShapeDtypeStruct((NSA_GROUPS, m, 128), BF16),
                   jax.ShapeDtypeStruct((NSA_GROUPS, m, 128), BF16)],
        compiler_params=_cp("parallel"),
        name="nsa_prep",
    )(z, z, z, z, cos_t, sin_t, qn, kn, bd512, bd128)


def _gelu_tanh(x):
    return 0.5 * x * (1.0 + jnp.tanh(math.sqrt(2.0 / math.pi) * (x + 0.044715 * (x * x * x))))


def _compress_kernel(rk_ref, rv_ref, pk_ref, pv_ref, wkt_ref, wkb_ref, wvt_ref, wvb_ref, wk2_ref, wv2_ref,
                     kct_ref, vc_ref):
    def mlp(r_ref, p_ref, wt_ref, wb_ref, w2_ref):
        r = r_ref[...]
        n = r.shape[0]
        top = _dot(r + p_ref[0:1, :], wt_ref[...])
        bot = _dot(r + p_ref[1:2, :], wb_ref[...])
        hid = top + pltpu.roll(bot, n - 1, 0)
        return _dot(_gelu_tanh(hid), w2_ref[...])

    kc = mlp(rk_ref, pk_ref, wkt_ref, wkb_ref, wk2_ref)
    vc = mlp(rv_ref, pv_ref, wvt_ref, wvb_ref, wv2_ref)
    kct_ref[...] = kc.T
    for g in range(NSA_GROUPS):
        vc_ref[g] = vc[:, g * 64:(g + 1) * 64]


def _compress_weights(pos, w1, w2):
    w1r = w1.reshape(2, 16, 64, CMP_HIDDEN)
    zero = jnp.zeros_like(w1r)
    big = jnp.stack([jnp.stack([w1r, zero], axis=-2), jnp.stack([zero, w1r], axis=-2)], axis=2)
    big = big.reshape(2, 16 * 2 * 64, 2 * CMP_HIDDEN).astype(BF16)
    posr = jnp.broadcast_to(pos.reshape(2, 16, 1, 64), (2, 16, 2, 64)).reshape(2, 2048)
    w2bd = jnp.zeros((2, CMP_HIDDEN, 2, 64), F32)
    w2bd = w2bd.at[0, :, 0, :].set(w2).at[1, :, 1, :].set(w2).reshape(2 * CMP_HIDDEN, 128).astype(BF16)
    return posr, big[0], big[1], w2bd


def _compress(rk, rv, pos_k, pos_v, ck_w1, ck_w2, cv_w1, cv_w2):
    b, n, _ = rk.shape
    pk, wkt, wkb, wk2 = _compress_weights(pos_k, ck_w1, ck_w2)
    pv, wvt, wvb, wv2 = _compress_weights(pos_v, cv_w1, cv_w2)
    c2 = lambda i: (0, 0)
    return pl.pallas_call(
        _compress_kernel,
        grid=(b,),
        in_specs=[pl.BlockSpec((None, n, 2048), lambda i: (i, 0, 0)),
                  pl.BlockSpec((None, n, 2048), lambda i: (i, 0, 0)),
                  pl.BlockSpec((2, 2048), c2), pl.BlockSpec((2, 2048), c2),
                  pl.BlockSpec((2048, 256), c2), pl.BlockSpec((2048, 256), c2),
                  pl.BlockSpec((2048, 256), c2), pl.BlockSpec((2048, 256), c2),
                  pl.BlockSpec((256, 128), c2), pl.BlockSpec((256, 128), c2)],
        out_specs=[pl.BlockSpec((None, 128, n), lambda i: (i, 0, 0)),
                   pl.BlockSpec((None, NSA_GROUPS, n, 64), lambda i: (i, 0, 0, 0))],
        out_shape=[jax.ShapeDtypeStruct((b, 128, n), F32),
                   jax.ShapeDtypeStruct((b, NSA_GROUPS, n, 64), F32)],
        compiler_params=_cp("parallel"),
        name="nsa_compress",
    )(rk, rv, pk, pv, wkt, wkb, wvt, wvb, wk2, wv2)


def _cmp_attn_kernel(q_ref, kct_ref, vc_ref, ovt_ref, o_ref, sel_ref, imp_ref, *, tq, ncp, nsel):
    i = pl.program_id(2)
    tpos = i * tq + lax.broadcasted_iota(jnp.int32, (tq, 1), 0)
    nblk = lax.broadcasted_iota(jnp.int32, (1, ncp), 1)
    bias = jnp.where((nblk * CMP_STRIDE + (CMP_BLOCK - 1)) <= tpos, 0.0, NEG_BIG)
    any_valid = jnp.where(tpos >= CMP_BLOCK - 1, 1.0, 0.0)
    q4 = jnp.concatenate([q_ref[:, h * 64:(h + 1) * 64] for h in range(NSA_HPG)], axis=0)
    s = jnp.dot(q4, kct_ref[...].astype(BF16), preferred_element_type=F32)
    s = s.reshape(NSA_HPG, tq, ncp) + bias[None]
    e = jnp.exp(s - jnp.max(s, axis=-1, keepdims=True))
    p = e * (any_valid / jnp.maximum(jnp.sum(e, axis=-1, keepdims=True), 1e-30))
    o = jnp.dot(p.reshape(NSA_HPG * tq, ncp).astype(BF16), vc_ref[...].astype(BF16), preferred_element_type=F32)
    for h in range(NSA_HPG):
        o_ref[:, h * 64:(h + 1) * 64] = o[h * tq:(h + 1) * tq].astype(o_ref.dtype)
    psum = p[0] + p[1] + p[2] + p[3]
    ph, plo = _split(psum, 2)
    ovt = ovt_ref[...]
    imp = _dot_nt(ovt, ph) + _dot_nt(ovt, plo)
    j = lax.broadcasted_iota(jnp.int32, (nsel, 1), 0)
    cur = (i * tq + lax.broadcasted_iota(jnp.int32, (1, tq), 1)) // SEL_BLOCK
    forced = (j == 0) | (j == cur) | (j == cur - 1)
    val = jnp.where(j <= cur, jnp.where(forced, FORCE_SCORE, imp), -1.0)
    imp_ref[...] = val
    groups = [val[g * 8:(g + 1) * 8] for g in range(nsel // 8)]
    cnt = [jnp.zeros((8, tq), F32) for _ in groups]
    jl = lax.broadcasted_iota(jnp.int32, (8, 1), 0)
    for r in range(nsel):
        vr = imp_ref[r:r + 1, :]
        for g, vg in enumerate(groups):
            if g * 8 > r:
                beats = vr >= vg
            elif g * 8 + 7 < r:
                beats = vr > vg
            else:
                beats = (vr > vg) | ((vr == vg) & (jl + g * 8 > r))
            cnt[g] = cnt[g] + jnp.where(beats, 1.0, 0.0)
    sel = jnp.where(jnp.concatenate(cnt, axis=0) < float(min(SEL_TOPK, nsel)), 1.0, 0.0)
    sel_ref[...] = sel.T.astype(BF16)


def _cmp_attn(q_r, kct, vc, b, s, tq):
    m = b * s
    nq = s // tq
    ncp = kct.shape[2]
    nsel = s // SEL_BLOCK
    n_cmp = (s - CMP_BLOCK) // CMP_STRIDE + 1
    tok = np.arange(n_cmp)[:, None] * CMP_STRIDE + np.arange(CMP_BLOCK)[None, :]
    ov = np.zeros((ncp, nsel), np.float32)
    for l in range(CMP_BLOCK):
        ov[np.arange(n_cmp), tok[:, l] // SEL_BLOCK] += 1.0
    ovt = jnp.asarray(ov.T, BF16)
    kern = functools.partial(_cmp_attn_kernel, tq=tq, ncp=ncp, nsel=nsel)
    return pl.pallas_call(
        kern,
        grid=(b, NSA_GROUPS, nq),
        in_specs=[pl.BlockSpec((tq, 256), lambda bb, g, i: (bb * nq + i, g)),
                  pl.BlockSpec((None, 64, ncp), lambda bb, g, i: (bb, g, 0)),
                  pl.BlockSpec((None, None, ncp, 64), lambda bb, g, i: (bb, g, 0, 0)),
                  pl.BlockSpec((nsel, ncp), lambda bb, g, i: (0, 0))],
        out_specs=[pl.BlockSpec((tq, 256), lambda bb, g, i: (bb * nq + i, g)),
                   pl.BlockSpec((None, tq, nsel), lambda bb, g, i: (g, bb * nq + i, 0))],
        out_shape=[jax.ShapeDtypeStruct((m, 512), BF16),
                   jax.ShapeDtypeStruct((NSA_GROUPS, m, nsel), BF16)],
        scratch_shapes=[pltpu.VMEM((nsel, tq), F32)],
        compiler_params=_cp("parallel", "parallel", "parallel"),
        name="nsa_cmp_attn",
    )(q_r, kct, vc, ovt)


def _sel_attn_kernel(qi_ref, kt_ref, first_ref, last_ref, q_ref, k_ref, v_ref, sel_ref, o_ref, q4_ref, m_ref, acc_ref,
                     *, tq, tk, nsel):
    n = pl.program_id(1)
    qi = qi_ref[n]
    ktile = kt_ref[n]
    rows = NSA_HPG * tq
    groups = [slice(g * rows, (g + 1) * rows) for g in range(NSA_GROUPS)]

    @pl.when(first_ref[n] == 1)
    def _():
        for g in range(NSA_GROUPS):
            unsel = ((sel_ref[g].astype(F32) - 1.0) * SEL_MASK).astype(BF16)
            for h in range(NSA_HPG):
                r0 = g * rows + h * tq
                q4_ref[r0:r0 + tq, 0:64] = q_ref[:, (g * NSA_HPG + h) * 64:(g * NSA_HPG + h + 1) * 64]
                q4_ref[r0:r0 + tq, 64:64 + nsel] = unsel
        m_ref[...] = jnp.full(m_ref.shape, NEG_BIG, F32)
        acc_ref[...] = jnp.zeros(acc_ref.shape, F32)

    def step(with_causal):
        kpos = ktile * tk + lax.broadcasted_iota(jnp.int32, (1, tk), 1)
        blk = lax.broadcasted_iota(jnp.int32, (nsel, 1), 0)
        own_block = jnp.where(blk == kpos // SEL_BLOCK, 1.0, 0.0).astype(BF16)
        scores = [jnp.dot(q4_ref[rs, :], jnp.concatenate([k_ref[g * 64:(g + 1) * 64, :], own_block], axis=0),
                          preferred_element_type=F32) for g, rs in enumerate(groups)]
        if with_causal:
            qpos = qi * tq + lax.broadcasted_iota(jnp.int32, (tq, 1), 0)
            causal = jnp.where(kpos <= qpos, 0.0, NEG_BIG)
        for g, (rs, s) in enumerate(zip(groups, scores)):
            if with_causal:
                s = (s.reshape(NSA_HPG, tq, tk) + causal[None]).reshape(rows, tk)
            m_prev = m_ref[rs, :]
            m_next = jnp.maximum(m_prev, jnp.max(s, axis=-1, keepdims=True))
            p = jnp.exp(s - jnp.tile(m_next, (1, tk // 128)))
            pv = jnp.dot(p.astype(BF16), v_ref[g], preferred_element_type=F32)
            acc_ref[rs, :] = jnp.exp(m_prev - m_next) * acc_ref[rs, :] + pv
            m_ref[rs, :] = m_next

    reaches_past = ktile * tk + tk - 1 > qi * tq
    pl.when(reaches_past)(functools.partial(step, True))
    pl.when(jnp.logical_not(reaches_past))(functools.partial(step, False))

    @pl.when(last_ref[n] == 1)
    def _():
        a = acc_ref[...]
        o = a / pltpu.roll(a, 64, 1)
        for hh in range(NSA_HEADS):
            o_ref[:, hh * 64:(hh + 1) * 64] = o[hh * tq:(hh + 1) * tq, 0:64].astype(o_ref.dtype)


def _sel_attn(q_r, kt, v, sel, b, s, tq, tk):
    m = b * s
    nq = s // tq
    nkb = s // tk
    nsel = s // SEL_BLOCK
    pairs = []
    for qi in range(nq):
        kts = [k for k in range(nkb) if k * tk <= qi * tq + tq - 1]
        pairs += [(qi, k, int(j == 0), int(j == len(kts) - 1)) for j, k in enumerate(kts)]
    tabs = [jnp.asarray(np.array([pr[c] for pr in pairs], np.int32)) for c in range(4)]
    kern = functools.partial(_sel_attn_kernel, tq=tq, tk=tk, nsel=nsel)
    return pl.pallas_call(
        kern,
        grid_spec=pltpu.PrefetchScalarGridSpec(
            num_scalar_prefetch=4,
            grid=(b, len(pairs)),
            in_specs=[pl.BlockSpec((tq, 512), lambda bb, n, qt, kt_, f, l: (bb * nq + qt[n], 0)),
                      pl.BlockSpec((None, 128, tk), lambda bb, n, qt, kt_, f, l: (bb, 0, kt_[n])),
                      pl.BlockSpec((NSA_GROUPS, tk, 128), lambda bb, n, qt, kt_, f, l: (0, bb * nkb + kt_[n], 0)),
                      pl.BlockSpec((NSA_GROUPS, tq, nsel), lambda bb, n, qt, kt_, f, l: (0, bb * nq + qt[n], 0))],
            out_specs=pl.BlockSpec((tq, 512), lambda bb, n, qt, kt_, f, l: (bb * nq + qt[n], 0)),
            scratch_shapes=[pltpu.VMEM((NSA_HEADS * tq, 64 + nsel), BF16), pltpu.VMEM((NSA_HEADS * tq, 128), F32),
                            pltpu.VMEM((NSA_HEADS * tq, 128), F32)]),
        out_shape=jax.ShapeDtypeStruct((m, 512), BF16),
        compiler_params=_cp("parallel", "arbitrary"),
        name="nsa_sel_attn",
    )(*tabs, q_r, kt, v, sel)


def _win_attn_kernel(q_ref, *refs, tq, nblk):
    k_refs, v_refs, o_ref = refs[:nblk], refs[nblk:2 * nblk], refs[2 * nblk]
    qi = pl.program_id(1)
    kpos = (qi - (nblk - 1)) * tq + lax.broadcasted_iota(jnp.int32, (1, nblk * tq), 1)
    dist = qi * tq + lax.broadcasted_iota(jnp.int32, (tq, 1), 0) - kpos
    bias = jnp.where((dist >= 0) & (dist < WINDOW) & (kpos >= 0), 0.0, NEG_BIG)
    scores = []
    for g in range(NSA_GROUPS):
        q4 = jnp.concatenate([q_ref[:, (g * NSA_HPG + h) * 64:(g * NSA_HPG + h + 1) * 64]
                              for h in range(NSA_HPG)], axis=0)
        kt = jnp.concatenate([r[g * 64:(g + 1) * 64, :] for r in k_refs], axis=1)
        scores.append(jnp.dot(q4, kt, preferred_element_type=F32))
    for g, s in enumerate(scores):
        s = (s.reshape(NSA_HPG, tq, nblk * tq) + bias[None]).reshape(NSA_HPG * tq, nblk * tq)
        p = jnp.exp(s - jnp.max(s, axis=-1, keepdims=True))
        v = jnp.concatenate([r[g] for r in v_refs], axis=0)
        pv = jnp.dot(p.astype(BF16), v, preferred_element_type=F32)
        o = pv / pltpu.roll(pv, 64, 1)
        for h in range(NSA_HPG):
            c = (g * NSA_HPG + h) * 64
            o_ref[:, c:c + 64] = o[h * tq:(h + 1) * tq, 0:64].astype(o_ref.dtype)


def _win_attn(q_r, kt, v, b, s, tq):
    m = b * s
    nq = s // tq
    nblk = WINDOW // tq + 1
    kidx = lambda qi, j: jnp.maximum(qi - (nblk - 1) + j, 0)
    k_specs = [pl.BlockSpec((None, 128, tq), functools.partial(lambda bb, qi, j: (bb, 0, kidx(qi, j)), j=j))
               for j in range(nblk)]
    v_specs = [pl.BlockSpec((NSA_GROUPS, tq, 128), functools.partial(lambda bb, qi, j: (0, bb * nq + kidx(qi, j), 0), j=j))
               for j in range(nblk)]
    kern = functools.partial(_win_attn_kernel, tq=tq, nblk=nblk)
    return pl.pallas_call(
        kern,
        grid=(b, nq),
        in_specs=[pl.BlockSpec((tq, 512), lambda bb, qi: (bb * nq + qi, 0))] + k_specs + v_specs,
        out_specs=pl.BlockSpec((tq, 512), lambda bb, qi: (bb * nq + qi, 0)),
        out_shape=jax.ShapeDtypeStruct((m, 512), BF16),
        compiler_params=_cp("parallel", "parallel"),
        name="nsa_win_attn",
    )(q_r, *([kt] * nblk), *([v] * nblk))


def _consts_recurrent():
    bd_mean = np.kron(np.eye(4), np.full((64, 64), 1.0 / 64))
    bd_ones = np.kron(np.eye(4), np.ones((64, 64)))
    return jnp.asarray(bd_mean, BF16), jnp.asarray(bd_ones, BF16)


def _group_rmsnorm(o, bd_mean, g):
    ms = _dot_ls(o * o, bd_mean, 2)
    return o * lax.rsqrt(ms + EPS) * g


def _hgrn_kernel(q_ref, f_ref, i_ref, g_ref, lbl_ref, on_ref, bdm_ref, bdo_ref, selp_ref,
                 o_ref, st_ref, z_ref, *, layer, depth, t):
    @pl.when(pl.program_id(1) == 0)
    def _():
        st_ref[...] = jnp.zeros(st_ref.shape, F32)

    lg = lbl_ref[...]
    mx = jnp.max(lg, axis=0, keepdims=True)
    ex = jnp.exp(lg - mx)
    pr = ex / jnp.sum(ex, axis=0, keepdims=True)
    cs = pr[0:1, :]
    for r in range(1, layer + 1):
        cs = cs + pr[r:r + 1, :]
    lb = cs - pr[0:1, :]

    q = _silu(q_ref[...])
    fz = f_ref[...]
    f = lb + (1.0 - lb) * _sigmoid(fz)
    logf = jnp.log(jnp.maximum(f, TINY))
    k = (1.0 - lb) * _sigmoid(-fz)
    v = i_ref[...]
    b = _chunk_cumsum(logf)

    bdm = _bd_mask()
    bdk = bdo_ref[...] > 0
    nsub = t // SUB
    tl = lax.broadcasted_iota(jnp.int32, (1, SUB, 1), 1)
    for hp in range(2):
        ls = slice(hp * 128, (hp + 1) * 128)
        q3 = q[:, ls].reshape(nsub, SUB, 128)
        k3 = k[:, ls].reshape(nsub, SUB, 128)
        b3 = b[:, ls].reshape(nsub, SUB, 128)
        for sl in range(SUB):
            e = jnp.exp(jnp.where(tl >= sl, b3 - b3[:, sl:sl + 1, :], NEG_BIG))
            zz = q3 * k3[:, sl:sl + 1, :] * e
            z_ref[hp * t:(hp + 1) * t, sl * 128:(sl + 1) * 128] = zz.reshape(t, 128)
    a_pairs = _dot_ls(z_ref[...], selp_ref[...], 2)
    a_all = jnp.concatenate([a_pairs[0:t], a_pairs[t:2 * t]], axis=1)
    trow = lax.broadcasted_iota(jnp.int32, (CHUNK, 256), 0)
    scol = lax.broadcasted_iota(jnp.int32, (CHUNK, 256), 1) % CHUNK
    diag_mask = (trow // SUB) == (scol // SUB)
    m1 = (trow >= 32) & (scol < 32)
    m2 = ((trow >= 16) & (trow < 32) & (scol < 16)) | ((trow >= 48) & (scol >= 32) & (scol < 48))
    rowi = lax.broadcasted_iota(jnp.int32, (CHUNK, 1), 0)

    chunks = [slice(c * CHUNK, (c + 1) * CHUNK) for c in range(t // CHUNK)]
    lvl1, lvl2 = [], []
    for rs in chunks:
        qc, kc, bc = q[rs], k[rs], b[rs]
        r1 = bc[31:32, :]
        q1 = qc * jnp.exp(jnp.minimum(bc - r1, 0.0))
        k1 = jnp.where(rowi < 32, kc * jnp.exp(jnp.minimum(r1 - bc, 0.0)), 0.0)
        r2 = jnp.where(rowi < 32, bc[15:16, :], bc[47:48, :])
        q2 = qc * jnp.exp(jnp.minimum(bc - r2, 0.0))
        k2 = kc * jnp.exp(jnp.minimum(r2 - bc, 0.0))
        lvl1.append(_dot_nt(q1, _block_diag4(k1, bdk)))
        lvl2.append(_dot_nt(q2, _block_diag4(k2, bdk)))
    o_intra, s_add, s_dec, q_dec = [], [], [], []
    for rs, a1, a2 in zip(chunks, lvl1, lvl2):
        kc, vc, bc = k[rs], v[rs], b[rs]
        attn = jnp.where(diag_mask, a_all[rs], 0.0) + jnp.where(m1, a1, 0.0) + jnp.where(m2, a2, 0.0)
        o_intra.append(_dot(attn, _block_diag4(vc, bdk)))
        bl = bc[CHUNK - 1:CHUNK, :]
        kdt = (kc * jnp.exp(bl - bc)).T
        s_add.append(jnp.where(bdm, _dot(kdt, vc), 0.0))
        s_dec.append(_as_column(jnp.exp(bl)))
        q_dec.append((q[rs] * jnp.exp(bc)).astype(BF16))

    st = st_ref[...]
    outs = []
    for oi, sa, sd, qd in zip(o_intra, s_add, s_dec, q_dec):
        outs.append(oi + jnp.dot(qd, st.astype(BF16), preferred_element_type=F32))
        st = sd * st + sa
    st_ref[...] = st
    o = jnp.concatenate(outs, axis=0)
    o_ref[...] = (_group_rmsnorm(o, bdm_ref[...], on_ref[...]) * _sigmoid(g_ref[...])).astype(BF16)


def _hgrn(z, lb_logits, out_norm, layer, b, s, t):
    m = b * s
    nt = s // t
    depth = lb_logits.shape[0]
    bd_mean, bd_ones = _consts_recurrent()
    selp = np.zeros((SUB, 2, 64, 2, CHUNK), np.float32)
    for sl in range(SUB):
        for h2 in range(2):
            selp[sl, h2, :, h2, sl::SUB] = 1.0
    selp = jnp.asarray(selp.reshape(SUB * 128, 128), BF16)
    col = COLB_HG // 256
    c2 = lambda bb, i: (0, 0)
    kern = functools.partial(_hgrn_kernel, layer=layer, depth=depth, t=t)
    return pl.pallas_call(
        kern,
        grid=(b, nt),
        in_specs=[pl.BlockSpec((t, 256), lambda bb, i: (bb * nt + i, col)),
                  pl.BlockSpec((t, 256), lambda bb, i: (bb * nt + i, col + 1)),
                  pl.BlockSpec((t, 256), lambda bb, i: (bb * nt + i, col + 2)),
                  pl.BlockSpec((t, 256), lambda bb, i: (bb * nt + i, col + 3)),
                  pl.BlockSpec((depth, 256), c2),
                  pl.BlockSpec((1, 256), c2),
                  pl.BlockSpec((256, 256), c2),
                  pl.BlockSpec((256, 256), c2),
                  pl.BlockSpec((SUB * 128, 128), c2)],
        out_specs=pl.BlockSpec((t, 256), lambda bb, i: (bb * nt + i, 0)),
        out_shape=jax.ShapeDtypeStruct((m, 256), BF16),
        scratch_shapes=[pltpu.VMEM((256, 256), F32), pltpu.VMEM((2 * t, SUB * 128), F32)],
        compiler_params=_cp("parallel", "arbitrary"),
        name="hgrn2",
    )(z, z, z, z, lb_logits, jnp.tile(out_norm, 4).reshape(1, 256), bd_mean, bd_ones, selp)


def _gdn_kernel(q_ref, k_ref, v_ref, z_ref, ba_ref, cw_ref, al_ref, dt_ref, on_ref, bdm_ref, bdo_ref,
                eb_ref, ea_ref, o_ref, st_ref, prev_ref, *, t):
    @pl.when(pl.program_id(1) == 0)
    def _():
        st_ref[...] = jnp.zeros(st_ref.shape, F32)
        prev_ref[...] = jnp.zeros(prev_ref.shape, F32)

    def conv_silu(x_ref, p):
        x = x_ref[...]
        xp = jnp.concatenate([prev_ref[:, p * 256:(p + 1) * 256], x], axis=0)
        w = cw_ref[:, p * 256:(p + 1) * 256]
        y = w[3:4, :] * xp[8:, :]
        for j in range(1, 4):
            y = y + w[3 - j:4 - j, :] * pltpu.roll(xp, j, 0)[8:, :]
        return _silu(y), x[t - 8:, :]

    qa, qtail = conv_silu(q_ref, 0)
    ka, ktail = conv_silu(k_ref, 1)
    va, vtail = conv_silu(v_ref, 2)
    prev_ref[:, 0:256] = qtail
    prev_ref[:, 256:512] = ktail
    prev_ref[:, 512:768] = vtail

    bdo = bdo_ref[...]
    bdk = bdo > 0
    ss = _dot_ls(jnp.concatenate([qa * qa, ka * ka], axis=0), bdo, 2)
    q = qa * lax.rsqrt(ss[0:t] + EPS) * (HEAD_DIM ** -0.5)
    k = ka * lax.rsqrt(ss[t:2 * t] + EPS)
    v = va

    gl = ba_ref[...].astype(F32)
    beta = _dot_ls(_sigmoid(gl), eb_ref[...], 2)
    xs = gl + dt_ref[...]
    softplus = jnp.maximum(xs, 0.0) + jnp.log(1.0 + jnp.exp(-jnp.abs(xs)))
    b = _dot_ls(_chunk_cumsum(-jnp.exp(al_ref[...]) * softplus), ea_ref[...], 3)

    bdm = _bd_mask()
    trow = lax.broadcasted_iota(jnp.int32, (CHUNK, 256), 0)
    scol = lax.broadcasted_iota(jnp.int32, (CHUNK, 256), 1) % CHUNK
    incl = scol <= trow
    strict = scol < trow
    eye = jnp.where(scol == trow, 1.0, 0.0)

    chunks = [slice(c * CHUNK, (c + 1) * CHUNK) for c in range(t // CHUNK)]
    nch = len(chunks)
    aw, aq = [], []
    for rs in chunks:
        qc, kc, bc = q[rs], k[rs], b[rs]
        brow = jnp.concatenate([bc[:, h * 64:(h + 1) * 64].T for h in range(N_HEADS4)], axis=1)
        e = jnp.exp(jnp.where(incl, bc - brow, 0.0))
        gram = _dot_nt(jnp.concatenate([kc * beta[rs], qc], axis=0), _block_diag4(kc, bdk))
        aw.append(jnp.where(strict, gram[:CHUNK] * e, 0.0))
        aq.append(jnp.where(incl, gram[CHUNK:] * e, 0.0).astype(BF16))
    pw = [None] * nch
    tinv = [None] * nch
    for c in range(nch):
        bh, bl_ = _split(-aw[c], 2)
        r1 = _rows_dot([bh, bl_], _block_diag4(bh, bdk))
        pw[c] = r1[0] + r1[1] + jnp.dot(bh, _block_diag4(bl_, bdk), preferred_element_type=F32)
        tinv[c] = eye - aw[c]
    for c in range(nch):
        ph, pl_ = _split(pw[c], 2)
        th, tl_ = _split(tinv[c], 2)
        r1 = _rows_dot([ph, pl_, th, tl_], _block_diag4(ph, bdk))
        r2 = _rows_dot([ph, th], _block_diag4(pl_, bdk))
        pw[c] = r1[0] + r1[1] + r2[0]
        tinv[c] = tinv[c] + r1[2] + r1[3] + r2[1]
    for _ in range(3):
        for c in range(nch):
            ph = pw[c].astype(BF16)
            r1 = _rows_dot([ph, tinv[c].astype(BF16)], _block_diag4(ph, bdk))
            pw[c] = r1[0]
            tinv[c] = tinv[c] + r1[1]
    for c in range(nch):
        tinv[c] = tinv[c] + jnp.dot(tinv[c].astype(BF16), _block_diag4(pw[c], bdk), preferred_element_type=F32)
    u, w = [], []
    for c, rs in enumerate(chunks):
        t2 = _split(tinv[c], 2)
        kbe = k[rs] * beta[rs] * jnp.exp(b[rs])
        r1 = _rows_dot(t2, _block_diag4(v[rs] * beta[rs], bdk))
        r2 = _rows_dot(t2, _block_diag4(kbe, bdk))
        u.append(r1[0] + r1[1])
        w.append(r2[0] + r2[1])
    lhs, s_add, s_dec, o0 = [], [], [], []
    for c, rs in enumerate(chunks):
        bc = b[rs]
        bl = bc[CHUNK - 1:CHUNK, :]
        kdt = (k[rs] * jnp.exp(bl - bc)).T.astype(BF16)
        rn = jnp.dot(kdt, jnp.concatenate([w[c], u[c]], axis=1).astype(BF16), preferred_element_type=F32)
        qeff = q[rs] * jnp.exp(bc) - jnp.dot(aq[c], _block_diag4(w[c], bdk), preferred_element_type=F32)
        lhs.append(jnp.concatenate([qeff.astype(BF16), jnp.where(bdm, rn[:, 0:256], 0.0).astype(BF16)], axis=0))
        s_add.append(jnp.where(bdm, rn[:, 256:512], 0.0))
        s_dec.append(_as_column(jnp.exp(bl)))
        o0.append(jnp.dot(aq[c], _block_diag4(u[c], bdk), preferred_element_type=F32))

    st = st_ref[...]
    outs = []
    for c in range(nch):
        r = jnp.dot(lhs[c], st.astype(BF16), preferred_element_type=F32)
        outs.append(o0[c] + r[0:CHUNK])
        st = s_dec[c] * st - r[CHUNK:] + s_add[c]
    st_ref[...] = st
    o = jnp.concatenate(outs, axis=0)
    o_ref[...] = (_group_rmsnorm(o, bdm_ref[...], on_ref[...]) * _silu(z_ref[...])).astype(BF16)


def _gdn(za, zb, conv_w, a_log, dt_bias, out_norm, b, s, t):
    m = b * s
    nt = s // t
    bd_mean, bd_ones = _consts_recurrent()
    eb = np.zeros((128, 256), np.float32)
    ea = np.zeros((128, 256), np.float32)
    for h in range(N_HEADS4):
        eb[h, h * 64:(h + 1) * 64] = 1.0
        ea[4 + h, h * 64:(h + 1) * 64] = 1.0
    al = jnp.zeros((1, 128), F32).at[0, 4:8].set(a_log)
    dt = jnp.zeros((1, 128), F32).at[0, 4:8].set(dt_bias)
    col = COLB_GD // 256
    c2 = lambda bb, i: (0, 0)
    kern = functools.partial(_gdn_kernel, t=t)
    return pl.pallas_call(
        kern,
        grid=(b, nt),
        in_specs=[pl.BlockSpec((t, 256), lambda bb, i: (bb * nt + i, col)),
                  pl.BlockSpec((t, 256), lambda bb, i: (bb * nt + i, col + 1)),
                  pl.BlockSpec((t, 256), lambda bb, i: (bb * nt + i, col + 2)),
                  pl.BlockSpec((t, 256), lambda bb, i: (bb * nt + i, col + 3)),
                  pl.BlockSpec((t, 128), lambda bb, i: (bb * nt + i, COL_BA // 128)),
                  pl.BlockSpec((4, 768), c2),
                  pl.BlockSpec((1, 128), c2), pl.BlockSpec((1, 128), c2),
                  pl.BlockSpec((1, 256), c2),
                  pl.BlockSpec((256, 256), c2), pl.BlockSpec((256, 256), c2),
                  pl.BlockSpec((128, 256), c2), pl.BlockSpec((128, 256), c2)],
        out_specs=pl.BlockSpec((t, 256), lambda bb, i: (bb * nt + i, 0)),
        out_shape=jax.ShapeDtypeStruct((m, 256), BF16),
        scratch_shapes=[pltpu.VMEM((256, 256), F32), pltpu.VMEM((8, 768), F32)],
        compiler_params=_cp("parallel", "arbitrary"),
        name="gdn",
    )(zb, zb, zb, zb, za, conv_w, al, dt, jnp.tile(out_norm, 4).reshape(1, 256), bd_mean, bd_ones,
      jnp.asarray(eb, BF16), jnp.asarray(ea, BF16))


def _merge_xattn_kernel(x_ref, oc_ref, os_ref, ow_ref, ng_ref, yb_ref, yc_ref, ma_ref, mb_ref, mc_ref,
                        ex_ref, wa_ref, wb_ref, wc_ref, wmix_ref, g_ref, wq_ref, qn_ref, kn_ref, mkv_ref, wo_ref,
                        o_ref):
    sg = _sigmoid(ng_ref[...].astype(F32))
    ya = (_dot_ls(sg, ex_ref[0], 2) * oc_ref[...].astype(F32) + _dot_ls(sg, ex_ref[1], 2) * os_ref[...].astype(F32)
          + _dot_ls(sg, ex_ref[2], 2) * ow_ref[...].astype(F32))
    merged = (_sigmoid(ma_ref[...].astype(F32)) * _dot(ya, wa_ref[...])
              + _sigmoid(mb_ref[...].astype(F32)) * jnp.dot(yb_ref[...], wb_ref[...], preferred_element_type=F32)
              + _sigmoid(mc_ref[...].astype(F32)) * jnp.dot(yc_ref[...], wc_ref[...], preferred_element_type=F32))
    x = x_ref[...] + _dot(merged, wmix_ref[...])

    h = x * lax.rsqrt(jnp.mean(x * x, axis=-1, keepdims=True) + EPS) * g_ref[...]
    q = _dot(h, wq_ref[...])
    scores = []
    for hd in range(N_HEADS4):
        ls = slice(hd * 128, (hd + 1) * 128)
        qh = q[:, ls]
        qh = qh * lax.rsqrt(jnp.mean(qh * qh, axis=-1, keepdims=True) + EPS) * qn_ref[...]
        kh = mkv_ref[:, ls]
        kh = kh * lax.rsqrt(jnp.mean(kh * kh, axis=-1, keepdims=True) + EPS) * kn_ref[...]
        scores.append(_dot_nt(qh, kh) * (XATTN_HEAD_DIM ** -0.5))
    outs = []
    for hd, s in enumerate(scores):
        e = jnp.exp(s - jnp.max(s, axis=-1, keepdims=True))
        pv = _dot(e, mkv_ref[:, 512 + hd * 128:512 + (hd + 1) * 128])
        outs.append(pv / jnp.sum(e, axis=-1, keepdims=True))
    o = jnp.concatenate(outs, axis=1)
    o_ref[...] = x + _dot(o, wo_ref[...])


def _merge_xattn(x2, o_cmp, o_sel, o_win, z, yb, yc, wa, wb, wc, wmix, g, wq, qn, kn, mkv, wo, s, tm):
    m, d = x2.shape
    nt = s // tm
    ex = np.zeros((3, 128, 512), np.float32)
    for c in range(3):
        for h in range(NSA_HEADS):
            ex[c, c * NSA_HEADS + h, h * 64:(h + 1) * 64] = 1.0
    row = lambda i: (i, 0)
    c2 = lambda i: (0, 0)
    once = dict(pipeline_mode=pl.Buffered(1))
    return pl.pallas_call(
        _merge_xattn_kernel,
        grid=(m // tm,),
        in_specs=[pl.BlockSpec((tm, d), row),
                  pl.BlockSpec((tm, 512), row), pl.BlockSpec((tm, 512), row), pl.BlockSpec((tm, 512), row),
                  pl.BlockSpec((tm, 128), lambda i: (i, COL_NG // 128)),
                  pl.BlockSpec((tm, 256), row), pl.BlockSpec((tm, 256), row),
                  pl.BlockSpec((tm, 1024), lambda i: (i, 0)),
                  pl.BlockSpec((tm, 1024), lambda i: (i, 1)),
                  pl.BlockSpec((tm, 1024), lambda i: (i, 2)),
                  pl.BlockSpec((3, 128, 512), lambda i: (0, 0, 0)),
                  pl.BlockSpec((512, d), c2, **once), pl.BlockSpec((256, d), c2, **once),
                  pl.BlockSpec((256, d), c2, **once),
                  pl.BlockSpec((d, d), c2, **once),
                  pl.BlockSpec((1, d), c2),
                  pl.BlockSpec((d, 512), c2, **once),
                  pl.BlockSpec((1, 128), c2), pl.BlockSpec((1, 128), c2),
                  pl.BlockSpec((N_MEM, 1024), lambda i: (i // nt, 0)),
                  pl.BlockSpec((512, d), c2, **once)],
        out_specs=pl.BlockSpec((tm, d), row),
        out_shape=jax.ShapeDtypeStruct((m, d), F32),
        compiler_params=_cp("parallel"),
        name="merge_xattn",
    )(x2, o_cmp, o_sel, o_win, z, yb, yc, z, z, z, jnp.asarray(ex, BF16), wa, wb, wc, wmix,
      g.reshape(1, d), wq, qn.reshape(1, 128), kn.reshape(1, 128), mkv, wo)


FFN_HALO = 16


def _ffn_kernel(x_ref, xh_ref, g_ref, wua_ref, wub_ref, cwa_ref, cwb_ref, wd_ref, o_ref, h_ref, acc_ref, *, nt, tm):
    i = pl.program_id(0)
    f = pl.program_id(1)

    @pl.when(f == 0)
    def _():
        def norm(x):
            return x * lax.rsqrt(jnp.mean(x * x, axis=-1, keepdims=True) + EPS) * g_ref[...]
        first = (i % nt) == 0
        h_ref[0:FFN_HALO, :] = jnp.where(first, 0.0, norm(xh_ref[...])).astype(BF16)
        h_ref[FFN_HALO:, :] = norm(x_ref[...]).astype(BF16)
        acc_ref[...] = jnp.zeros(acc_ref.shape, F32)

    h = h_ref[...]

    def up_conv(w_ref, cw_ref):
        u = jnp.dot(h, w_ref[...], preferred_element_type=F32)
        cw = cw_ref[...]
        y = cw[2:3, :] * u + cw[1:2, :] * pltpu.roll(u, 1, 0) + cw[0:1, :] * pltpu.roll(u, 2, 0)
        return y[FFN_HALO:, :]

    a = up_conv(wua_ref, cwa_ref)
    bb = up_conv(wub_ref, cwb_ref)
    acc_ref[...] += _dot(_silu(a) * bb, wd_ref[...])

    @pl.when(f == pl.num_programs(1) - 1)
    def _():
        o_ref[...] = x_ref[...] + acc_ref[...]


def _ffn(x2, g, w_up, conv_w, w_down, s, tm, tf):
    m, d = x2.shape
    nt = s // tm
    nf = D_FF // tf
    hb = tm // FFN_HALO
    kern = functools.partial(_ffn_kernel, nt=nt, tm=tm)
    return pl.pallas_call(
        kern,
        grid=(m // tm, nf),
        in_specs=[pl.BlockSpec((tm, d), lambda i, f: (i, 0)),
                  pl.BlockSpec((FFN_HALO, d), lambda i, f: (jnp.maximum(i * hb - 1, 0), 0)),
                  pl.BlockSpec((1, d), lambda i, f: (0, 0)),
                  pl.BlockSpec((d, tf), lambda i, f: (0, f)),
                  pl.BlockSpec((d, tf), lambda i, f: (0, nf + f)),
                  pl.BlockSpec((3, tf), lambda i, f: (0, f)),
                  pl.BlockSpec((3, tf), lambda i, f: (0, nf + f)),
                  pl.BlockSpec((tf, d), lambda i, f: (f, 0))],
        out_specs=pl.BlockSpec((tm, d), lambda i, f: (i, 0)),
        out_shape=jax.ShapeDtypeStruct((m, d), F32),
        scratch_shapes=[pltpu.VMEM((tm + FFN_HALO, d), BF16), pltpu.VMEM((tm, d), F32)],
        compiler_params=_cp("parallel", "arbitrary"),
        name="conv_glu_ffn",
    )(x2, x2, g.reshape(1, d), w_up, w_up, conv_w, conv_w, w_down)


def _permute_w_in(w):
    d = w.shape[0]
    gate = w[:, 1280:1304].reshape(d, NSA_HEADS, 3).transpose(0, 2, 1).reshape(d, 24)
    z104 = jnp.zeros((d, 104), w.dtype)
    z120 = jnp.zeros((d, 120), w.dtype)
    wa = jnp.concatenate([w[:, 3360:6432], w[:, 0:1280], gate, z104, w[:, 3352:3360], z120], axis=1)
    return wa.astype(BF16), w[:, 1304:3352].astype(BF16)


def _rope_tables(s):
    inv_freq = (1.0 / (ROPE_THETA ** (np.arange(0, HEAD_DIM, 2, dtype=np.float32) / HEAD_DIM))).astype(np.float32)
    ang = np.arange(s, dtype=np.float32)[:, None] * inv_freq[None, :]
    c, sn = np.cos(ang), np.sin(ang)
    return (jnp.asarray(np.tile(np.concatenate([c, c], axis=1), (1, 2)), F32),
            jnp.asarray(np.tile(np.concatenate([-sn, sn], axis=1), (1, 2)), F32))


def kernel(x, mem, mem_norm, mem_w_kv, hgrn_lb_logits, norm_mix, w_in, nsa_q_norm, nsa_k_norm, cmp_pos_k, cmp_pos_v, cmp_k_w1, cmp_k_w2, cmp_v_w1, cmp_v_w2, hgrn_out_norm, gdn_conv, gdn_a_log, gdn_dt_bias, gdn_out_norm, w_branch_a, w_branch_b, w_branch_c, w_mix_out, norm_cross, xattn_wq, xattn_q_norm, xattn_k_norm, xattn_wo, norm_ffn, ffn_w_up, ffn_conv, ffn_w_down):
    b, s, d = x.shape
    m = b * s
    depth = w_in.shape[0]
    cos_t, sin_t = _rope_tables(s)
    x2 = x.reshape(m, d)
    mkv = _norm_matmul(mem.reshape(b * N_MEM, d), mem_norm, mem_w_kv.astype(BF16), N_MEM, 512)

    tm_in = min(1024, m)
    for l in range(depth):
        w_a, w_b = _permute_w_in(w_in[l])
        z = _norm_matmul(x2, norm_mix[l], w_a, tm_in, WIDTH_A // 3, BF16)
        zb = _norm_matmul(x2, norm_mix[l], w_b, tm_in, WIDTH_B // 2)
        q_r, kcmp, vcmp, kst, kwt, vs, vw = _nsa_prep(z, cos_t, sin_t, nsa_q_norm[l], nsa_k_norm[l], b, s, 512)
        rk = kcmp.reshape(b, s // CMP_STRIDE, CMP_STRIDE * 128)
        rv = vcmp.reshape(b, s // CMP_STRIDE, CMP_STRIDE * 128)
        kct, vc = _compress(rk, rv, cmp_pos_k[l], cmp_pos_v[l], cmp_k_w1[l], cmp_k_w2[l], cmp_v_w1[l], cmp_v_w2[l])
        o_cmp, sel = _cmp_attn(q_r, kct, vc, b, s, 256)
        o_sel = _sel_attn(q_r, kst, vs, sel, b, s, 512, 512)
        o_win = _win_attn(q_r, kwt, vw, b, s, 256)
        yb = _hgrn(zb, hgrn_lb_logits, hgrn_out_norm[l], l, b, s, 512)
        yc = _gdn(z, zb, gdn_conv[l], gdn_a_log[l], gdn_dt_bias[l], gdn_out_norm[l], b, s, 512)
        x2 = _merge_xattn(x2, o_cmp, o_sel, o_win, z, yb, yc, w_branch_a[l].astype(BF16),
                          w_branch_b[l].astype(BF16), w_branch_c[l].astype(BF16), w_mix_out[l].astype(BF16),
                          norm_cross[l], xattn_wq[l].astype(BF16), xattn_q_norm[l], xattn_k_norm[l], mkv,
                          xattn_wo[l].astype(BF16), s, 512)
        x2 = _ffn(x2, norm_ffn[l], ffn_w_up[l].astype(BF16), ffn_conv[l], ffn_w_down[l].astype(BF16), s, 512, 1408)
    return x2.reshape(b, s, d)
```

```python
import functools
import math

import numpy as np
import jax
import jax.numpy as jnp
from jax import lax
from jax.experimental import pallas as pl
from jax.experimental.pallas import tpu as pltpu

F32 = jnp.float32
BF16 = jnp.bfloat16

EPS = 1e-6
ROPE_THETA = 10000.0
NEG_BIG = -1e30
TINY = 1e-20
FORCE_SCORE = 1e6
SEL_MASK = 2.0 ** 30

D_MODEL = 1024
N_MEM = 256
HEAD_DIM = 64
NSA_HEADS = 8
NSA_GROUPS = 2
NSA_HPG = 4
CMP_BLOCK = 32
CMP_STRIDE = 16
CMP_HIDDEN = 128
SEL_BLOCK = 64
SEL_TOPK = 16
WINDOW = 512
N_HEADS4 = 4
CHUNK = 64
SUB = 16
XATTN_HEAD_DIM = 128
D_FF = 2816

COL_M = 0
COL_NQ = 3072
COL_KV = 3584
COL_NG = 4352
COL_BA = 4480
WIDTH_A = 4608
COLB_HG = 0
COLB_GD = 1024
WIDTH_B = 2048

VMEM_LIMIT = 48 * 1024 * 1024


def _cp(*sem):
    return pltpu.CompilerParams(dimension_semantics=sem, vmem_limit_bytes=VMEM_LIMIT)


def _dot(a, b):
    return jnp.dot(a.astype(BF16), b.astype(BF16), preferred_element_type=F32)


def _dot_nt(a, b):
    return lax.dot_general(a.astype(BF16), b.astype(BF16), (((1,), (1,)), ((), ())),
                           preferred_element_type=F32)


def _split(a, n):
    parts = []
    r = a
    for _ in range(n):
        p = r.astype(BF16)
        parts.append(p)
        r = r - p.astype(F32)
    return parts


def _rows_dot(blocks, rhs):
    if len(blocks) == 1:
        return [jnp.dot(blocks[0], rhs, preferred_element_type=F32)]
    r = jnp.dot(jnp.concatenate(blocks, axis=0), rhs, preferred_element_type=F32)
    out, o = [], 0
    for blk in blocks:
        out.append(r[o:o + blk.shape[0]])
        o += blk.shape[0]
    return out


def _dot_ls(a, b_exact, n=2):
    parts = _rows_dot(_split(a, n), b_exact)
    acc = parts[0]
    for p in parts[1:]:
        acc = acc + p
    return acc


def _chunk_cumsum(x):
    row = lax.broadcasted_iota(jnp.int32, (x.shape[0], 1), 0) % CHUNK
    sh = 1
    while sh < CHUNK:
        x = x + jnp.where(row >= sh, pltpu.roll(x, sh, 0), 0.0)
        sh *= 2
    return x


def _as_column(row):
    return jnp.broadcast_to(row, (8, row.shape[1])).T[:, 0:1]


def _sigmoid(x):
    return 1.0 / (1.0 + jnp.exp(-x))


def _silu(x):
    return x * _sigmoid(x)


def _block_diag4(y, keep):
    yb = y.astype(BF16)
    return jnp.where(keep, jnp.concatenate([yb, yb, yb, yb], axis=0), jnp.zeros((), BF16))


def _bd_mask():
    r = lax.broadcasted_iota(jnp.int32, (256, 256), 0) // 64
    c = lax.broadcasted_iota(jnp.int32, (256, 256), 1) // 64
    return r == c


def _norm_matmul_kernel(x_ref, g_ref, w_ref, o_ref, h_ref):
    @pl.when(pl.program_id(1) == 0)
    def _():
        x = x_ref[...]
        ms = jnp.mean(x * x, axis=-1, keepdims=True)
        h_ref[...] = (x * lax.rsqrt(ms + EPS) * g_ref[...]).astype(BF16)

    o_ref[...] = jnp.dot(h_ref[...], w_ref[...], preferred_element_type=F32).astype(o_ref.dtype)


def _norm_matmul(x, g, w, layer, tm, tn, out_dtype=F32):
    m, d = x.shape
    n = w.shape[2]
    return pl.pallas_call(
        _norm_matmul_kernel,
        grid=(m // tm, n // tn),
        in_specs=[pl.BlockSpec((tm, d), lambda i, j: (i, 0)),
                  pl.BlockSpec((1, d), lambda i, j: (0, 0)),
                  pl.BlockSpec((None, d, tn), lambda i, j: (layer, 0, j))],
        out_specs=pl.BlockSpec((tm, tn), lambda i, j: (i, j)),
        out_shape=jax.ShapeDtypeStruct((m, n), out_dtype),
        scratch_shapes=[pltpu.VMEM((tm, d), BF16)],
        compiler_params=_cp("parallel", "arbitrary"),
        name="norm_matmul",
    )(x, g.reshape(1, d), w)


def _head_norm(x, bd, g):
    ms = _dot_ls(x * x, bd, 2)
    return x * lax.rsqrt(ms + EPS) * g


def _rope(x, c, s):
    w = x.shape[1]
    lane = lax.broadcasted_iota(jnp.int32, x.shape, 1)
    sw = jnp.where((lane & 32) != 0, pltpu.roll(x, 32, 1), pltpu.roll(x, w - 32, 1))
    return x * c + sw * s


def _nsa_prep_kernel(q_ref, kvc_ref, kvs_ref, kvw_ref, cos_ref, sin_ref, qn_ref, kn_ref, bd512_ref, bd128_ref,
                     qo_ref, kc_ref, vc_ref, kst_ref, kwt_ref, vs_ref, vw_ref, tok_ref):
    c = cos_ref[...]
    s = sin_ref[...]
    c4 = jnp.concatenate([c, c, c, c], axis=1)
    s4 = jnp.concatenate([s, s, s, s], axis=1)
    q = _rope(_head_norm(q_ref[...].astype(F32), bd512_ref[...], qn_ref[...]), c4, s4)
    qo_ref[...] = (q * (HEAD_DIM ** -0.5)).astype(BF16)

    def key(ref, row):
        return _rope(_head_norm(ref[:, 0:128].astype(F32), bd128_ref[...], kn_ref[row:row + 1, :]), c, s)

    tok_ref[0] = key(kvc_ref, 0)
    tok_ref[1] = kvc_ref[:, 128:256].astype(F32)
    n16 = tok_ref.shape[1] // CMP_STRIDE
    for r in range(CMP_STRIDE):
        kc_ref[:, r * 128:(r + 1) * 128] = tok_ref[0, pl.ds(r, n16, stride=CMP_STRIDE), :]
        vc_ref[:, r * 128:(r + 1) * 128] = tok_ref[1, pl.ds(r, n16, stride=CMP_STRIDE), :]
    kst_ref[...] = key(kvs_ref, 1).T.astype(BF16)
    kwt_ref[...] = key(kvw_ref, 2).T.astype(BF16)
    vs = kvs_ref[:, 128:256].astype(BF16)
    vw = kvw_ref[:, 128:256].astype(BF16)
    ones = jnp.ones((vs.shape[0], 64), BF16)
    for g in range(NSA_GROUPS):
        vs_ref[g] = jnp.concatenate([vs[:, g * 64:(g + 1) * 64], ones], axis=1)
        vw_ref[g] = jnp.concatenate([vw[:, g * 64:(g + 1) * 64], ones], axis=1)


def _nsa_prep(z, cos_t, sin_t, q_norm, k_norm, b, s, tm):
    m = b * s
    nt = s // tm
    bd512 = jnp.asarray(np.kron(np.eye(8), np.full((64, 64), 1.0 / 64)), BF16)
    bd128 = jnp.asarray(np.kron(np.eye(2), np.full((64, 64), 1.0 / 64)), BF16)
    qn = jnp.tile(q_norm, 8).reshape(1, 512)
    kn = jnp.tile(k_norm, (1, 2))
    row = lambda i: (i, 0)
    const = lambda i: (0, 0)
    return pl.pallas_call(
        _nsa_prep_kernel,
        grid=(m // tm,),
        in_specs=[pl.BlockSpec((tm, 512), lambda i: (i, COL_NQ // 512)),
                  pl.BlockSpec((tm, 256), lambda i: (i, COL_KV // 256)),
                  pl.BlockSpec((tm, 256), lambda i: (i, COL_KV // 256 + 1)),
                  pl.BlockSpec((tm, 256), lambda i: (i, COL_KV // 256 + 2)),
                  pl.BlockSpec((tm, 128), lambda i: (i % nt, 0)),
                  pl.BlockSpec((tm, 128), lambda i: (i % nt, 0)),
                  pl.BlockSpec((1, 512), const),
                  pl.BlockSpec((3, 128), const),
                  pl.BlockSpec((512, 512), const),
                  pl.BlockSpec((128, 128), const)],
        out_specs=[pl.BlockSpec((tm, 512), row),
                   pl.BlockSpec((tm // CMP_STRIDE, CMP_STRIDE * 128), row),
                   pl.BlockSpec((tm // CMP_STRIDE, CMP_STRIDE * 128), row),
                   pl.BlockSpec((None, 128, tm), lambda i: (i // nt, 0, i % nt)),
                   pl.BlockSpec((None, 128, tm), lambda i: (i // nt, 0, i % nt)),
                   pl.BlockSpec((NSA_GROUPS, tm, 128), lambda i: (0, i, 0)),
                   pl.BlockSpec((NSA_GROUPS, tm, 128), lambda i: (0, i, 0))],
        out_shape=[jax.ShapeDtypeStruct((m, 512), BF16),
                   jax.ShapeDtypeStruct((m // CMP_STRIDE, CMP_STRIDE * 128), F32),
                   jax.ShapeDtypeStruct((m // CMP_STRIDE, CMP_STRIDE * 128), F32),
                   jax.ShapeDtypeStruct((b, 128, s), BF16),
                   jax.ShapeDtypeStruct((b, 128, s), BF16),
                   jax.ShapeDtypeStruct((NSA_GROUPS, m, 128), BF16),
                   jax.ShapeDtypeStruct((NSA_GROUPS, m, 128), BF16)],
        scratch_shapes=[pltpu.VMEM((2, tm, 128), F32)],
        compiler_params=_cp("parallel"),
        name="nsa_prep",
    )(z, z, z, z, cos_t, sin_t, qn, kn, bd512, bd128)


def _gelu_tanh(x):
    return 0.5 * x * (1.0 + jnp.tanh(math.sqrt(2.0 / math.pi) * (x + 0.044715 * (x * x * x))))


def _compress_kernel(rk_ref, rv_ref, pk_ref, pv_ref, wkt_ref, wkb_ref, wvt_ref, wvb_ref, wk2_ref, wv2_ref,
                     kct_ref, vc_ref):
    def mlp(r_ref, p_ref, wt_ref, wb_ref, w2_ref):
        r = r_ref[...]
        n = r.shape[0]
        top = _dot(r + p_ref[0:1, :], wt_ref[...])
        bot = _dot(r + p_ref[1:2, :], wb_ref[...])
        hid = top + pltpu.roll(bot, n - 1, 0)
        return _dot(_gelu_tanh(hid), w2_ref[...])

    kc = mlp(rk_ref, pk_ref, wkt_ref, wkb_ref, wk2_ref)
    vc = mlp(rv_ref, pv_ref, wvt_ref, wvb_ref, wv2_ref)
    kct_ref[...] = kc.T
    for g in range(NSA_GROUPS):
        vc_ref[g] = vc[:, g * 64:(g + 1) * 64]


def _compress_weights(pos, w1, w2):
    w1r = w1.reshape(2, 16, 64, CMP_HIDDEN)
    zero = jnp.zeros_like(w1r)
    big = jnp.stack([jnp.stack([w1r, zero], axis=-2), jnp.stack([zero, w1r], axis=-2)], axis=2)
    big = big.reshape(2, 16 * 2 * 64, 2 * CMP_HIDDEN).astype(BF16)
    posr = jnp.broadcast_to(pos.reshape(2, 16, 1, 64), (2, 16, 2, 64)).reshape(2, 2048)
    w2bd = jnp.zeros((2, CMP_HIDDEN, 2, 64), F32)
    w2bd = w2bd.at[0, :, 0, :].set(w2).at[1, :, 1, :].set(w2).reshape(2 * CMP_HIDDEN, 128).astype(BF16)
    return posr, big[0], big[1], w2bd


def _compress(rk, rv, pos_k, pos_v, ck_w1, ck_w2, cv_w1, cv_w2):
    b, n, _ = rk.shape
    pk, wkt, wkb, wk2 = _compress_weights(pos_k, ck_w1, ck_w2)
    pv, wvt, wvb, wv2 = _compress_weights(pos_v, cv_w1, cv_w2)
    c2 = lambda i: (0, 0)
    return pl.pallas_call(
        _compress_kernel,
        grid=(b,),
        in_specs=[pl.BlockSpec((None, n, 2048), lambda i: (i, 0, 0)),
                  pl.BlockSpec((None, n, 2048), lambda i: (i, 0, 0)),
                  pl.BlockSpec((2, 2048), c2), pl.BlockSpec((2, 2048), c2),
                  pl.BlockSpec((2048, 256), c2), pl.BlockSpec((2048, 256), c2),
                  pl.BlockSpec((2048, 256), c2), pl.BlockSpec((2048, 256), c2),
                  pl.BlockSpec((256, 128), c2), pl.BlockSpec((256, 128), c2)],
        out_specs=[pl.BlockSpec((None, 128, n), lambda i: (i, 0, 0)),
                   pl.BlockSpec((None, NSA_GROUPS, n, 64), lambda i: (i, 0, 0, 0))],
        out_shape=[jax.ShapeDtypeStruct((b, 128, n), F32),
                   jax.ShapeDtypeStruct((b, NSA_GROUPS, n, 64), F32)],
        compiler_params=_cp("parallel"),
        name="nsa_compress",
    )(rk, rv, pk, pv, wkt, wkb, wvt, wvb, wk2, wv2)


def _cmp_attn_kernel(q_ref, kct_ref, vc_ref, ovt_ref, o_ref, sel_ref, imp_ref, *, tq, ncp, nsel):
    i = pl.program_id(1)
    tpos = i * tq + lax.broadcasted_iota(jnp.int32, (tq, 1), 0)
    nblk = lax.broadcasted_iota(jnp.int32, (1, ncp), 1)
    bias = jnp.where((nblk * CMP_STRIDE + (CMP_BLOCK - 1)) <= tpos, 0.0, NEG_BIG)
    any_valid = jnp.where(tpos >= CMP_BLOCK - 1, 1.0, 0.0)
    j = lax.broadcasted_iota(jnp.int32, (nsel, 1), 0)
    cur = (i * tq + lax.broadcasted_iota(jnp.int32, (1, tq), 1)) // SEL_BLOCK
    forced = (j == 0) | (j == cur) | (j == cur - 1)
    jl = lax.broadcasted_iota(jnp.int32, (8, 1), 0)
    ovt = ovt_ref[...]

    scores = []
    for g in range(NSA_GROUPS):
        q4 = jnp.concatenate([q_ref[:, (g * NSA_HPG + h) * 64:(g * NSA_HPG + h + 1) * 64]
                              for h in range(NSA_HPG)], axis=0)
        scores.append(jnp.dot(q4, kct_ref[g * 64:(g + 1) * 64, :].astype(BF16), preferred_element_type=F32))
    psums = []
    for g, s in enumerate(scores):
        s = s.reshape(NSA_HPG, tq, ncp) + bias[None]
        e = jnp.exp(s - jnp.max(s, axis=-1, keepdims=True))
        p = e * (any_valid / jnp.maximum(jnp.sum(e, axis=-1, keepdims=True), 1e-30))
        o = jnp.dot(p.reshape(NSA_HPG * tq, ncp).astype(BF16), vc_ref[g].astype(BF16), preferred_element_type=F32)
        for h in range(NSA_HPG):
            c = (g * NSA_HPG + h) * 64
            o_ref[:, c:c + 64] = o[h * tq:(h + 1) * tq].astype(o_ref.dtype)
        psums.append(p[0] + p[1] + p[2] + p[3])
    for g, psum in enumerate(psums):
        ph, plo = _split(psum, 2)
        imp = _dot_nt(ovt, ph) + _dot_nt(ovt, plo)
        imp_ref[g] = jnp.where(j <= cur, jnp.where(forced, FORCE_SCORE, imp), -1.0)
    for g in range(NSA_GROUPS):
        val = imp_ref[g]
        rows8 = [val[r8 * 8:(r8 + 1) * 8] for r8 in range(nsel // 8)]
        cnt = [jnp.zeros((8, tq), F32) for _ in rows8]
        for r in range(nsel):
            vr = imp_ref[g, r:r + 1, :]
            for r8, vg in enumerate(rows8):
                if r8 * 8 > r:
                    beats = vr >= vg
                elif r8 * 8 + 7 < r:
                    beats = vr > vg
                else:
                    beats = (vr > vg) | ((vr == vg) & (jl + r8 * 8 > r))
                cnt[r8] = cnt[r8] + jnp.where(beats, 1.0, 0.0)
        sel = jnp.where(jnp.concatenate(cnt, axis=0) < float(min(SEL_TOPK, nsel)), 1.0, 0.0)
        sel_ref[g] = sel.T.astype(BF16)


def _cmp_attn(q_r, kct, vc, b, s, tq):
    m = b * s
    nq = s // tq
    ncp = kct.shape[2]
    nsel = s // SEL_BLOCK
    n_cmp = (s - CMP_BLOCK) // CMP_STRIDE + 1
    tok = np.arange(n_cmp)[:, None] * CMP_STRIDE + np.arange(CMP_BLOCK)[None, :]
    ov = np.zeros((ncp, nsel), np.float32)
    for l in range(CMP_BLOCK):
        ov[np.arange(n_cmp), tok[:, l] // SEL_BLOCK] += 1.0
    ovt = jnp.asarray(ov.T, BF16)
    kern = functools.partial(_cmp_attn_kernel, tq=tq, ncp=ncp, nsel=nsel)
    return pl.pallas_call(
        kern,
        grid=(b, nq),
        in_specs=[pl.BlockSpec((tq, 512), lambda bb, i: (bb * nq + i, 0)),
                  pl.BlockSpec((None, 128, ncp), lambda bb, i: (bb, 0, 0)),
                  pl.BlockSpec((None, NSA_GROUPS, ncp, 64), lambda bb, i: (bb, 0, 0, 0)),
                  pl.BlockSpec((nsel, ncp), lambda bb, i: (0, 0))],
        out_specs=[pl.BlockSpec((tq, 512), lambda bb, i: (bb * nq + i, 0)),
                   pl.BlockSpec((NSA_GROUPS, tq, nsel), lambda bb, i: (0, bb * nq + i, 0))],
        out_shape=[jax.ShapeDtypeStruct((m, 512), BF16),
                   jax.ShapeDtypeStruct((NSA_GROUPS, m, nsel), BF16)],
        scratch_shapes=[pltpu.VMEM((NSA_GROUPS, nsel, tq), F32)],
        compiler_params=_cp("parallel", "parallel"),
        name="nsa_cmp_attn",
    )(q_r, kct, vc, ovt)


def _sel_attn_kernel(qi_ref, kt_ref, first_ref, last_ref, q_ref, k_ref, v_ref, sel_ref, o_ref, q4_ref, m_ref, acc_ref,
                     *, tq, tk, nsel):
    n = pl.program_id(1)
    qi = qi_ref[n]
    ktile = kt_ref[n]
    rows = NSA_HPG * tq
    groups = [slice(g * rows, (g + 1) * rows) for g in range(NSA_GROUPS)]

    @pl.when(first_ref[n] == 1)
    def _():
        for g in range(NSA_GROUPS):
            unsel = ((sel_ref[g].astype(F32) - 1.0) * SEL_MASK).astype(BF16)
            for h in range(NSA_HPG):
                r0 = g * rows + h * tq
                q4_ref[r0:r0 + tq, 0:64] = q_ref[:, (g * NSA_HPG + h) * 64:(g * NSA_HPG + h + 1) * 64]
                q4_ref[r0:r0 + tq, 64:64 + nsel] = unsel
        m_ref[...] = jnp.full(m_ref.shape, NEG_BIG, F32)
        acc_ref[...] = jnp.zeros(acc_ref.shape, F32)

    def step(with_causal):
        kpos = ktile * tk + lax.broadcasted_iota(jnp.int32, (1, tk), 1)
        blk = lax.broadcasted_iota(jnp.int32, (nsel, 1), 0)
        own_block = jnp.where(blk == kpos // SEL_BLOCK, 1.0, 0.0).astype(BF16)
        scores = [jnp.dot(q4_ref[rs, :], jnp.concatenate([k_ref[g * 64:(g + 1) * 64, :], own_block], axis=0),
                          preferred_element_type=F32) for g, rs in enumerate(groups)]
        if with_causal:
            qpos = qi * tq + lax.broadcasted_iota(jnp.int32, (tq, 1), 0)
            causal = jnp.where(kpos <= qpos, 0.0, NEG_BIG)
        for g, (rs, s) in enumerate(zip(groups, scores)):
            if with_causal:
                s = (s.reshape(NSA_HPG, tq, tk) + causal[None]).reshape(rows, tk)
            m_prev = m_ref[rs, :]
            m_next = jnp.maximum(m_prev, jnp.max(s, axis=-1, keepdims=True))
            p = jnp.exp(s - jnp.tile(m_next, (1, tk // 128)))
            pv = jnp.dot(p.astype(BF16), v_ref[g], preferred_element_type=F32)
            acc_ref[rs, :] = jnp.exp(m_prev - m_next) * acc_ref[rs, :] + pv
            m_ref[rs, :] = m_next

    reaches_past = ktile * tk + tk - 1 > qi * tq
    pl.when(reaches_past)(functools.partial(step, True))
    pl.when(jnp.logical_not(reaches_past))(functools.partial(step, False))

    @pl.when(last_ref[n] == 1)
    def _():
        a = acc_ref[...]
        o = a / pltpu.roll(a, 64, 1)
        for hh in range(NSA_HEADS):
            o_ref[:, hh * 64:(hh + 1) * 64] = o[hh * tq:(hh + 1) * tq, 0:64].astype(o_ref.dtype)


def _sel_attn(q_r, kt, v, sel, b, s, tq, tk):
    m = b * s
    nq = s // tq
    nkb = s // tk
    nsel = s // SEL_BLOCK
    pairs = []
    for qi in range(nq):
        kts = [k for k in range(nkb) if k * tk <= qi * tq + tq - 1]
        pairs += [(qi, k, int(j == 0), int(j == len(kts) - 1)) for j, k in enumerate(kts)]
    tabs = [jnp.asarray(np.array([pr[c] for pr in pairs], np.int32)) for c in range(4)]
    kern = functools.partial(_sel_attn_kernel, tq=tq, tk=tk, nsel=nsel)
    return pl.pallas_call(
        kern,
        grid_spec=pltpu.PrefetchScalarGridSpec(
            num_scalar_prefetch=4,
            grid=(b, len(pairs)),
            in_specs=[pl.BlockSpec((tq, 512), lambda bb, n, qt, kt_, f, l: (bb * nq + qt[n], 0)),
                      pl.BlockSpec((None, 128, tk), lambda bb, n, qt, kt_, f, l: (bb, 0, kt_[n])),
                      pl.BlockSpec((NSA_GROUPS, tk, 128), lambda bb, n, qt, kt_, f, l: (0, bb * nkb + kt_[n], 0)),
                      pl.BlockSpec((NSA_GROUPS, tq, nsel), lambda bb, n, qt, kt_, f, l: (0, bb * nq + qt[n], 0))],
            out_specs=pl.BlockSpec((tq, 512), lambda bb, n, qt, kt_, f, l: (bb * nq + qt[n], 0)),
            scratch_shapes=[pltpu.VMEM((NSA_HEADS * tq, 64 + nsel), BF16), pltpu.VMEM((NSA_HEADS * tq, 128), F32),
                            pltpu.VMEM((NSA_HEADS * tq, 128), F32)]),
        out_shape=jax.ShapeDtypeStruct((m, 512), BF16),
        compiler_params=_cp("parallel", "arbitrary"),
        name="nsa_sel_attn",
    )(*tabs, q_r, kt, v, sel)


def _win_attn_kernel(q_ref, *refs, tq, nblk):
    k_refs, v_refs, o_ref = refs[:nblk], refs[nblk:2 * nblk], refs[2 * nblk]
    qi = pl.program_id(1)
    kpos = (qi - (nblk - 1)) * tq + lax.broadcasted_iota(jnp.int32, (1, nblk * tq), 1)
    dist = qi * tq + lax.broadcasted_iota(jnp.int32, (tq, 1), 0) - kpos
    bias = jnp.where((dist >= 0) & (dist < WINDOW) & (kpos >= 0), 0.0, NEG_BIG)
    scores = []
    for g in range(NSA_GROUPS):
        q4 = jnp.concatenate([q_ref[:, (g * NSA_HPG + h) * 64:(g * NSA_HPG + h + 1) * 64]
                              for h in range(NSA_HPG)], axis=0)
        kt = jnp.concatenate([r[g * 64:(g + 1) * 64, :] for r in k_refs], axis=1)
        scores.append(jnp.dot(q4, kt, preferred_element_type=F32))
    for g, s in enumerate(scores):
        s = (s.reshape(NSA_HPG, tq, nblk * tq) + bias[None]).reshape(NSA_HPG * tq, nblk * tq)
        p = jnp.exp(s - jnp.max(s, axis=-1, keepdims=True))
        v = jnp.concatenate([r[g] for r in v_refs], axis=0)
        pv = jnp.dot(p.astype(BF16), v, preferred_element_type=F32)
        o = pv / pltpu.roll(pv, 64, 1)
        for h in range(NSA_HPG):
            c = (g * NSA_HPG + h) * 64
            o_ref[:, c:c + 64] = o[h * tq:(h + 1) * tq, 0:64].astype(o_ref.dtype)


def _win_attn(q_r, kt, v, b, s, tq):
    m = b * s
    nq = s // tq
    nblk = WINDOW // tq + 1
    kidx = lambda qi, j: jnp.maximum(qi - (nblk - 1) + j, 0)
    k_specs = [pl.BlockSpec((None, 128, tq), functools.partial(lambda bb, qi, j: (bb, 0, kidx(qi, j)), j=j))
               for j in range(nblk)]
    v_specs = [pl.BlockSpec((NSA_GROUPS, tq, 128), functools.partial(lambda bb, qi, j: (0, bb * nq + kidx(qi, j), 0), j=j))
               for j in range(nblk)]
    kern = functools.partial(_win_attn_kernel, tq=tq, nblk=nblk)
    return pl.pallas_call(
        kern,
        grid=(b, nq),
        in_specs=[pl.BlockSpec((tq, 512), lambda bb, qi: (bb * nq + qi, 0))] + k_specs + v_specs,
        out_specs=pl.BlockSpec((tq, 512), lambda bb, qi: (bb * nq + qi, 0)),
        out_shape=jax.ShapeDtypeStruct((m, 512), BF16),
        compiler_params=_cp("parallel", "parallel"),
        name="nsa_win_attn",
    )(q_r, *([kt] * nblk), *([v] * nblk))


def _consts_recurrent():
    bd_mean = np.kron(np.eye(4), np.full((64, 64), 1.0 / 64))
    bd_ones = np.kron(np.eye(4), np.ones((64, 64)))
    return jnp.asarray(bd_mean, BF16), jnp.asarray(bd_ones, BF16)


def _group_rmsnorm(o, bd_mean, g):
    ms = _dot_ls(o * o, bd_mean, 2)
    return o * lax.rsqrt(ms + EPS) * g


def _hgrn_kernel(q_ref, f_ref, i_ref, g_ref, lbl_ref, on_ref, bdm_ref, bdo_ref, selp_ref,
                 o_ref, st_ref, z_ref, *, layer, depth, t):
    @pl.when(pl.program_id(1) == 0)
    def _():
        st_ref[...] = jnp.zeros(st_ref.shape, F32)

    lg = lbl_ref[...]
    mx = jnp.max(lg, axis=0, keepdims=True)
    ex = jnp.exp(lg - mx)
    pr = ex / jnp.sum(ex, axis=0, keepdims=True)
    cs = pr[0:1, :]
    for r in range(1, layer + 1):
        cs = cs + pr[r:r + 1, :]
    lb = cs - pr[0:1, :]

    q = _silu(q_ref[...])
    fz = f_ref[...]
    f = lb + (1.0 - lb) * _sigmoid(fz)
    logf = jnp.log(jnp.maximum(f, TINY))
    k = (1.0 - lb) * _sigmoid(-fz)
    v = i_ref[...]
    b = _chunk_cumsum(logf)

    bdm = _bd_mask()
    bdk = bdo_ref[...] > 0
    nsub = t // SUB
    tl = lax.broadcasted_iota(jnp.int32, (1, SUB, 1), 1)
    for hp in range(2):
        ls = slice(hp * 128, (hp + 1) * 128)
        q3 = q[:, ls].reshape(nsub, SUB, 128)
        k3 = k[:, ls].reshape(nsub, SUB, 128)
        b3 = b[:, ls].reshape(nsub, SUB, 128)
        for sl in range(SUB):
            e = jnp.exp(jnp.where(tl >= sl, b3 - b3[:, sl:sl + 1, :], NEG_BIG))
            zz = q3 * k3[:, sl:sl + 1, :] * e
            z_ref[hp * t:(hp + 1) * t, sl * 128:(sl + 1) * 128] = zz.reshape(t, 128)
    a_pairs = _dot_ls(z_ref[...], selp_ref[...], 2)
    a_all = jnp.concatenate([a_pairs[0:t], a_pairs[t:2 * t]], axis=1)
    trow = lax.broadcasted_iota(jnp.int32, (CHUNK, 256), 0)
    scol = lax.broadcasted_iota(jnp.int32, (CHUNK, 256), 1) % CHUNK
    diag_mask = (trow // SUB) == (scol // SUB)
    m1 = (trow >= 32) & (scol < 32)
    m2 = ((trow >= 16) & (trow < 32) & (scol < 16)) | ((trow >= 48) & (scol >= 32) & (scol < 48))
    rowi = lax.broadcasted_iota(jnp.int32, (CHUNK, 1), 0)

    chunks = [slice(c * CHUNK, (c + 1) * CHUNK) for c in range(t // CHUNK)]
    lvl1, lvl2 = [], []
    for rs in chunks:
        qc, kc, bc = q[rs], k[rs], b[rs]
        r1 = bc[31:32, :]
        q1 = qc * jnp.exp(jnp.minimum(bc - r1, 0.0))
        k1 = jnp.where(rowi < 32, kc * jnp.exp(jnp.minimum(r1 - bc, 0.0)), 0.0)
        r2 = jnp.where(rowi < 32, bc[15:16, :], bc[47:48, :])
        q2 = qc * jnp.exp(jnp.minimum(bc - r2, 0.0))
        k2 = kc * jnp.exp(jnp.minimum(r2 - bc, 0.0))
        lvl1.append(_dot_nt(q1, _block_diag4(k1, bdk)))
        lvl2.append(_dot_nt(q2, _block_diag4(k2, bdk)))
    o_intra, s_add, s_dec, q_dec = [], [], [], []
    for rs, a1, a2 in zip(chunks, lvl1, lvl2):
        kc, vc, bc = k[rs], v[rs], b[rs]
        attn = jnp.where(diag_mask, a_all[rs], 0.0) + jnp.where(m1, a1, 0.0) + jnp.where(m2, a2, 0.0)
        o_intra.append(_dot(attn, _block_diag4(vc, bdk)))
        bl = bc[CHUNK - 1:CHUNK, :]
        kdt = (kc * jnp.exp(bl - bc)).T
        s_add.append(jnp.where(bdm, _dot(kdt, vc), 0.0))
        s_dec.append(_as_column(jnp.exp(bl)))
        q_dec.append((q[rs] * jnp.exp(bc)).astype(BF16))

    st = st_ref[...]
    outs = []
    for oi, sa, sd, qd in zip(o_intra, s_add, s_dec, q_dec):
        outs.append(oi + jnp.dot(qd, st.astype(BF16), preferred_element_type=F32))
        st = sd * st + sa
    st_ref[...] = st
    o = jnp.concatenate(outs, axis=0)
    o_ref[...] = (_group_rmsnorm(o, bdm_ref[...], on_ref[...]) * _sigmoid(g_ref[...])).astype(BF16)


def _hgrn(z, lb_logits, out_norm, layer, b, s, t):
    m = b * s
    nt = s // t
    depth = lb_logits.shape[0]
    bd_mean, bd_ones = _consts_recurrent()
    selp = np.zeros((SUB, 2, 64, 2, CHUNK), np.float32)
    for sl in range(SUB):
        for h2 in range(2):
            selp[sl, h2, :, h2, sl::SUB] = 1.0
    selp = jnp.asarray(selp.reshape(SUB * 128, 128), BF16)
    col = COLB_HG // 256
    c2 = lambda bb, i: (0, 0)
    kern = functools.partial(_hgrn_kernel, layer=layer, depth=depth, t=t)
    return pl.pallas_call(
        kern,
        grid=(b, nt),
        in_specs=[pl.BlockSpec((t, 256), lambda bb, i: (bb * nt + i, col)),
                  pl.BlockSpec((t, 256), lambda bb, i: (bb * nt + i, col + 1)),
                  pl.BlockSpec((t, 256), lambda bb, i: (bb * nt + i, col + 2)),
                  pl.BlockSpec((t, 256), lambda bb, i: (bb * nt + i, col + 3)),
                  pl.BlockSpec((depth, 256), c2),
                  pl.BlockSpec((1, 256), c2),
                  pl.BlockSpec((256, 256), c2),
                  pl.BlockSpec((256, 256), c2),
                  pl.BlockSpec((SUB * 128, 128), c2)],
        out_specs=pl.BlockSpec((t, 256), lambda bb, i: (bb * nt + i, 0)),
        out_shape=jax.ShapeDtypeStruct((m, 256), BF16),
        scratch_shapes=[pltpu.VMEM((256, 256), F32), pltpu.VMEM((2 * t, SUB * 128), F32)],
        compiler_params=_cp("parallel", "arbitrary"),
        name="hgrn2",
    )(z, z, z, z, lb_logits, jnp.tile(out_norm, 4).reshape(1, 256), bd_mean, bd_ones, selp)


def _gdn_kernel(q_ref, k_ref, v_ref, z_ref, ba_ref, cw_ref, al_ref, dt_ref, on_ref, bdm_ref, bdo_ref,
                eb_ref, ea_ref, o_ref, st_ref, prev_ref, *, t):
    @pl.when(pl.program_id(1) == 0)
    def _():
        st_ref[...] = jnp.zeros(st_ref.shape, F32)
        prev_ref[...] = jnp.zeros(prev_ref.shape, F32)

    def conv_silu(x_ref, p):
        x = x_ref[...]
        xp = jnp.concatenate([prev_ref[:, p * 256:(p + 1) * 256], x], axis=0)
        w = cw_ref[:, p * 256:(p + 1) * 256]
        y = w[3:4, :] * xp[8:, :]
        for j in range(1, 4):
            y = y + w[3 - j:4 - j, :] * pltpu.roll(xp, j, 0)[8:, :]
        return _silu(y), x[t - 8:, :]

    qa, qtail = conv_silu(q_ref, 0)
    ka, ktail = conv_silu(k_ref, 1)
    va, vtail = conv_silu(v_ref, 2)
    prev_ref[:, 0:256] = qtail
    prev_ref[:, 256:512] = ktail
    prev_ref[:, 512:768] = vtail

    bdo = bdo_ref[...]
    bdk = bdo > 0
    ss = _dot_ls(jnp.concatenate([qa * qa, ka * ka], axis=0), bdo, 2)
    q = qa * lax.rsqrt(ss[0:t] + EPS) * (HEAD_DIM ** -0.5)
    k = ka * lax.rsqrt(ss[t:2 * t] + EPS)
    v = va

    gl = ba_ref[...].astype(F32)
    beta = _dot_ls(_sigmoid(gl), eb_ref[...], 2)
    xs = gl + dt_ref[...]
    softplus = jnp.maximum(xs, 0.0) + jnp.log(1.0 + jnp.exp(-jnp.abs(xs)))
    b = _dot_ls(_chunk_cumsum(-jnp.exp(al_ref[...]) * softplus), ea_ref[...], 3)

    bdm = _bd_mask()
    trow = lax.broadcasted_iota(jnp.int32, (CHUNK, 256), 0)
    scol = lax.broadcasted_iota(jnp.int32, (CHUNK, 256), 1) % CHUNK
    incl = scol <= trow
    strict = scol < trow
    eye = jnp.where(scol == trow, 1.0, 0.0)

    chunks = [slice(c * CHUNK, (c + 1) * CHUNK) for c in range(t // CHUNK)]
    nch = len(chunks)
    aw, aq = [], []
    for rs in chunks:
        qc, kc, bc = q[rs], k[rs], b[rs]
        brow = jnp.concatenate([bc[:, h * 64:(h + 1) * 64].T for h in range(N_HEADS4)], axis=1)
        e = jnp.exp(jnp.where(incl, bc - brow, 0.0))
        gram = _dot_nt(jnp.concatenate([kc * beta[rs], qc], axis=0), _block_diag4(kc, bdk))
        aw.append(jnp.where(strict, gram[:CHUNK] * e, 0.0))
        aq.append(jnp.where(incl, gram[CHUNK:] * e, 0.0).astype(BF16))
    pw = [None] * nch
    tinv = [None] * nch
    for c in range(nch):
        bh, bl_ = _split(-aw[c], 2)
        r1 = _rows_dot([bh, bl_], _block_diag4(bh, bdk))
        pw[c] = r1[0] + r1[1] + jnp.dot(bh, _block_diag4(bl_, bdk), preferred_element_type=F32)
        tinv[c] = eye - aw[c]
    for c in range(nch):
        ph, pl_ = _split(pw[c], 2)
        th, tl_ = _split(tinv[c], 2)
        r1 = _rows_dot([ph, pl_, th, tl_], _block_diag4(ph, bdk))
        r2 = _rows_dot([ph, th], _block_diag4(pl_, bdk))
        pw[c] = r1[0] + r1[1] + r2[0]
        tinv[c] = tinv[c] + r1[2] + r1[3] + r2[1]
    for _ in range(3):
        for c in range(nch):
            ph = pw[c].astype(BF16)
            r1 = _rows_dot([ph, tinv[c].astype(BF16)], _block_diag4(ph, bdk))
            pw[c] = r1[0]
            tinv[c] = tinv[c] + r1[1]
    for c in range(nch):
        tinv[c] = tinv[c] + jnp.dot(tinv[c].astype(BF16), _block_diag4(pw[c], bdk), preferred_element_type=F32)
    u, w = [], []
    for c, rs in enumerate(chunks):
        t2 = _split(tinv[c], 2)
        kbe = k[rs] * beta[rs] * jnp.exp(b[rs])
        r1 = _rows_dot(t2, _block_diag4(v[rs] * beta[rs], bdk))
        r2 = _rows_dot(t2, _block_diag4(kbe, bdk))
        u.append(r1[0] + r1[1])
        w.append(r2[0] + r2[1])
    lhs, s_add, s_dec, o0 = [], [], [], []
    for c, rs in enumerate(chunks):
        bc = b[rs]
        bl = bc[CHUNK - 1:CHUNK, :]
        kdt = (k[rs] * jnp.exp(bl - bc)).T.astype(BF16)
        rn = jnp.dot(kdt, jnp.concatenate([w[c], u[c]], axis=1).astype(BF16), preferred_element_type=F32)
        qeff = q[rs] * jnp.exp(bc) - jnp.dot(aq[c], _block_diag4(w[c], bdk), preferred_element_type=F32)
        lhs.append(jnp.concatenate([qeff.astype(BF16), jnp.where(bdm, rn[:, 0:256], 0.0).astype(BF16)], axis=0))
        s_add.append(jnp.where(bdm, rn[:, 256:512], 0.0))
        s_dec.append(_as_column(jnp.exp(bl)))
        o0.append(jnp.dot(aq[c], _block_diag4(u[c], bdk), preferred_element_type=F32))

    st = st_ref[...]
    outs = []
    for c in range(nch):
        r = jnp.dot(lhs[c], st.astype(BF16), preferred_element_type=F32)
        outs.append(o0[c] + r[0:CHUNK])
        st = s_dec[c] * st - r[CHUNK:] + s_add[c]
    st_ref[...] = st
    o = jnp.concatenate(outs, axis=0)
    o_ref[...] = (_group_rmsnorm(o, bdm_ref[...], on_ref[...]) * _silu(z_ref[...])).astype(BF16)


def _gdn(za, zb, conv_w, a_log, dt_bias, out_norm, b, s, t):
    m = b * s
    nt = s // t
    bd_mean, bd_ones = _consts_recurrent()
    eb = np.zeros((128, 256), np.float32)
    ea = np.zeros((128, 256), np.float32)
    for h in range(N_HEADS4):
        eb[h, h * 64:(h + 1) * 64] = 1.0
        ea[4 + h, h * 64:(h + 1) * 64] = 1.0
    al = jnp.zeros((1, 128), F32).at[0, 4:8].set(a_log)
    dt = jnp.zeros((1, 128), F32).at[0, 4:8].set(dt_bias)
    col = COLB_GD // 256
    c2 = lambda bb, i: (0, 0)
    kern = functools.partial(_gdn_kernel, t=t)
    return pl.pallas_call(
        kern,
        grid=(b, nt),
        in_specs=[pl.BlockSpec((t, 256), lambda bb, i: (bb * nt + i, col)),
                  pl.BlockSpec((t, 256), lambda bb, i: (bb * nt + i, col + 1)),
                  pl.BlockSpec((t, 256), lambda bb, i: (bb * nt + i, col + 2)),
                  pl.BlockSpec((t, 256), lambda bb, i: (bb * nt + i, col + 3)),
                  pl.BlockSpec((t, 128), lambda bb, i: (bb * nt + i, COL_BA // 128)),
                  pl.BlockSpec((4, 768), c2),
                  pl.BlockSpec((1, 128), c2), pl.BlockSpec((1, 128), c2),
                  pl.BlockSpec((1, 256), c2),
                  pl.BlockSpec((256, 256), c2), pl.BlockSpec((256, 256), c2),
                  pl.BlockSpec((128, 256), c2), pl.BlockSpec((128, 256), c2)],
        out_specs=pl.BlockSpec((t, 256), lambda bb, i: (bb * nt + i, 0)),
        out_shape=jax.ShapeDtypeStruct((m, 256), BF16),
        scratch_shapes=[pltpu.VMEM((256, 256), F32), pltpu.VMEM((8, 768), F32)],
        compiler_params=_cp("parallel", "arbitrary"),
        name="gdn",
    )(zb, zb, zb, zb, za, conv_w, al, dt, jnp.tile(out_norm, 4).reshape(1, 256), bd_mean, bd_ones,
      jnp.asarray(eb, BF16), jnp.asarray(ea, BF16))


def _merge_xattn_kernel(x_ref, oc_ref, os_ref, ow_ref, ng_ref, yb_ref, yc_ref, ma_ref, mb_ref, mc_ref,
                        ex_ref, wa_ref, wb_ref, wc_ref, wmix_ref, g_ref, wq_ref, qn_ref, kn_ref, mkv_ref, wo_ref,
                        o_ref):
    sg = _sigmoid(ng_ref[...].astype(F32))
    ya = (_dot_ls(sg, ex_ref[0], 2) * oc_ref[...].astype(F32) + _dot_ls(sg, ex_ref[1], 2) * os_ref[...].astype(F32)
          + _dot_ls(sg, ex_ref[2], 2) * ow_ref[...].astype(F32))
    merged = (_sigmoid(ma_ref[...].astype(F32)) * _dot(ya, wa_ref[...])
              + _sigmoid(mb_ref[...].astype(F32)) * jnp.dot(yb_ref[...], wb_ref[...], preferred_element_type=F32)
              + _sigmoid(mc_ref[...].astype(F32)) * jnp.dot(yc_ref[...], wc_ref[...], preferred_element_type=F32))
    x = x_ref[...] + _dot(merged, wmix_ref[...])

    h = x * lax.rsqrt(jnp.mean(x * x, axis=-1, keepdims=True) + EPS) * g_ref[...]
    q = _dot(h, wq_ref[...])
    scores = []
    for hd in range(N_HEADS4):
        ls = slice(hd * 128, (hd + 1) * 128)
        qh = q[:, ls]
        qh = qh * lax.rsqrt(jnp.mean(qh * qh, axis=-1, keepdims=True) + EPS) * qn_ref[...]
        kh = mkv_ref[:, ls]
        kh = kh * lax.rsqrt(jnp.mean(kh * kh, axis=-1, keepdims=True) + EPS) * kn_ref[...]
        scores.append(_dot_nt(qh, kh) * (XATTN_HEAD_DIM ** -0.5))
    outs = []
    for hd, s in enumerate(scores):
        e = jnp.exp(s - jnp.max(s, axis=-1, keepdims=True))
        pv = _dot(e, mkv_ref[:, 512 + hd * 128:512 + (hd + 1) * 128])
        outs.append(pv / jnp.sum(e, axis=-1, keepdims=True))
    o = jnp.concatenate(outs, axis=1)
    o_ref[...] = x + _dot(o, wo_ref[...])


def _merge_xattn(x2, o_cmp, o_sel, o_win, z, yb, yc, wa, wb, wc, wmix, g, wq, qn, kn, mkv, wo, layer, s, tm):
    m, d = x2.shape
    wl = lambda i: (layer, 0, 0)
    nt = s // tm
    ex = np.zeros((3, 128, 512), np.float32)
    for c in range(3):
        for h in range(NSA_HEADS):
            ex[c, c * NSA_HEADS + h, h * 64:(h + 1) * 64] = 1.0
    row = lambda i: (i, 0)
    c2 = lambda i: (0, 0)
    once = dict(pipeline_mode=pl.Buffered(1))
    return pl.pallas_call(
        _merge_xattn_kernel,
        grid=(m // tm,),
        in_specs=[pl.BlockSpec((tm, d), row),
                  pl.BlockSpec((tm, 512), row), pl.BlockSpec((tm, 512), row), pl.BlockSpec((tm, 512), row),
                  pl.BlockSpec((tm, 128), lambda i: (i, COL_NG // 128)),
                  pl.BlockSpec((tm, 256), row), pl.BlockSpec((tm, 256), row),
                  pl.BlockSpec((tm, 1024), lambda i: (i, 0)),
                  pl.BlockSpec((tm, 1024), lambda i: (i, 1)),
                  pl.BlockSpec((tm, 1024), lambda i: (i, 2)),
                  pl.BlockSpec((3, 128, 512), lambda i: (0, 0, 0)),
                  pl.BlockSpec((None, 512, d), wl, **once), pl.BlockSpec((None, 256, d), wl, **once),
                  pl.BlockSpec((None, 256, d), wl, **once),
                  pl.BlockSpec((None, d, d), wl, **once),
                  pl.BlockSpec((1, d), c2),
                  pl.BlockSpec((None, d, 512), wl, **once),
                  pl.BlockSpec((1, 128), c2), pl.BlockSpec((1, 128), c2),
                  pl.BlockSpec((N_MEM, 1024), lambda i: (i // nt, 0)),
                  pl.BlockSpec((None, 512, d), wl, **once)],
        out_specs=pl.BlockSpec((tm, d), row),
        out_shape=jax.ShapeDtypeStruct((m, d), F32),
        compiler_params=_cp("parallel"),
        name="merge_xattn",
    )(x2, o_cmp, o_sel, o_win, z, yb, yc, z, z, z, jnp.asarray(ex, BF16), wa, wb, wc, wmix,
      g.reshape(1, d), wq, qn.reshape(1, 128), kn.reshape(1, 128), mkv, wo)


FFN_HALO = 16


def _ffn_kernel(x_ref, xh_ref, g_ref, wua_ref, wub_ref, cwa_ref, cwb_ref, wd_ref, o_ref, h_ref, acc_ref, *, nt, tm):
    i = pl.program_id(0)
    f = pl.program_id(1)

    @pl.when(f == 0)
    def _():
        def norm(x):
            return x * lax.rsqrt(jnp.mean(x * x, axis=-1, keepdims=True) + EPS) * g_ref[...]
        first = (i % nt) == 0
        h_ref[0:FFN_HALO, :] = jnp.where(first, 0.0, norm(xh_ref[...])).astype(BF16)
        h_ref[FFN_HALO:, :] = norm(x_ref[...]).astype(BF16)
        acc_ref[...] = jnp.zeros(acc_ref.shape, F32)

    h = h_ref[...]

    def up_conv(w_ref, cw_ref):
        u = jnp.dot(h, w_ref[...], preferred_element_type=F32)
        cw = cw_ref[...]
        y = cw[2:3, :] * u + cw[1:2, :] * pltpu.roll(u, 1, 0) + cw[0:1, :] * pltpu.roll(u, 2, 0)
        return y[FFN_HALO:, :]

    a = up_conv(wua_ref, cwa_ref)
    bb = up_conv(wub_ref, cwb_ref)
    acc_ref[...] += _dot(_silu(a) * bb, wd_ref[...])

    @pl.when(f == pl.num_programs(1) - 1)
    def _():
        o_ref[...] = x_ref[...] + acc_ref[...]


def _ffn(x2, g, w_up, conv_w, w_down, layer, s, tm, tf):
    m, d = x2.shape
    nt = s // tm
    nf = D_FF // tf
    hb = tm // FFN_HALO
    kern = functools.partial(_ffn_kernel, nt=nt, tm=tm)
    return pl.pallas_call(
        kern,
        grid=(m // tm, nf),
        in_specs=[pl.BlockSpec((tm, d), lambda i, f: (i, 0)),
                  pl.BlockSpec((FFN_HALO, d), lambda i, f: (jnp.maximum(i * hb - 1, 0), 0)),
                  pl.BlockSpec((1, d), lambda i, f: (0, 0)),
                  pl.BlockSpec((None, d, tf), lambda i, f: (layer, 0, f)),
                  pl.BlockSpec((None, d, tf), lambda i, f: (layer, 0, nf + f)),
                  pl.BlockSpec((None, 3, tf), lambda i, f: (layer, 0, f)),
                  pl.BlockSpec((None, 3, tf), lambda i, f: (layer, 0, nf + f)),
                  pl.BlockSpec((None, tf, d), lambda i, f: (layer, f, 0))],
        out_specs=pl.BlockSpec((tm, d), lambda i, f: (i, 0)),
        out_shape=jax.ShapeDtypeStruct((m, d), F32),
        scratch_shapes=[pltpu.VMEM((tm + FFN_HALO, d), BF16), pltpu.VMEM((tm, d), F32)],
        compiler_params=_cp("parallel", "arbitrary"),
        name="conv_glu_ffn",
    )(x2, x2, g.reshape(1, d), w_up, w_up, conv_w, conv_w, w_down)


def _permute_w_in(w):
    nl, d = w.shape[0], w.shape[1]
    gate = w[:, :, 1280:1304].reshape(nl, d, NSA_HEADS, 3).transpose(0, 1, 3, 2).reshape(nl, d, 24)
    z104 = jnp.zeros((nl, d, 104), w.dtype)
    z120 = jnp.zeros((nl, d, 120), w.dtype)
    wa = jnp.concatenate([w[:, :, 3360:6432], w[:, :, 0:1280], gate, z104, w[:, :, 3352:3360], z120], axis=2)
    return wa.astype(BF16), w[:, :, 1304:3352].astype(BF16)


def _rope_tables(s):
    inv_freq = (1.0 / (ROPE_THETA ** (np.arange(0, HEAD_DIM, 2, dtype=np.float32) / HEAD_DIM))).astype(np.float32)
    ang = np.arange(s, dtype=np.float32)[:, None] * inv_freq[None, :]
    c, sn = np.cos(ang), np.sin(ang)
    return (jnp.asarray(np.tile(np.concatenate([c, c], axis=1), (1, 2)), F32),
            jnp.asarray(np.tile(np.concatenate([-sn, sn], axis=1), (1, 2)), F32))


def kernel(x, mem, mem_norm, mem_w_kv, hgrn_lb_logits, norm_mix, w_in, nsa_q_norm, nsa_k_norm, cmp_pos_k, cmp_pos_v, cmp_k_w1, cmp_k_w2, cmp_v_w1, cmp_v_w2, hgrn_out_norm, gdn_conv, gdn_a_log, gdn_dt_bias, gdn_out_norm, w_branch_a, w_branch_b, w_branch_c, w_mix_out, norm_cross, xattn_wq, xattn_q_norm, xattn_k_norm, xattn_wo, norm_ffn, ffn_w_up, ffn_conv, ffn_w_down):
    b, s, d = x.shape
    m = b * s
    depth = w_in.shape[0]
    cos_t, sin_t = _rope_tables(s)
    x2 = x.reshape(m, d)
    mkv = _norm_matmul(mem.reshape(b * N_MEM, d), mem_norm, mem_w_kv.astype(BF16)[None], 0, N_MEM, 512)

    tm_in = min(1024, m)
    w_a, w_b = _permute_w_in(w_in)
    w_up, w_down = ffn_w_up.astype(BF16), ffn_w_down.astype(BF16)
    w_br_a, w_br_b, w_br_c = w_branch_a.astype(BF16), w_branch_b.astype(BF16), w_branch_c.astype(BF16)
    w_mix, w_xq, w_xo = w_mix_out.astype(BF16), xattn_wq.astype(BF16), xattn_wo.astype(BF16)
    for l in range(depth):
        z = _norm_matmul(x2, norm_mix[l], w_a, l, tm_in, WIDTH_A // 3, BF16)
        zb = _norm_matmul(x2, norm_mix[l], w_b, l, tm_in, WIDTH_B // 2)
        q_r, kcmp, vcmp, kst, kwt, vs, vw = _nsa_prep(z, cos_t, sin_t, nsa_q_norm[l], nsa_k_norm[l], b, s, 512)
        rk = kcmp.reshape(b, s // CMP_STRIDE, CMP_STRIDE * 128)
        rv = vcmp.reshape(b, s // CMP_STRIDE, CMP_STRIDE * 128)
        kct, vc = _compress(rk, rv, cmp_pos_k[l], cmp_pos_v[l], cmp_k_w1[l], cmp_k_w2[l], cmp_v_w1[l], cmp_v_w2[l])
        o_cmp, sel = _cmp_attn(q_r, kct, vc, b, s, 256)
        o_sel = _sel_attn(q_r, kst, vs, sel, b, s, 512, 512)
        o_win = _win_attn(q_r, kwt, vw, b, s, 256)
        yb = _hgrn(zb, hgrn_lb_logits, hgrn_out_norm[l], l, b, s, 512)
        yc = _gdn(z, zb, gdn_conv[l], gdn_a_log[l], gdn_dt_bias[l], gdn_out_norm[l], b, s, 512)
        x2 = _merge_xattn(x2, o_cmp, o_sel, o_win, z, yb, yc, w_br_a, w_br_b, w_br_c, w_mix,
                          norm_cross[l], w_xq, xattn_q_norm[l], xattn_k_norm[l], mkv, w_xo, l, s, 512)
        x2 = _ffn(x2, norm_ffn[l], w_up, ffn_conv, w_down, l, s, 512, 1408)
    return x2.reshape(b, s, d)
```

```python
import functools
import math

import numpy as np
import jax
import jax.numpy as jnp
from jax import lax
from jax.experimental import pallas as pl
from jax.experimental.pallas import tpu as pltpu

F32 = jnp.float32
BF16 = jnp.bfloat16

EPS = 1e-6
ROPE_THETA = 10000.0
NEG_BIG = -1e30
TINY = 1e-20
FORCE_SCORE = 1e6
SEL_MASK = 2.0 ** 30

D_MODEL = 1024
N_MEM = 256
HEAD_DIM = 64
NSA_HEADS = 8
NSA_GROUPS = 2
NSA_HPG = 4
CMP_BLOCK = 32
CMP_STRIDE = 16
CMP_HIDDEN = 128
SEL_BLOCK = 64
SEL_TOPK = 16
WINDOW = 512
N_HEADS4 = 4
CHUNK = 64
SUB = 16
XATTN_HEAD_DIM = 128
D_FF = 2816

COL_M = 0
COL_NQ = 3072
COL_KV = 3584
COL_NG = 4352
COL_BA = 4480
COL_HG = 4608
COL_GD = 5376
COL_HF = 6400
IN_WIDTH_P = 6912

VMEM_LIMIT = 48 * 1024 * 1024


def _cp(*sem):
    return pltpu.CompilerParams(dimension_semantics=sem, vmem_limit_bytes=VMEM_LIMIT)


def _dot(a, b):
    return jnp.dot(a.astype(BF16), b.astype(BF16), preferred_element_type=F32)


def _dot_nt(a, b):
    return lax.dot_general(a.astype(BF16), b.astype(BF16), (((1,), (1,)), ((), ())),
                           preferred_element_type=F32)


def _split(a, n):
    parts = []
    r = a
    for _ in range(n):
        p = r.astype(BF16)
        parts.append(p)
        r = r - p.astype(F32)
    return parts


def _rows_dot(blocks, rhs):
    if len(blocks) == 1:
        return [jnp.dot(blocks[0], rhs, preferred_element_type=F32)]
    r = jnp.dot(jnp.concatenate(blocks, axis=0), rhs, preferred_element_type=F32)
    out, o = [], 0
    for blk in blocks:
        out.append(r[o:o + blk.shape[0]])
        o += blk.shape[0]
    return out


def _dot_ls(a, b_exact, n=2):
    parts = _rows_dot(_split(a, n), b_exact)
    acc = parts[0]
    for p in parts[1:]:
        acc = acc + p
    return acc


def _chunk_cumsum(x):
    row = lax.broadcasted_iota(jnp.int32, (x.shape[0], 1), 0) % CHUNK
    sh = 1
    while sh < CHUNK:
        x = x + jnp.where(row >= sh, pltpu.roll(x, sh, 0), 0.0)
        sh *= 2
    return x


def _as_column(row):
    return jnp.broadcast_to(row, (8, row.shape[1])).T[:, 0:1]


def _sigmoid(x):
    return 1.0 / (1.0 + jnp.exp(-x))


def _silu(x):
    return x * _sigmoid(x)


def _block_diag4(y, keep):
    yb = y.astype(BF16)
    return jnp.where(keep, jnp.concatenate([yb, yb, yb, yb], axis=0), jnp.zeros((), BF16))


def _bd_mask():
    r = lax.broadcasted_iota(jnp.int32, (256, 256), 0) // 64
    c = lax.broadcasted_iota(jnp.int32, (256, 256), 1) // 64
    return r == c


def _norm_matmul_kernel(x_ref, g_ref, w_ref, o_ref, h_ref, *, split_last):
    j = pl.program_id(1)

    @pl.when(j == 0)
    def _():
        x = x_ref[...]
        ms = jnp.mean(x * x, axis=-1, keepdims=True)
        h_ref[...] = (x * lax.rsqrt(ms + EPS) * g_ref[...]).astype(BF16)

    r = jnp.dot(h_ref[...], w_ref[...], preferred_element_type=F32)
    if not split_last:
        o_ref[...] = r.astype(o_ref.dtype)
    else:
        last = pl.num_programs(1) - 1
        n, k = r.shape[1], split_last

        @pl.when(j != last)
        def _():
            o_ref[...] = r.astype(o_ref.dtype)

        @pl.when(j == last)
        def _():
            o_ref[:, 0:n - 2 * k] = r[:, 0:n - 2 * k].astype(o_ref.dtype)
            lead = r[:, n - 2 * k:n - k].astype(BF16)
            o_ref[:, n - 2 * k:n - k] = lead
            o_ref[:, n - k:n] = (r[:, n - 2 * k:n - k] - lead.astype(F32)).astype(BF16)


def _norm_matmul(x, g, w, layer, tm, tn, out_dtype=F32, split_last=0):
    m, d = x.shape
    n = w.shape[2]
    return pl.pallas_call(
        functools.partial(_norm_matmul_kernel, split_last=split_last),
        grid=(m // tm, n // tn),
        in_specs=[pl.BlockSpec((tm, d), lambda i, j: (i, 0)),
                  pl.BlockSpec((1, d), lambda i, j: (0, 0)),
                  pl.BlockSpec((None, d, tn), lambda i, j: (layer, 0, j))],
        out_specs=pl.BlockSpec((tm, tn), lambda i, j: (i, j)),
        out_shape=jax.ShapeDtypeStruct((m, n), out_dtype),
        scratch_shapes=[pltpu.VMEM((tm, d), BF16)],
        compiler_params=_cp("parallel", "arbitrary"),
        name="norm_matmul",
    )(x, g.reshape(1, d), w)


def _head_norm(x, bd, g):
    ms = _dot_ls(x * x, bd, 2)
    return x * lax.rsqrt(ms + EPS) * g


def _rope(x, c, s):
    w = x.shape[1]
    lane = lax.broadcasted_iota(jnp.int32, x.shape, 1)
    sw = jnp.where((lane & 32) != 0, pltpu.roll(x, 32, 1), pltpu.roll(x, w - 32, 1))
    return x * c + sw * s


def _nsa_prep_kernel(q_ref, kvc_ref, kvs_ref, kvw_ref, cos_ref, sin_ref, qn_ref, kn_ref, bd512_ref, bd128_ref,
                     qo_ref, kc_ref, vc_ref, kst_ref, kwt_ref, vs_ref, vw_ref, tok_ref):
    c = cos_ref[...]
    s = sin_ref[...]
    c4 = jnp.concatenate([c, c, c, c], axis=1)
    s4 = jnp.concatenate([s, s, s, s], axis=1)
    q = _rope(_head_norm(q_ref[...].astype(F32), bd512_ref[...], qn_ref[...]), c4, s4)
    qo_ref[...] = (q * (HEAD_DIM ** -0.5)).astype(BF16)

    def key(ref, row):
        return _rope(_head_norm(ref[:, 0:128].astype(F32), bd128_ref[...], kn_ref[row:row + 1, :]), c, s)

    tok_ref[0] = key(kvc_ref, 0)
    tok_ref[1] = kvc_ref[:, 128:256].astype(F32)
    n16 = tok_ref.shape[1] // CMP_STRIDE
    for r in range(CMP_STRIDE):
        kc_ref[:, r * 128:(r + 1) * 128] = tok_ref[0, pl.ds(r, n16, stride=CMP_STRIDE), :]
        vc_ref[:, r * 128:(r + 1) * 128] = tok_ref[1, pl.ds(r, n16, stride=CMP_STRIDE), :]
    kst_ref[...] = key(kvs_ref, 1).T.astype(BF16)
    kwt_ref[...] = key(kvw_ref, 2).T.astype(BF16)
    vs = kvs_ref[:, 128:256].astype(BF16)
    vw = kvw_ref[:, 128:256].astype(BF16)
    ones = jnp.ones((vs.shape[0], 64), BF16)
    for g in range(NSA_GROUPS):
        vs_ref[g] = jnp.concatenate([vs[:, g * 64:(g + 1) * 64], ones], axis=1)
        vw_ref[g] = jnp.concatenate([vw[:, g * 64:(g + 1) * 64], ones], axis=1)


def _nsa_prep(z, cos_t, sin_t, q_norm, k_norm, b, s, tm):
    m = b * s
    nt = s // tm
    bd512 = jnp.asarray(np.kron(np.eye(8), np.full((64, 64), 1.0 / 64)), BF16)
    bd128 = jnp.asarray(np.kron(np.eye(2), np.full((64, 64), 1.0 / 64)), BF16)
    qn = jnp.tile(q_norm, 8).reshape(1, 512)
    kn = jnp.tile(k_norm, (1, 2))
    row = lambda i: (i, 0)
    const = lambda i: (0, 0)
    return pl.pallas_call(
        _nsa_prep_kernel,
        grid=(m // tm,),
        in_specs=[pl.BlockSpec((tm, 512), lambda i: (i, COL_NQ // 512)),
                  pl.BlockSpec((tm, 256), lambda i: (i, COL_KV // 256)),
                  pl.BlockSpec((tm, 256), lambda i: (i, COL_KV // 256 + 1)),
                  pl.BlockSpec((tm, 256), lambda i: (i, COL_KV // 256 + 2)),
                  pl.BlockSpec((tm, 128), lambda i: (i % nt, 0)),
                  pl.BlockSpec((tm, 128), lambda i: (i % nt, 0)),
                  pl.BlockSpec((1, 512), const),
                  pl.BlockSpec((3, 128), const),
                  pl.BlockSpec((512, 512), const),
                  pl.BlockSpec((128, 128), const)],
        out_specs=[pl.BlockSpec((tm, 512), row),
                   pl.BlockSpec((None, tm // CMP_STRIDE, CMP_STRIDE * 128), lambda i: (i // nt, i % nt, 0)),
                   pl.BlockSpec((None, tm // CMP_STRIDE, CMP_STRIDE * 128), lambda i: (i // nt, i % nt, 0)),
                   pl.BlockSpec((None, 128, tm), lambda i: (i // nt, 0, i % nt)),
                   pl.BlockSpec((None, 128, tm), lambda i: (i // nt, 0, i % nt)),
                   pl.BlockSpec((NSA_GROUPS, tm, 128), lambda i: (0, i, 0)),
                   pl.BlockSpec((NSA_GROUPS, tm, 128), lambda i: (0, i, 0))],
        out_shape=[jax.ShapeDtypeStruct((m, 512), BF16),
                   jax.ShapeDtypeStruct((b, s // CMP_STRIDE, CMP_STRIDE * 128), F32),
                   jax.ShapeDtypeStruct((b, s // CMP_STRIDE, CMP_STRIDE * 128), F32),
                   jax.ShapeDtypeStruct((b, 128, s), BF16),
                   jax.ShapeDtypeStruct((b, 128, s), BF16),
                   jax.ShapeDtypeStruct((NSA_GROUPS, m, 128), BF16),
                   jax.ShapeDtypeStruct((NSA_GROUPS, m, 128), BF16)],
        scratch_shapes=[pltpu.VMEM((2, tm, 128), F32)],
        compiler_params=_cp("parallel"),
        name="nsa_prep",
    )(z, z, z, z, cos_t, sin_t, qn, kn, bd512, bd128)


def _gelu_tanh(x):
    return 0.5 * x * (1.0 + jnp.tanh(math.sqrt(2.0 / math.pi) * (x + 0.044715 * (x * x * x))))


def _compress_kernel(rk_ref, rv_ref, pk_ref, pv_ref, wkt_ref, wkb_ref, wvt_ref, wvb_ref, wk2_ref, wv2_ref,
                     kct_ref, vc_ref):
    def mlp(r_ref, p_ref, wt_ref, wb_ref, w2_ref):
        r = r_ref[...]
        n = r.shape[0]
        top = _dot(r + p_ref[0:1, :], wt_ref[...])
        bot = _dot(r + p_ref[1:2, :], wb_ref[...])
        hid = top + pltpu.roll(bot, n - 1, 0)
        return _dot(_gelu_tanh(hid), w2_ref[...])

    kc = mlp(rk_ref, pk_ref, wkt_ref, wkb_ref, wk2_ref)
    vc = mlp(rv_ref, pv_ref, wvt_ref, wvb_ref, wv2_ref)
    kct_ref[...] = kc.T
    for g in range(NSA_GROUPS):
        vc_ref[g] = vc[:, g * 64:(g + 1) * 64]


def _compress_weights(pos, w1, w2):
    w1r = w1.reshape(2, 16, 64, CMP_HIDDEN)
    zero = jnp.zeros_like(w1r)
    big = jnp.stack([jnp.stack([w1r, zero], axis=-2), jnp.stack([zero, w1r], axis=-2)], axis=2)
    big = big.reshape(2, 16 * 2 * 64, 2 * CMP_HIDDEN).astype(BF16)
    posr = jnp.broadcast_to(pos.reshape(2, 16, 1, 64), (2, 16, 2, 64)).reshape(2, 2048)
    w2bd = jnp.zeros((2, CMP_HIDDEN, 2, 64), F32)
    w2bd = w2bd.at[0, :, 0, :].set(w2).at[1, :, 1, :].set(w2).reshape(2 * CMP_HIDDEN, 128).astype(BF16)
    return posr, big[0], big[1], w2bd


def _compress(rk, rv, pos_k, pos_v, ck_w1, ck_w2, cv_w1, cv_w2):
    b, n, _ = rk.shape
    pk, wkt, wkb, wk2 = _compress_weights(pos_k, ck_w1, ck_w2)
    pv, wvt, wvb, wv2 = _compress_weights(pos_v, cv_w1, cv_w2)
    c2 = lambda i: (0, 0)
    return pl.pallas_call(
        _compress_kernel,
        grid=(b,),
        in_specs=[pl.BlockSpec((None, n, 2048), lambda i: (i, 0, 0)),
                  pl.BlockSpec((None, n, 2048), lambda i: (i, 0, 0)),
                  pl.BlockSpec((2, 2048), c2), pl.BlockSpec((2, 2048), c2),
                  pl.BlockSpec((2048, 256), c2), pl.BlockSpec((2048, 256), c2),
                  pl.BlockSpec((2048, 256), c2), pl.BlockSpec((2048, 256), c2),
                  pl.BlockSpec((256, 128), c2), pl.BlockSpec((256, 128), c2)],
        out_specs=[pl.BlockSpec((None, 128, n), lambda i: (i, 0, 0)),
                   pl.BlockSpec((None, NSA_GROUPS, n, 64), lambda i: (i, 0, 0, 0))],
        out_shape=[jax.ShapeDtypeStruct((b, 128, n), F32),
                   jax.ShapeDtypeStruct((b, NSA_GROUPS, n, 64), F32)],
        compiler_params=_cp("parallel"),
        name="nsa_compress",
    )(rk, rv, pk, pv, wkt, wkb, wvt, wvb, wk2, wv2)


def _cmp_attn_kernel(q_ref, kct_ref, vc_ref, ovt_ref, o_ref, sel_ref, imp_ref, *, tq, ncp, nsel):
    i = pl.program_id(1)
    tpos = i * tq + lax.broadcasted_iota(jnp.int32, (tq, 1), 0)
    nblk = lax.broadcasted_iota(jnp.int32, (1, ncp), 1)
    bias = jnp.where((nblk * CMP_STRIDE + (CMP_BLOCK - 1)) <= tpos, 0.0, NEG_BIG)
    any_valid = jnp.where(tpos >= CMP_BLOCK - 1, 1.0, 0.0)
    j = lax.broadcasted_iota(jnp.int32, (nsel, 1), 0)
    cur = (i * tq + lax.broadcasted_iota(jnp.int32, (1, tq), 1)) // SEL_BLOCK
    forced = (j == 0) | (j == cur) | (j == cur - 1)
    jl = lax.broadcasted_iota(jnp.int32, (8, 1), 0)
    ovt = ovt_ref[...]

    scores = []
    for g in range(NSA_GROUPS):
        q4 = jnp.concatenate([q_ref[:, (g * NSA_HPG + h) * 64:(g * NSA_HPG + h + 1) * 64]
                              for h in range(NSA_HPG)], axis=0)
        scores.append(jnp.dot(q4, kct_ref[g * 64:(g + 1) * 64, :].astype(BF16), preferred_element_type=F32))
    psums = []
    for g, s in enumerate(scores):
        s = s.reshape(NSA_HPG, tq, ncp) + bias[None]
        e = jnp.exp(s - jnp.max(s, axis=-1, keepdims=True))
        p = e * (any_valid / jnp.maximum(jnp.sum(e, axis=-1, keepdims=True), 1e-30))
        o = jnp.dot(p.reshape(NSA_HPG * tq, ncp).astype(BF16), vc_ref[g].astype(BF16), preferred_element_type=F32)
        for h in range(NSA_HPG):
            c = (g * NSA_HPG + h) * 64
            o_ref[:, c:c + 64] = o[h * tq:(h + 1) * tq].astype(o_ref.dtype)
        psums.append(p[0] + p[1] + p[2] + p[3])
    for g, psum in enumerate(psums):
        ph, plo = _split(psum, 2)
        imp = _dot_nt(ovt, ph) + _dot_nt(ovt, plo)
        imp_ref[g] = jnp.where(j <= cur, jnp.where(forced, FORCE_SCORE, imp), -1.0)
    for g in range(NSA_GROUPS):
        val = imp_ref[g]
        rows8 = [val[r8 * 8:(r8 + 1) * 8] for r8 in range(nsel // 8)]
        cnt = [jnp.zeros((8, tq), F32) for _ in rows8]
        for r in range(nsel):
            vr = imp_ref[g, r:r + 1, :]
            for r8, vg in enumerate(rows8):
                if r8 * 8 > r:
                    beats = vr >= vg
                elif r8 * 8 + 7 < r:
                    beats = vr > vg
                else:
                    beats = (vr > vg) | ((vr == vg) & (jl + r8 * 8 > r))
                cnt[r8] = cnt[r8] + jnp.where(beats, 1.0, 0.0)
        sel = jnp.where(jnp.concatenate(cnt, axis=0) < float(min(SEL_TOPK, nsel)), 1.0, 0.0)
        sel_ref[g] = sel.T.astype(BF16)


def _cmp_attn(q_r, kct, vc, b, s, tq):
    m = b * s
    nq = s // tq
    ncp = kct.shape[2]
    nsel = s // SEL_BLOCK
    n_cmp = (s - CMP_BLOCK) // CMP_STRIDE + 1
    tok = np.arange(n_cmp)[:, None] * CMP_STRIDE + np.arange(CMP_BLOCK)[None, :]
    ov = np.zeros((ncp, nsel), np.float32)
    for l in range(CMP_BLOCK):
        ov[np.arange(n_cmp), tok[:, l] // SEL_BLOCK] += 1.0
    ovt = jnp.asarray(ov.T, BF16)
    kern = functools.partial(_cmp_attn_kernel, tq=tq, ncp=ncp, nsel=nsel)
    return pl.pallas_call(
        kern,
        grid=(b, nq),
        in_specs=[pl.BlockSpec((tq, 512), lambda bb, i: (bb * nq + i, 0)),
                  pl.BlockSpec((None, 128, ncp), lambda bb, i: (bb, 0, 0)),
                  pl.BlockSpec((None, NSA_GROUPS, ncp, 64), lambda bb, i: (bb, 0, 0, 0)),
                  pl.BlockSpec((nsel, ncp), lambda bb, i: (0, 0))],
        out_specs=[pl.BlockSpec((tq, 512), lambda bb, i: (bb * nq + i, 0)),
                   pl.BlockSpec((NSA_GROUPS, tq, nsel), lambda bb, i: (0, bb * nq + i, 0))],
        out_shape=[jax.ShapeDtypeStruct((m, 512), BF16),
                   jax.ShapeDtypeStruct((NSA_GROUPS, m, nsel), BF16)],
        scratch_shapes=[pltpu.VMEM((NSA_GROUPS, nsel, tq), F32)],
        compiler_params=_cp("parallel", "parallel"),
        name="nsa_cmp_attn",
    )(q_r, kct, vc, ovt)


def _sel_attn_kernel(qi_ref, kt_ref, first_ref, last_ref, q_ref, k_ref, v_ref, sel_ref, o_ref, q4_ref, m_ref, acc_ref,
                     *, tq, tk, nsel):
    n = pl.program_id(1)
    qi = qi_ref[n]
    ktile = kt_ref[n]
    rows = NSA_HPG * tq
    groups = [slice(g * rows, (g + 1) * rows) for g in range(NSA_GROUPS)]

    @pl.when(first_ref[n] == 1)
    def _():
        for g in range(NSA_GROUPS):
            unsel = ((sel_ref[g].astype(F32) - 1.0) * SEL_MASK).astype(BF16)
            for h in range(NSA_HPG):
                r0 = g * rows + h * tq
                q4_ref[r0:r0 + tq, 0:64] = q_ref[:, (g * NSA_HPG + h) * 64:(g * NSA_HPG + h + 1) * 64]
                q4_ref[r0:r0 + tq, 64:64 + nsel] = unsel
        m_ref[...] = jnp.full(m_ref.shape, NEG_BIG, F32)
        acc_ref[...] = jnp.zeros(acc_ref.shape, F32)

    def step(with_causal):
        kpos = ktile * tk + lax.broadcasted_iota(jnp.int32, (1, tk), 1)
        blk = lax.broadcasted_iota(jnp.int32, (nsel, 1), 0)
        own_block = jnp.where(blk == kpos // SEL_BLOCK, 1.0, 0.0).astype(BF16)
        scores = [jnp.dot(q4_ref[rs, :], jnp.concatenate([k_ref[g * 64:(g + 1) * 64, :], own_block], axis=0),
                          preferred_element_type=F32) for g, rs in enumerate(groups)]
        if with_causal:
            qpos = qi * tq + lax.broadcasted_iota(jnp.int32, (tq, 1), 0)
            causal = jnp.where(kpos <= qpos, 0.0, NEG_BIG)
        for g, (rs, s) in enumerate(zip(groups, scores)):
            if with_causal:
                s = (s.reshape(NSA_HPG, tq, tk) + causal[None]).reshape(rows, tk)
            m_prev = m_ref[rs, :]
            m_next = jnp.maximum(m_prev, jnp.max(s, axis=-1, keepdims=True))
            p = jnp.exp(s - jnp.tile(m_next, (1, tk // 128)))
            pv = jnp.dot(p.astype(BF16), v_ref[g], preferred_element_type=F32)
            acc_ref[rs, :] = jnp.exp(m_prev - m_next) * acc_ref[rs, :] + pv
            m_ref[rs, :] = m_next

    reaches_past = ktile * tk + tk - 1 > qi * tq
    pl.when(reaches_past)(functools.partial(step, True))
    pl.when(jnp.logical_not(reaches_past))(functools.partial(step, False))

    @pl.when(last_ref[n] == 1)
    def _():
        a = acc_ref[...]
        o = a / pltpu.roll(a, 64, 1)
        for hh in range(NSA_HEADS):
            o_ref[:, hh * 64:(hh + 1) * 64] = o[hh * tq:(hh + 1) * tq, 0:64].astype(o_ref.dtype)


def _sel_attn(q_r, kt, v, sel, b, s, tq, tk):
    m = b * s
    nq = s // tq
    nkb = s // tk
    nsel = s // SEL_BLOCK
    pairs = []
    for qi in range(nq):
        kts = [k for k in range(nkb) if k * tk <= qi * tq + tq - 1]
        pairs += [(qi, k, int(j == 0), int(j == len(kts) - 1)) for j, k in enumerate(kts)]
    tabs = [jnp.asarray(np.array([pr[c] for pr in pairs], np.int32)) for c in range(4)]
    kern = functools.partial(_sel_attn_kernel, tq=tq, tk=tk, nsel=nsel)
    return pl.pallas_call(
        kern,
        grid_spec=pltpu.PrefetchScalarGridSpec(
            num_scalar_prefetch=4,
            grid=(b, len(pairs)),
            in_specs=[pl.BlockSpec((tq, 512), lambda bb, n, qt, kt_, f, l: (bb * nq + qt[n], 0)),
                      pl.BlockSpec((None, 128, tk), lambda bb, n, qt, kt_, f, l: (bb, 0, kt_[n])),
                      pl.BlockSpec((NSA_GROUPS, tk, 128), lambda bb, n, qt, kt_, f, l: (0, bb * nkb + kt_[n], 0)),
                      pl.BlockSpec((NSA_GROUPS, tq, nsel), lambda bb, n, qt, kt_, f, l: (0, bb * nq + qt[n], 0))],
            out_specs=pl.BlockSpec((tq, 512), lambda bb, n, qt, kt_, f, l: (bb * nq + qt[n], 0)),
            scratch_shapes=[pltpu.VMEM((NSA_HEADS * tq, 64 + nsel), BF16), pltpu.VMEM((NSA_HEADS * tq, 128), F32),
                            pltpu.VMEM((NSA_HEADS * tq, 128), F32)]),
        out_shape=jax.ShapeDtypeStruct((m, 512), BF16),
        compiler_params=_cp("parallel", "arbitrary"),
        name="nsa_sel_attn",
    )(*tabs, q_r, kt, v, sel)


def _win_attn_kernel(q_ref, *refs, tq, nblk):
    k_refs, v_refs, o_ref = refs[:nblk], refs[nblk:2 * nblk], refs[2 * nblk]
    qi = pl.program_id(1)
    kpos = (qi - (nblk - 1)) * tq + lax.broadcasted_iota(jnp.int32, (1, nblk * tq), 1)
    dist = qi * tq + lax.broadcasted_iota(jnp.int32, (tq, 1), 0) - kpos
    bias = jnp.where((dist >= 0) & (dist < WINDOW) & (kpos >= 0), 0.0, NEG_BIG)
    scores = []
    for g in range(NSA_GROUPS):
        q4 = jnp.concatenate([q_ref[:, (g * NSA_HPG + h) * 64:(g * NSA_HPG + h + 1) * 64]
                              for h in range(NSA_HPG)], axis=0)
        kt = jnp.concatenate([r[g * 64:(g + 1) * 64, :] for r in k_refs], axis=1)
        scores.append(jnp.dot(q4, kt, preferred_element_type=F32))
    for g, s in enumerate(scores):
        s = (s.reshape(NSA_HPG, tq, nblk * tq) + bias[None]).reshape(NSA_HPG * tq, nblk * tq)
        p = jnp.exp(s - jnp.max(s, axis=-1, keepdims=True))
        v = jnp.concatenate([r[g] for r in v_refs], axis=0)
        pv = jnp.dot(p.astype(BF16), v, preferred_element_type=F32)
        o = pv / pltpu.roll(pv, 64, 1)
        for h in range(NSA_HPG):
            c = (g * NSA_HPG + h) * 64
            o_ref[:, c:c + 64] = o[h * tq:(h + 1) * tq, 0:64].astype(o_ref.dtype)


def _win_attn(q_r, kt, v, b, s, tq):
    m = b * s
    nq = s // tq
    nblk = WINDOW // tq + 1
    kidx = lambda qi, j: jnp.maximum(qi - (nblk - 1) + j, 0)
    k_specs = [pl.BlockSpec((None, 128, tq), functools.partial(lambda bb, qi, j: (bb, 0, kidx(qi, j)), j=j))
               for j in range(nblk)]
    v_specs = [pl.BlockSpec((NSA_GROUPS, tq, 128), functools.partial(lambda bb, qi, j: (0, bb * nq + kidx(qi, j), 0), j=j))
               for j in range(nblk)]
    kern = functools.partial(_win_attn_kernel, tq=tq, nblk=nblk)
    return pl.pallas_call(
        kern,
        grid=(b, nq),
        in_specs=[pl.BlockSpec((tq, 512), lambda bb, qi: (bb * nq + qi, 0))] + k_specs + v_specs,
        out_specs=pl.BlockSpec((tq, 512), lambda bb, qi: (bb * nq + qi, 0)),
        out_shape=jax.ShapeDtypeStruct((m, 512), BF16),
        compiler_params=_cp("parallel", "parallel"),
        name="nsa_win_attn",
    )(q_r, *([kt] * nblk), *([v] * nblk))


def _consts_recurrent():
    bd_mean = np.kron(np.eye(4), np.full((64, 64), 1.0 / 64))
    bd_ones = np.kron(np.eye(4), np.ones((64, 64)))
    return jnp.asarray(bd_mean, BF16), jnp.asarray(bd_ones, BF16)


def _group_rmsnorm(o, bd_mean, g):
    ms = _dot_ls(o * o, bd_mean, 2)
    return o * lax.rsqrt(ms + EPS) * g


def _hgrn_kernel(q_ref, i_ref, g_ref, fh_ref, fl_ref, lbl_ref, on_ref, bdm_ref, bdo_ref, selp_ref,
                 o_ref, st_ref, z_ref, *, layer, depth, t):
    @pl.when(pl.program_id(1) == 0)
    def _():
        st_ref[...] = jnp.zeros(st_ref.shape, F32)

    lg = lbl_ref[...]
    mx = jnp.max(lg, axis=0, keepdims=True)
    ex = jnp.exp(lg - mx)
    pr = ex / jnp.sum(ex, axis=0, keepdims=True)
    cs = pr[0:1, :]
    for r in range(1, layer + 1):
        cs = cs + pr[r:r + 1, :]
    lb = cs - pr[0:1, :]

    q = _silu(q_ref[...].astype(F32))
    fz = fh_ref[...].astype(F32) + fl_ref[...].astype(F32)
    f = lb + (1.0 - lb) * _sigmoid(fz)
    logf = jnp.log(jnp.maximum(f, TINY))
    k = (1.0 - lb) * _sigmoid(-fz)
    v = i_ref[...].astype(F32)
    b = _chunk_cumsum(logf)

    bdm = _bd_mask()
    bdk = bdo_ref[...] > 0
    nsub = t // SUB
    tl = lax.broadcasted_iota(jnp.int32, (1, SUB, 1), 1)
    for hp in range(2):
        ls = slice(hp * 128, (hp + 1) * 128)
        q3 = q[:, ls].reshape(nsub, SUB, 128)
        k3 = k[:, ls].reshape(nsub, SUB, 128)
        b3 = b[:, ls].reshape(nsub, SUB, 128)
        for sl in range(SUB):
            e = jnp.exp(jnp.where(tl >= sl, b3 - b3[:, sl:sl + 1, :], NEG_BIG))
            zz = q3 * k3[:, sl:sl + 1, :] * e
            z_ref[hp * t:(hp + 1) * t, sl * 128:(sl + 1) * 128] = zz.reshape(t, 128)
    a_pairs = _dot_ls(z_ref[...], selp_ref[...], 2)
    a_all = jnp.concatenate([a_pairs[0:t], a_pairs[t:2 * t]], axis=1)
    trow = lax.broadcasted_iota(jnp.int32, (CHUNK, 256), 0)
    scol = lax.broadcasted_iota(jnp.int32, (CHUNK, 256), 1) % CHUNK
    diag_mask = (trow // SUB) == (scol // SUB)
    m1 = (trow >= 32) & (scol < 32)
    m2 = ((trow >= 16) & (trow < 32) & (scol < 16)) | ((trow >= 48) & (scol >= 32) & (scol < 48))
    rowi = lax.broadcasted_iota(jnp.int32, (CHUNK, 1), 0)

    chunks = [slice(c * CHUNK, (c + 1) * CHUNK) for c in range(t // CHUNK)]
    lvl1, lvl2 = [], []
    for rs in chunks:
        qc, kc, bc = q[rs], k[rs], b[rs]
        r1 = bc[31:32, :]
        q1 = qc * jnp.exp(jnp.minimum(bc - r1, 0.0))
        k1 = jnp.where(rowi < 32, kc * jnp.exp(jnp.minimum(r1 - bc, 0.0)), 0.0)
        r2 = jnp.where(rowi < 32, bc[15:16, :], bc[47:48, :])
        q2 = qc * jnp.exp(jnp.minimum(bc - r2, 0.0))
        k2 = kc * jnp.exp(jnp.minimum(r2 - bc, 0.0))
        lvl1.append(_dot_nt(q1, _block_diag4(k1, bdk)))
        lvl2.append(_dot_nt(q2, _block_diag4(k2, bdk)))
    o_intra, s_add, s_dec, q_dec = [], [], [], []
    for rs, a1, a2 in zip(chunks, lvl1, lvl2):
        kc, vc, bc = k[rs], v[rs], b[rs]
        attn = jnp.where(diag_mask, a_all[rs], 0.0) + jnp.where(m1, a1, 0.0) + jnp.where(m2, a2, 0.0)
        o_intra.append(_dot(attn, _block_diag4(vc, bdk)))
        bl = bc[CHUNK - 1:CHUNK, :]
        kdt = (kc * jnp.exp(bl - bc)).T
        s_add.append(jnp.where(bdm, _dot(kdt, vc), 0.0))
        s_dec.append(_as_column(jnp.exp(bl)))
        q_dec.append((q[rs] * jnp.exp(bc)).astype(BF16))

    st = st_ref[...]
    outs = []
    for oi, sa, sd, qd in zip(o_intra, s_add, s_dec, q_dec):
        outs.append(oi + jnp.dot(qd, st.astype(BF16), preferred_element_type=F32))
        st = sd * st + sa
    st_ref[...] = st
    o = jnp.concatenate(outs, axis=0)
    o_ref[...] = (_group_rmsnorm(o, bdm_ref[...], on_ref[...]) * _sigmoid(g_ref[...].astype(F32))).astype(BF16)


def _hgrn(z, lb_logits, out_norm, layer, b, s, t):
    m = b * s
    nt = s // t
    depth = lb_logits.shape[0]
    bd_mean, bd_ones = _consts_recurrent()
    selp = np.zeros((SUB, 2, 64, 2, CHUNK), np.float32)
    for sl in range(SUB):
        for h2 in range(2):
            selp[sl, h2, :, h2, sl::SUB] = 1.0
    selp = jnp.asarray(selp.reshape(SUB * 128, 128), BF16)
    col, colf = COL_HG // 256, COL_HF // 256
    c2 = lambda bb, i: (0, 0)
    kern = functools.partial(_hgrn_kernel, layer=layer, depth=depth, t=t)
    return pl.pallas_call(
        kern,
        grid=(b, nt),
        in_specs=[pl.BlockSpec((t, 256), lambda bb, i: (bb * nt + i, col)),
                  pl.BlockSpec((t, 256), lambda bb, i: (bb * nt + i, col + 1)),
                  pl.BlockSpec((t, 256), lambda bb, i: (bb * nt + i, col + 2)),
                  pl.BlockSpec((t, 256), lambda bb, i: (bb * nt + i, colf)),
                  pl.BlockSpec((t, 256), lambda bb, i: (bb * nt + i, colf + 1)),
                  pl.BlockSpec((depth, 256), c2),
                  pl.BlockSpec((1, 256), c2),
                  pl.BlockSpec((256, 256), c2),
                  pl.BlockSpec((256, 256), c2),
                  pl.BlockSpec((SUB * 128, 128), c2)],
        out_specs=pl.BlockSpec((t, 256), lambda bb, i: (bb * nt + i, 0)),
        out_shape=jax.ShapeDtypeStruct((m, 256), BF16),
        scratch_shapes=[pltpu.VMEM((256, 256), F32), pltpu.VMEM((2 * t, SUB * 128), F32)],
        compiler_params=_cp("parallel", "arbitrary"),
        name="hgrn2",
    )(z, z, z, z, z, lb_logits, jnp.tile(out_norm, 4).reshape(1, 256), bd_mean, bd_ones, selp)


def _gdn_kernel(q_ref, k_ref, v_ref, z_ref, ba_ref, cw_ref, al_ref, dt_ref, on_ref, bdm_ref, bdo_ref,
                eb_ref, ea_ref, o_ref, st_ref, prev_ref, *, t):
    @pl.when(pl.program_id(1) == 0)
    def _():
        st_ref[...] = jnp.zeros(st_ref.shape, F32)
        prev_ref[...] = jnp.zeros(prev_ref.shape, F32)

    def conv_silu(x_ref, p):
        x = x_ref[...].astype(F32)
        xp = jnp.concatenate([prev_ref[:, p * 256:(p + 1) * 256], x], axis=0)
        w = cw_ref[:, p * 256:(p + 1) * 256]
        y = w[3:4, :] * xp[8:, :]
        for j in range(1, 4):
            y = y + w[3 - j:4 - j, :] * pltpu.roll(xp, j, 0)[8:, :]
        return _silu(y), x[t - 8:, :]

    qa, qtail = conv_silu(q_ref, 0)
    ka, ktail = conv_silu(k_ref, 1)
    va, vtail = conv_silu(v_ref, 2)
    prev_ref[:, 0:256] = qtail
    prev_ref[:, 256:512] = ktail
    prev_ref[:, 512:768] = vtail

    bdo = bdo_ref[...]
    bdk = bdo > 0
    ss = _dot_ls(jnp.concatenate([qa * qa, ka * ka], axis=0), bdo, 2)
    q = qa * lax.rsqrt(ss[0:t] + EPS) * (HEAD_DIM ** -0.5)
    k = ka * lax.rsqrt(ss[t:2 * t] + EPS)
    v = va

    gl = ba_ref[...].astype(F32)
    beta = _dot_ls(_sigmoid(gl), eb_ref[...], 2)
    xs = gl + dt_ref[...]
    softplus = jnp.maximum(xs, 0.0) + jnp.log(1.0 + jnp.exp(-jnp.abs(xs)))
    b = _dot_ls(_chunk_cumsum(-jnp.exp(al_ref[...]) * softplus), ea_ref[...], 3)

    bdm = _bd_mask()
    trow = lax.broadcasted_iota(jnp.int32, (CHUNK, 256), 0)
    scol = lax.broadcasted_iota(jnp.int32, (CHUNK, 256), 1) % CHUNK
    incl = scol <= trow
    strict = scol < trow
    eye = jnp.where(scol == trow, 1.0, 0.0)

    chunks = [slice(c * CHUNK, (c + 1) * CHUNK) for c in range(t // CHUNK)]
    nch = len(chunks)
    aw, aq = [], []
    for rs in chunks:
        qc, kc, bc = q[rs], k[rs], b[rs]
        brow = jnp.concatenate([bc[:, h * 64:(h + 1) * 64].T for h in range(N_HEADS4)], axis=1)
        e = jnp.exp(jnp.where(incl, bc - brow, 0.0))
        gram = _dot_nt(jnp.concatenate([kc * beta[rs], qc], axis=0), _block_diag4(kc, bdk))
        aw.append(jnp.where(strict, gram[:CHUNK] * e, 0.0))
        aq.append(jnp.where(incl, gram[CHUNK:] * e, 0.0).astype(BF16))
    pw = [None] * nch
    tinv = [None] * nch
    for c in range(nch):
        bh, bl_ = _split(-aw[c], 2)
        r1 = _rows_dot([bh, bl_], _block_diag4(bh, bdk))
        pw[c] = r1[0] + r1[1] + jnp.dot(bh, _block_diag4(bl_, bdk), preferred_element_type=F32)
        tinv[c] = eye - aw[c]
    for c in range(nch):
        ph, pl_ = _split(pw[c], 2)
        th, tl_ = _split(tinv[c], 2)
        r1 = _rows_dot([ph, pl_, th, tl_], _block_diag4(ph, bdk))
        r2 = _rows_dot([ph, th], _block_diag4(pl_, bdk))
        pw[c] = r1[0] + r1[1] + r2[0]
        tinv[c] = tinv[c] + r1[2] + r1[3] + r2[1]
    for _ in range(3):
        for c in range(nch):
            ph = pw[c].astype(BF16)
            r1 = _rows_dot([ph, tinv[c].astype(BF16)], _block_diag4(ph, bdk))
            pw[c] = r1[0]
            tinv[c] = tinv[c] + r1[1]
    for c in range(nch):
        tinv[c] = tinv[c] + jnp.dot(tinv[c].astype(BF16), _block_diag4(pw[c], bdk), preferred_element_type=F32)
    u, w = [], []
    for c, rs in enumerate(chunks):
        t2 = _split(tinv[c], 2)
        kbe = k[rs] * beta[rs] * jnp.exp(b[rs])
        r1 = _rows_dot(t2, _block_diag4(v[rs] * beta[rs], bdk))
        r2 = _rows_dot(t2, _block_diag4(kbe, bdk))
        u.append(r1[0] + r1[1])
        w.append(r2[0] + r2[1])
    lhs, s_add, s_dec, o0 = [], [], [], []
    for c, rs in enumerate(chunks):
        bc = b[rs]
        bl = bc[CHUNK - 1:CHUNK, :]
        kdt = (k[rs] * jnp.exp(bl - bc)).T.astype(BF16)
        rn = jnp.dot(kdt, jnp.concatenate([w[c], u[c]], axis=1).astype(BF16), preferred_element_type=F32)
        qeff = q[rs] * jnp.exp(bc) - jnp.dot(aq[c], _block_diag4(w[c], bdk), preferred_element_type=F32)
        lhs.append(jnp.concatenate([qeff.astype(BF16), jnp.where(bdm, rn[:, 0:256], 0.0).astype(BF16)], axis=0))
        s_add.append(jnp.where(bdm, rn[:, 256:512], 0.0))
        s_dec.append(_as_column(jnp.exp(bl)))
        o0.append(jnp.dot(aq[c], _block_diag4(u[c], bdk), preferred_element_type=F32))

    st = st_ref[...]
    outs = []
    for c in range(nch):
        r = jnp.dot(lhs[c], st.astype(BF16), preferred_element_type=F32)
        outs.append(o0[c] + r[0:CHUNK])
        st = s_dec[c] * st - r[CHUNK:] + s_add[c]
    st_ref[...] = st
    o = jnp.concatenate(outs, axis=0)
    o_ref[...] = (_group_rmsnorm(o, bdm_ref[...], on_ref[...]) * _silu(z_ref[...].astype(F32))).astype(BF16)


def _gdn(z, conv_w, a_log, dt_bias, out_norm, b, s, t):
    m = b * s
    nt = s // t
    bd_mean, bd_ones = _consts_recurrent()
    eb = np.zeros((128, 256), np.float32)
    ea = np.zeros((128, 256), np.float32)
    for h in range(N_HEADS4):
        eb[h, h * 64:(h + 1) * 64] = 1.0
        ea[4 + h, h * 64:(h + 1) * 64] = 1.0
    al = jnp.zeros((1, 128), F32).at[0, 4:8].set(a_log)
    dt = jnp.zeros((1, 128), F32).at[0, 4:8].set(dt_bias)
    col = COL_GD // 256
    c2 = lambda bb, i: (0, 0)
    kern = functools.partial(_gdn_kernel, t=t)
    return pl.pallas_call(
        kern,
        grid=(b, nt),
        in_specs=[pl.BlockSpec((t, 256), lambda bb, i: (bb * nt + i, col)),
                  pl.BlockSpec((t, 256), lambda bb, i: (bb * nt + i, col + 1)),
                  pl.BlockSpec((t, 256), lambda bb, i: (bb * nt + i, col + 2)),
                  pl.BlockSpec((t, 256), lambda bb, i: (bb * nt + i, col + 3)),
                  pl.BlockSpec((t, 128), lambda bb, i: (bb * nt + i, COL_BA // 128)),
                  pl.BlockSpec((4, 768), c2),
                  pl.BlockSpec((1, 128), c2), pl.BlockSpec((1, 128), c2),
                  pl.BlockSpec((1, 256), c2),
                  pl.BlockSpec((256, 256), c2), pl.BlockSpec((256, 256), c2),
                  pl.BlockSpec((128, 256), c2), pl.BlockSpec((128, 256), c2)],
        out_specs=pl.BlockSpec((t, 256), lambda bb, i: (bb * nt + i, 0)),
        out_shape=jax.ShapeDtypeStruct((m, 256), BF16),
        scratch_shapes=[pltpu.VMEM((256, 256), F32), pltpu.VMEM((8, 768), F32)],
        compiler_params=_cp("parallel", "arbitrary"),
        name="gdn",
    )(z, z, z, z, z, conv_w, al, dt, jnp.tile(out_norm, 4).reshape(1, 256), bd_mean, bd_ones,
      jnp.asarray(eb, BF16), jnp.asarray(ea, BF16))


def _merge_xattn_kernel(x_ref, oc_ref, os_ref, ow_ref, ng_ref, yb_ref, yc_ref, ma_ref, mb_ref, mc_ref,
                        ex_ref, wa_ref, wb_ref, wc_ref, wmix_ref, g_ref, wq_ref, qn_ref, kn_ref, mkv_ref, wo_ref,
                        o_ref):
    sg = _sigmoid(ng_ref[...].astype(F32))
    ya = (_dot_ls(sg, ex_ref[0], 2) * oc_ref[...].astype(F32) + _dot_ls(sg, ex_ref[1], 2) * os_ref[...].astype(F32)
          + _dot_ls(sg, ex_ref[2], 2) * ow_ref[...].astype(F32))
    merged = (_sigmoid(ma_ref[...].astype(F32)) * _dot(ya, wa_ref[...])
              + _sigmoid(mb_ref[...].astype(F32)) * jnp.dot(yb_ref[...], wb_ref[...], preferred_element_type=F32)
              + _sigmoid(mc_ref[...].astype(F32)) * jnp.dot(yc_ref[...], wc_ref[...], preferred_element_type=F32))
    x = x_ref[...] + _dot(merged, wmix_ref[...])

    h = x * lax.rsqrt(jnp.mean(x * x, axis=-1, keepdims=True) + EPS) * g_ref[...]
    q = _dot(h, wq_ref[...])
    scores = []
    for hd in range(N_HEADS4):
        ls = slice(hd * 128, (hd + 1) * 128)
        qh = q[:, ls]
        qh = qh * lax.rsqrt(jnp.mean(qh * qh, axis=-1, keepdims=True) + EPS) * qn_ref[...]
        kh = mkv_ref[:, ls]
        kh = kh * lax.rsqrt(jnp.mean(kh * kh, axis=-1, keepdims=True) + EPS) * kn_ref[...]
        scores.append(_dot_nt(qh, kh) * (XATTN_HEAD_DIM ** -0.5))
    outs = []
    for hd, s in enumerate(scores):
        e = jnp.exp(s - jnp.max(s, axis=-1, keepdims=True))
        pv = _dot(e, mkv_ref[:, 512 + hd * 128:512 + (hd + 1) * 128])
        outs.append(pv / jnp.sum(e, axis=-1, keepdims=True))
    o = jnp.concatenate(outs, axis=1)
    o_ref[...] = x + _dot(o, wo_ref[...])


def _merge_xattn(x2, o_cmp, o_sel, o_win, z, yb, yc, wa, wb, wc, wmix, g, wq, qn, kn, mkv, wo, layer, s, tm):
    m, d = x2.shape
    wl = lambda i: (layer, 0, 0)
    nt = s // tm
    ex = np.zeros((3, 128, 512), np.float32)
    for c in range(3):
        for h in range(NSA_HEADS):
            ex[c, c * NSA_HEADS + h, h * 64:(h + 1) * 64] = 1.0
    row = lambda i: (i, 0)
    c2 = lambda i: (0, 0)
    once = dict(pipeline_mode=pl.Buffered(1))
    return pl.pallas_call(
        _merge_xattn_kernel,
        grid=(m // tm,),
        in_specs=[pl.BlockSpec((tm, d), row),
                  pl.BlockSpec((tm, 512), row), pl.BlockSpec((tm, 512), row), pl.BlockSpec((tm, 512), row),
                  pl.BlockSpec((tm, 128), lambda i: (i, COL_NG // 128)),
                  pl.BlockSpec((tm, 256), row), pl.BlockSpec((tm, 256), row),
                  pl.BlockSpec((tm, 1024), lambda i: (i, 0)),
                  pl.BlockSpec((tm, 1024), lambda i: (i, 1)),
                  pl.BlockSpec((tm, 1024), lambda i: (i, 2)),
                  pl.BlockSpec((3, 128, 512), lambda i: (0, 0, 0)),
                  pl.BlockSpec((None, 512, d), wl, **once), pl.BlockSpec((None, 256, d), wl, **once),
                  pl.BlockSpec((None, 256, d), wl, **once),
                  pl.BlockSpec((None, d, d), wl, **once),
                  pl.BlockSpec((1, d), c2),
                  pl.BlockSpec((None, d, 512), wl, **once),
                  pl.BlockSpec((1, 128), c2), pl.BlockSpec((1, 128), c2),
                  pl.BlockSpec((N_MEM, 1024), lambda i: (i // nt, 0)),
                  pl.BlockSpec((None, 512, d), wl, **once)],
        out_specs=pl.BlockSpec((tm, d), row),
        out_shape=jax.ShapeDtypeStruct((m, d), F32),
        compiler_params=_cp("parallel"),
        name="merge_xattn",
    )(x2, o_cmp, o_sel, o_win, z, yb, yc, z, z, z, jnp.asarray(ex, BF16), wa, wb, wc, wmix,
      g.reshape(1, d), wq, qn.reshape(1, 128), kn.reshape(1, 128), mkv, wo)


FFN_HALO = 16


def _ffn_kernel(x_ref, xh_ref, g_ref, wua_ref, wub_ref, cwa_ref, cwb_ref, wd_ref, o_ref, h_ref, acc_ref, *, nt, tm):
    i = pl.program_id(0)
    f = pl.program_id(1)

    @pl.when(f == 0)
    def _():
        def norm(x):
            return x * lax.rsqrt(jnp.mean(x * x, axis=-1, keepdims=True) + EPS) * g_ref[...]
        first = (i % nt) == 0
        h_ref[0:FFN_HALO, :] = jnp.where(first, 0.0, norm(xh_ref[...])).astype(BF16)
        h_ref[FFN_HALO:, :] = norm(x_ref[...]).astype(BF16)
        acc_ref[...] = jnp.zeros(acc_ref.shape, F32)

    h = h_ref[...]

    def up_conv(w_ref, cw_ref):
        u = jnp.dot(h, w_ref[...], preferred_element_type=F32)
        cw = cw_ref[...]
        y = cw[2:3, :] * u + cw[1:2, :] * pltpu.roll(u, 1, 0) + cw[0:1, :] * pltpu.roll(u, 2, 0)
        return y[FFN_HALO:, :]

    a = up_conv(wua_ref, cwa_ref)
    bb = up_conv(wub_ref, cwb_ref)
    acc_ref[...] += _dot(_silu(a) * bb, wd_ref[...])

    @pl.when(f == pl.num_programs(1) - 1)
    def _():
        o_ref[...] = x_ref[...] + acc_ref[...]


def _ffn(x2, g, w_up, conv_w, w_down, layer, s, tm, tf):
    m, d = x2.shape
    nt = s // tm
    nf = D_FF // tf
    hb = tm // FFN_HALO
    kern = functools.partial(_ffn_kernel, nt=nt, tm=tm)
    return pl.pallas_call(
        kern,
        grid=(m // tm, nf),
        in_specs=[pl.BlockSpec((tm, d), lambda i, f: (i, 0)),
                  pl.BlockSpec((FFN_HALO, d), lambda i, f: (jnp.maximum(i * hb - 1, 0), 0)),
                  pl.BlockSpec((1, d), lambda i, f: (0, 0)),
                  pl.BlockSpec((None, d, tf), lambda i, f: (layer, 0, f)),
                  pl.BlockSpec((None, d, tf), lambda i, f: (layer, 0, nf + f)),
                  pl.BlockSpec((None, 3, tf), lambda i, f: (layer, 0, f)),
                  pl.BlockSpec((None, 3, tf), lambda i, f: (layer, 0, nf + f)),
                  pl.BlockSpec((None, tf, d), lambda i, f: (layer, f, 0))],
        out_specs=pl.BlockSpec((tm, d), lambda i, f: (i, 0)),
        out_shape=jax.ShapeDtypeStruct((m, d), F32),
        scratch_shapes=[pltpu.VMEM((tm + FFN_HALO, d), BF16), pltpu.VMEM((tm, d), F32)],
        compiler_params=_cp("parallel", "arbitrary"),
        name="conv_glu_ffn",
    )(x2, x2, g.reshape(1, d), w_up, w_up, conv_w, conv_w, w_down)


def _permute_w_in(w):
    nl, d = w.shape[0], w.shape[1]
    gate = w[:, :, 1280:1304].reshape(nl, d, NSA_HEADS, 3).transpose(0, 1, 3, 2).reshape(nl, d, 24)
    zeros = lambda n: jnp.zeros((nl, d, n), BF16)
    cols = lambda a, b_: w[:, :, a:b_].astype(BF16)
    return jnp.concatenate([cols(3360, 6432), cols(0, 1280), gate.astype(BF16), zeros(104), cols(3352, 3360), zeros(120),
                            cols(1304, 1560), cols(1816, 2328), cols(2328, 3352), cols(1560, 1816), zeros(256)], axis=2)


def _rope_tables(s):
    inv_freq = (1.0 / (ROPE_THETA ** (np.arange(0, HEAD_DIM, 2, dtype=np.float32) / HEAD_DIM))).astype(np.float32)
    ang = np.arange(s, dtype=np.float32)[:, None] * inv_freq[None, :]
    c, sn = np.cos(ang), np.sin(ang)
    return (jnp.asarray(np.tile(np.concatenate([c, c], axis=1), (1, 2)), F32),
            jnp.asarray(np.tile(np.concatenate([-sn, sn], axis=1), (1, 2)), F32))


def kernel(x, mem, mem_norm, mem_w_kv, hgrn_lb_logits, norm_mix, w_in, nsa_q_norm, nsa_k_norm, cmp_pos_k, cmp_pos_v, cmp_k_w1, cmp_k_w2, cmp_v_w1, cmp_v_w2, hgrn_out_norm, gdn_conv, gdn_a_log, gdn_dt_bias, gdn_out_norm, w_branch_a, w_branch_b, w_branch_c, w_mix_out, norm_cross, xattn_wq, xattn_q_norm, xattn_k_norm, xattn_wo, norm_ffn, ffn_w_up, ffn_conv, ffn_w_down):
    b, s, d = x.shape
    m = b * s
    depth = w_in.shape[0]
    cos_t, sin_t = _rope_tables(s)
    x2 = x.reshape(m, d)
    mkv = _norm_matmul(mem.reshape(b * N_MEM, d), mem_norm, mem_w_kv.astype(BF16)[None], 0, N_MEM, 512)

    tm_in = min(1024, m)
    w_in_p = _permute_w_in(w_in)
    w_up, w_down = ffn_w_up.astype(BF16), ffn_w_down.astype(BF16)
    w_br_a, w_br_b, w_br_c = w_branch_a.astype(BF16), w_branch_b.astype(BF16), w_branch_c.astype(BF16)
    w_mix, w_xq, w_xo = w_mix_out.astype(BF16), xattn_wq.astype(BF16), xattn_wo.astype(BF16)
    for l in range(depth):
        z = _norm_matmul(x2, norm_mix[l], w_in_p, l, tm_in, IN_WIDTH_P // 3, BF16, split_last=256)
        q_r, rk, rv, kst, kwt, vs, vw = _nsa_prep(z, cos_t, sin_t, nsa_q_norm[l], nsa_k_norm[l], b, s, 512)
        kct, vc = _compress(rk, rv, cmp_pos_k[l], cmp_pos_v[l], cmp_k_w1[l], cmp_k_w2[l], cmp_v_w1[l], cmp_v_w2[l])
        o_cmp, sel = _cmp_attn(q_r, kct, vc, b, s, 256)
        o_sel = _sel_attn(q_r, kst, vs, sel, b, s, 512, 512)
        o_win = _win_attn(q_r, kwt, vw, b, s, 256)
        yb = _hgrn(z, hgrn_lb_logits, hgrn_out_norm[l], l, b, s, 512)
        yc = _gdn(z, gdn_conv[l], gdn_a_log[l], gdn_dt_bias[l], gdn_out_norm[l], b, s, 512)
        x2 = _merge_xattn(x2, o_cmp, o_sel, o_win, z, yb, yc, w_br_a, w_br_b, w_br_c, w_mix,
                          norm_cross[l], w_xq, xattn_q_norm[l], xattn_k_norm[l], mkv, w_xo, l, s, 512)
        x2 = _ffn(x2, norm_ffn[l], w_up, ffn_conv, w_down, l, s, 512, 1408)
    return x2.reshape(b, s, d)
```

```python
import functools
import math

import numpy as np
import jax
import jax.numpy as jnp
from jax import lax
from jax.experimental import pallas as pl
from jax.experimental.pallas import tpu as pltpu

F32 = jnp.float32
BF16 = jnp.bfloat16

EPS = 1e-6
ROPE_THETA = 10000.0
NEG_BIG = -1e30
TINY = 1e-20
FORCE_SCORE = 1e6
SEL_MASK = 2.0 ** 30

D_MODEL = 1024
N_MEM = 256
HEAD_DIM = 64
NSA_HEADS = 8
NSA_GROUPS = 2
NSA_HPG = 4
CMP_BLOCK = 32
CMP_STRIDE = 16
CMP_HIDDEN = 128
SEL_BLOCK = 64
SEL_TOPK = 16
WINDOW = 512
N_HEADS4 = 4
CHUNK = 64
SUB = 16
XATTN_HEAD_DIM = 128
D_FF = 2816

COL_M = 0
COL_NQ = 3072
COL_KV = 3584
COL_NG = 4352
COL_BA = 4480
COL_HG = 4608
COL_GD = 5376
COL_HF = 6400
IN_WIDTH_P = 6912

VMEM_LIMIT = 48 * 1024 * 1024


def _cp(*sem):
    return pltpu.CompilerParams(dimension_semantics=sem, vmem_limit_bytes=VMEM_LIMIT)


def _dot(a, b):
    return jnp.dot(a.astype(BF16), b.astype(BF16), preferred_element_type=F32)


def _dot_nt(a, b):
    return lax.dot_general(a.astype(BF16), b.astype(BF16), (((1,), (1,)), ((), ())),
                           preferred_element_type=F32)


def _split(a, n):
    parts = []
    r = a
    for _ in range(n):
        p = r.astype(BF16)
        parts.append(p)
        r = r - p.astype(F32)
    return parts


def _rows_dot(blocks, rhs):
    if len(blocks) == 1:
        return [jnp.dot(blocks[0], rhs, preferred_element_type=F32)]
    r = jnp.dot(jnp.concatenate(blocks, axis=0), rhs, preferred_element_type=F32)
    out, o = [], 0
    for blk in blocks:
        out.append(r[o:o + blk.shape[0]])
        o += blk.shape[0]
    return out


def _dot_ls(a, b_exact, n=2):
    parts = _rows_dot(_split(a, n), b_exact)
    acc = parts[0]
    for p in parts[1:]:
        acc = acc + p
    return acc


def _chunk_cumsum(x):
    row = lax.broadcasted_iota(jnp.int32, (x.shape[0], 1), 0) % CHUNK
    sh = 1
    while sh < CHUNK:
        x = x + jnp.where(row >= sh, pltpu.roll(x, sh, 0), 0.0)
        sh *= 2
    return x


def _as_column(row):
    return jnp.broadcast_to(row, (8, row.shape[1])).T[:, 0:1]


def _sigmoid(x):
    return 1.0 / (1.0 + jnp.exp(-x))


def _silu(x):
    return x * _sigmoid(x)


def _block_diag4(y, keep):
    yb = y.astype(BF16)
    return jnp.where(keep, jnp.concatenate([yb, yb, yb, yb], axis=0), jnp.zeros((), BF16))


def _bd_mask():
    r = lax.broadcasted_iota(jnp.int32, (256, 256), 0) // 64
    c = lax.broadcasted_iota(jnp.int32, (256, 256), 1) // 64
    return r == c


def _norm_matmul_kernel(x_ref, g_ref, w_ref, o_ref, h_ref, *, split_last):
    j = pl.program_id(1)

    @pl.when(j == 0)
    def _():
        x = x_ref[...]
        ms = jnp.mean(x * x, axis=-1, keepdims=True)
        h_ref[...] = (x * lax.rsqrt(ms + EPS) * g_ref[...]).astype(BF16)

    r = jnp.dot(h_ref[...], w_ref[...], preferred_element_type=F32)
    if not split_last:
        o_ref[...] = r.astype(o_ref.dtype)
    else:
        last = pl.num_programs(1) - 1
        n, k = r.shape[1], split_last

        @pl.when(j != last)
        def _():
            o_ref[...] = r.astype(o_ref.dtype)

        @pl.when(j == last)
        def _():
            o_ref[:, 0:n - 2 * k] = r[:, 0:n - 2 * k].astype(o_ref.dtype)
            lead = r[:, n - 2 * k:n - k].astype(BF16)
            o_ref[:, n - 2 * k:n - k] = lead
            o_ref[:, n - k:n] = (r[:, n - 2 * k:n - k] - lead.astype(F32)).astype(BF16)


def _norm_matmul(x, g, w, layer, tm, tn, out_dtype=F32, split_last=0):
    m, d = x.shape
    n = w.shape[2]
    return pl.pallas_call(
        functools.partial(_norm_matmul_kernel, split_last=split_last),
        grid=(m // tm, n // tn),
        in_specs=[pl.BlockSpec((tm, d), lambda i, j: (i, 0)),
                  pl.BlockSpec((1, d), lambda i, j: (0, 0)),
                  pl.BlockSpec((None, d, tn), lambda i, j: (layer, 0, j))],
        out_specs=pl.BlockSpec((tm, tn), lambda i, j: (i, j)),
        out_shape=jax.ShapeDtypeStruct((m, n), out_dtype),
        scratch_shapes=[pltpu.VMEM((tm, d), BF16)],
        compiler_params=_cp("parallel", "arbitrary"),
        name="norm_matmul",
    )(x, g.reshape(1, d), w)


def _head_norm(x, bd, g):
    ms = _dot_ls(x * x, bd, 2)
    return x * lax.rsqrt(ms + EPS) * g


def _rope(x, c, s):
    w = x.shape[1]
    lane = lax.broadcasted_iota(jnp.int32, x.shape, 1)
    sw = jnp.where((lane & 32) != 0, pltpu.roll(x, 32, 1), pltpu.roll(x, w - 32, 1))
    return x * c + sw * s


def _nsa_prep_kernel(q_ref, kvc_ref, kvs_ref, kvw_ref, cos_ref, sin_ref, qn_ref, kn_ref, bd512_ref, bd128_ref,
                     qo_ref, kc_ref, vc_ref, kst_ref, kwt_ref, vs_ref, vw_ref, tok_ref):
    c = cos_ref[...]
    s = sin_ref[...]
    c4 = jnp.concatenate([c, c, c, c], axis=1)
    s4 = jnp.concatenate([s, s, s, s], axis=1)
    q = _rope(_head_norm(q_ref[...].astype(F32), bd512_ref[...], qn_ref[...]), c4, s4)
    qo_ref[...] = (q * (HEAD_DIM ** -0.5)).astype(BF16)

    def key(ref, row):
        return _rope(_head_norm(ref[:, 0:128].astype(F32), bd128_ref[...], kn_ref[row:row + 1, :]), c, s)

    tok_ref[0] = key(kvc_ref, 0)
    tok_ref[1] = kvc_ref[:, 128:256].astype(F32)
    n16 = tok_ref.shape[1] // CMP_STRIDE
    for r in range(CMP_STRIDE):
        kc_ref[:, r * 128:(r + 1) * 128] = tok_ref[0, pl.ds(r, n16, stride=CMP_STRIDE), :]
        vc_ref[:, r * 128:(r + 1) * 128] = tok_ref[1, pl.ds(r, n16, stride=CMP_STRIDE), :]
    kst_ref[...] = key(kvs_ref, 1).T.astype(BF16)
    kwt_ref[...] = key(kvw_ref, 2).T.astype(BF16)
    vs = kvs_ref[:, 128:256].astype(BF16)
    vw = kvw_ref[:, 128:256].astype(BF16)
    ones = jnp.ones((vs.shape[0], 64), BF16)
    for g in range(NSA_GROUPS):
        vs_ref[g] = jnp.concatenate([vs[:, g * 64:(g + 1) * 64], ones], axis=1)
        vw_ref[g] = jnp.concatenate([vw[:, g * 64:(g + 1) * 64], ones], axis=1)


def _nsa_prep(z, cos_t, sin_t, q_norm, k_norm, b, s, tm):
    m = b * s
    nt = s // tm
    bd512 = jnp.asarray(np.kron(np.eye(8), np.full((64, 64), 1.0 / 64)), BF16)
    bd128 = jnp.asarray(np.kron(np.eye(2), np.full((64, 64), 1.0 / 64)), BF16)
    qn = jnp.tile(q_norm, 8).reshape(1, 512)
    kn = jnp.tile(k_norm, (1, 2))
    row = lambda i: (i, 0)
    const = lambda i: (0, 0)
    return pl.pallas_call(
        _nsa_prep_kernel,
        grid=(m // tm,),
        in_specs=[pl.BlockSpec((tm, 512), lambda i: (i, COL_NQ // 512)),
                  pl.BlockSpec((tm, 256), lambda i: (i, COL_KV // 256)),
                  pl.BlockSpec((tm, 256), lambda i: (i, COL_KV // 256 + 1)),
                  pl.BlockSpec((tm, 256), lambda i: (i, COL_KV // 256 + 2)),
                  pl.BlockSpec((tm, 128), lambda i: (i % nt, 0)),
                  pl.BlockSpec((tm, 128), lambda i: (i % nt, 0)),
                  pl.BlockSpec((1, 512), const),
                  pl.BlockSpec((3, 128), const),
                  pl.BlockSpec((512, 512), const),
                  pl.BlockSpec((128, 128), const)],
        out_specs=[pl.BlockSpec((tm, 512), row),
                   pl.BlockSpec((None, tm // CMP_STRIDE, CMP_STRIDE * 128), lambda i: (i // nt, i % nt, 0)),
                   pl.BlockSpec((None, tm // CMP_STRIDE, CMP_STRIDE * 128), lambda i: (i // nt, i % nt, 0)),
                   pl.BlockSpec((None, 128, tm), lambda i: (i // nt, 0, i % nt)),
                   pl.BlockSpec((None, 128, tm), lambda i: (i // nt, 0, i % nt)),
                   pl.BlockSpec((NSA_GROUPS, tm, 128), lambda i: (0, i, 0)),
                   pl.BlockSpec((NSA_GROUPS, tm, 128), lambda i: (0, i, 0))],
        out_shape=[jax.ShapeDtypeStruct((m, 512), BF16),
                   jax.ShapeDtypeStruct((b, s // CMP_STRIDE, CMP_STRIDE * 128), F32),
                   jax.ShapeDtypeStruct((b, s // CMP_STRIDE, CMP_STRIDE * 128), F32),
                   jax.ShapeDtypeStruct((b, 128, s), BF16),
                   jax.ShapeDtypeStruct((b, 128, s), BF16),
                   jax.ShapeDtypeStruct((NSA_GROUPS, m, 128), BF16),
                   jax.ShapeDtypeStruct((NSA_GROUPS, m, 128), BF16)],
        scratch_shapes=[pltpu.VMEM((2, tm, 128), F32)],
        compiler_params=_cp("parallel"),
        name="nsa_prep",
    )(z, z, z, z, cos_t, sin_t, qn, kn, bd512, bd128)


def _gelu_tanh(x):
    return 0.5 * x * (1.0 + jnp.tanh(math.sqrt(2.0 / math.pi) * (x + 0.044715 * (x * x * x))))


def _compress_kernel(rk_ref, rv_ref, pk_ref, pv_ref, wkt_ref, wkb_ref, wvt_ref, wvb_ref, wk2_ref, wv2_ref,
                     kct_ref, vc_ref):
    def mlp(r_ref, p_ref, wt_ref, wb_ref, w2_ref):
        r = r_ref[...]
        n = r.shape[0]
        top = _dot(r + p_ref[0:1, :], wt_ref[...])
        bot = _dot(r + p_ref[1:2, :], wb_ref[...])
        hid = top + pltpu.roll(bot, n - 1, 0)
        return _dot(_gelu_tanh(hid), w2_ref[...])

    kc = mlp(rk_ref, pk_ref, wkt_ref, wkb_ref, wk2_ref)
    vc = mlp(rv_ref, pv_ref, wvt_ref, wvb_ref, wv2_ref)
    kct_ref[...] = kc.T
    for g in range(NSA_GROUPS):
        vc_ref[g] = vc[:, g * 64:(g + 1) * 64]


def _compress_weights(pos, w1, w2):
    w1r = w1.reshape(2, 16, 64, CMP_HIDDEN)
    zero = jnp.zeros_like(w1r)
    big = jnp.stack([jnp.stack([w1r, zero], axis=-2), jnp.stack([zero, w1r], axis=-2)], axis=2)
    big = big.reshape(2, 16 * 2 * 64, 2 * CMP_HIDDEN).astype(BF16)
    posr = jnp.broadcast_to(pos.reshape(2, 16, 1, 64), (2, 16, 2, 64)).reshape(2, 2048)
    w2bd = jnp.zeros((2, CMP_HIDDEN, 2, 64), F32)
    w2bd = w2bd.at[0, :, 0, :].set(w2).at[1, :, 1, :].set(w2).reshape(2 * CMP_HIDDEN, 128).astype(BF16)
    return posr, big[0], big[1], w2bd


def _compress(rk, rv, pos_k, pos_v, ck_w1, ck_w2, cv_w1, cv_w2):
    b, n, _ = rk.shape
    pk, wkt, wkb, wk2 = _compress_weights(pos_k, ck_w1, ck_w2)
    pv, wvt, wvb, wv2 = _compress_weights(pos_v, cv_w1, cv_w2)
    c2 = lambda i: (0, 0)
    return pl.pallas_call(
        _compress_kernel,
        grid=(b,),
        in_specs=[pl.BlockSpec((None, n, 2048), lambda i: (i, 0, 0)),
                  pl.BlockSpec((None, n, 2048), lambda i: (i, 0, 0)),
                  pl.BlockSpec((2, 2048), c2), pl.BlockSpec((2, 2048), c2),
                  pl.BlockSpec((2048, 256), c2), pl.BlockSpec((2048, 256), c2),
                  pl.BlockSpec((2048, 256), c2), pl.BlockSpec((2048, 256), c2),
                  pl.BlockSpec((256, 128), c2), pl.BlockSpec((256, 128), c2)],
        out_specs=[pl.BlockSpec((None, 128, n), lambda i: (i, 0, 0)),
                   pl.BlockSpec((None, NSA_GROUPS, n, 64), lambda i: (i, 0, 0, 0))],
        out_shape=[jax.ShapeDtypeStruct((b, 128, n), F32),
                   jax.ShapeDtypeStruct((b, NSA_GROUPS, n, 64), F32)],
        compiler_params=_cp("parallel"),
        name="nsa_compress",
    )(rk, rv, pk, pv, wkt, wkb, wvt, wvb, wk2, wv2)


def _cmp_attn_kernel(q_ref, kct_ref, vc_ref, ovt_ref, o_ref, sel_ref, imp_ref, *, tq, ncp, nsel):
    i = pl.program_id(1)
    tpos = i * tq + lax.broadcasted_iota(jnp.int32, (tq, 1), 0)
    nblk = lax.broadcasted_iota(jnp.int32, (1, ncp), 1)
    bias = jnp.where((nblk * CMP_STRIDE + (CMP_BLOCK - 1)) <= tpos, 0.0, NEG_BIG)
    any_valid = jnp.where(tpos >= CMP_BLOCK - 1, 1.0, 0.0)
    j = lax.broadcasted_iota(jnp.int32, (nsel, 1), 0)
    cur = (i * tq + lax.broadcasted_iota(jnp.int32, (1, tq), 1)) // SEL_BLOCK
    forced = (j == 0) | (j == cur) | (j == cur - 1)
    jl = lax.broadcasted_iota(jnp.int32, (8, 1), 0)
    ovt = ovt_ref[...]

    scores = []
    for g in range(NSA_GROUPS):
        q4 = jnp.concatenate([q_ref[:, (g * NSA_HPG + h) * 64:(g * NSA_HPG + h + 1) * 64]
                              for h in range(NSA_HPG)], axis=0)
        scores.append(jnp.dot(q4, kct_ref[g * 64:(g + 1) * 64, :].astype(BF16), preferred_element_type=F32))
    psums = []
    for g, s in enumerate(scores):
        s = s.reshape(NSA_HPG, tq, ncp) + bias[None]
        e = jnp.exp(s - jnp.max(s, axis=-1, keepdims=True))
        p = e * (any_valid / jnp.maximum(jnp.sum(e, axis=-1, keepdims=True), 1e-30))
        o = jnp.dot(p.reshape(NSA_HPG * tq, ncp).astype(BF16), vc_ref[g].astype(BF16), preferred_element_type=F32)
        for h in range(NSA_HPG):
            c = (g * NSA_HPG + h) * 64
            o_ref[:, c:c + 64] = o[h * tq:(h + 1) * tq].astype(o_ref.dtype)
        psums.append(p[0] + p[1] + p[2] + p[3])
    for g, psum in enumerate(psums):
        ph, plo = _split(psum, 2)
        imp = _dot_nt(ovt, ph) + _dot_nt(ovt, plo)
        imp_ref[g] = jnp.where(j <= cur, jnp.where(forced, FORCE_SCORE, imp), -1.0)
    for g in range(NSA_GROUPS):
        val = imp_ref[g]
        rows8 = [val[r8 * 8:(r8 + 1) * 8] for r8 in range(nsel // 8)]
        cnt = [jnp.zeros((8, tq), F32) for _ in rows8]
        for r in range(nsel):
            vr = imp_ref[g, r:r + 1, :]
            for r8, vg in enumerate(rows8):
                if r8 * 8 > r:
                    beats = vr >= vg
                elif r8 * 8 + 7 < r:
                    beats = vr > vg
                else:
                    beats = (vr > vg) | ((vr == vg) & (jl + r8 * 8 > r))
                cnt[r8] = cnt[r8] + jnp.where(beats, 1.0, 0.0)
        sel = jnp.where(jnp.concatenate(cnt, axis=0) < float(min(SEL_TOPK, nsel)), 1.0, 0.0)
        sel_ref[g] = sel.T.astype(BF16)


def _cmp_attn(q_r, kct, vc, b, s, tq):
    m = b * s
    nq = s // tq
    ncp = kct.shape[2]
    nsel = s // SEL_BLOCK
    n_cmp = (s - CMP_BLOCK) // CMP_STRIDE + 1
    tok = np.arange(n_cmp)[:, None] * CMP_STRIDE + np.arange(CMP_BLOCK)[None, :]
    ov = np.zeros((ncp, nsel), np.float32)
    for l in range(CMP_BLOCK):
        ov[np.arange(n_cmp), tok[:, l] // SEL_BLOCK] += 1.0
    ovt = jnp.asarray(ov.T, BF16)
    kern = functools.partial(_cmp_attn_kernel, tq=tq, ncp=ncp, nsel=nsel)
    return pl.pallas_call(
        kern,
        grid=(b, nq),
        in_specs=[pl.BlockSpec((tq, 512), lambda bb, i: (bb * nq + i, 0)),
                  pl.BlockSpec((None, 128, ncp), lambda bb, i: (bb, 0, 0)),
                  pl.BlockSpec((None, NSA_GROUPS, ncp, 64), lambda bb, i: (bb, 0, 0, 0)),
                  pl.BlockSpec((nsel, ncp), lambda bb, i: (0, 0))],
        out_specs=[pl.BlockSpec((tq, 512), lambda bb, i: (bb * nq + i, 0)),
                   pl.BlockSpec((NSA_GROUPS, tq, nsel), lambda bb, i: (0, bb * nq + i, 0))],
        out_shape=[jax.ShapeDtypeStruct((m, 512), BF16),
                   jax.ShapeDtypeStruct((NSA_GROUPS, m, nsel), BF16)],
        scratch_shapes=[pltpu.VMEM((NSA_GROUPS, nsel, tq), F32)],
        compiler_params=_cp("parallel", "parallel"),
        name="nsa_cmp_attn",
    )(q_r, kct, vc, ovt)


def _sel_attn_kernel(qi_ref, kt_ref, first_ref, last_ref, q_ref, k_ref, v_ref, sel_ref, o_ref, q4_ref, m_ref, acc_ref,
                     *, tq, tk, nsel):
    n = pl.program_id(1)
    qi = qi_ref[n]
    ktile = kt_ref[n]
    rows = NSA_HPG * tq
    groups = [slice(g * rows, (g + 1) * rows) for g in range(NSA_GROUPS)]

    @pl.when(first_ref[n] == 1)
    def _():
        for g in range(NSA_GROUPS):
            unsel = ((sel_ref[g].astype(F32) - 1.0) * SEL_MASK).astype(BF16)
            for h in range(NSA_HPG):
                r0 = g * rows + h * tq
                q4_ref[r0:r0 + tq, 0:64] = q_ref[:, (g * NSA_HPG + h) * 64:(g * NSA_HPG + h + 1) * 64]
                q4_ref[r0:r0 + tq, 64:64 + nsel] = unsel
        m_ref[...] = jnp.full(m_ref.shape, NEG_BIG, F32)
        acc_ref[...] = jnp.zeros(acc_ref.shape, F32)

    def step(with_causal):
        kpos = ktile * tk + lax.broadcasted_iota(jnp.int32, (1, tk), 1)
        blk = lax.broadcasted_iota(jnp.int32, (nsel, 1), 0)
        own_block = jnp.where(blk == kpos // SEL_BLOCK, 1.0, 0.0).astype(BF16)
        scores = [jnp.dot(q4_ref[rs, :], jnp.concatenate([k_ref[g * 64:(g + 1) * 64, :], own_block], axis=0),
                          preferred_element_type=F32) for g, rs in enumerate(groups)]
        if with_causal:
            qpos = qi * tq + lax.broadcasted_iota(jnp.int32, (tq, 1), 0)
            causal = jnp.where(kpos <= qpos, 0.0, NEG_BIG)
        for g, (rs, s) in enumerate(zip(groups, scores)):
            if with_causal:
                s = (s.reshape(NSA_HPG, tq, tk) + causal[None]).reshape(rows, tk)
            m_prev = m_ref[rs, :]
            m_next = jnp.maximum(m_prev, jnp.max(s, axis=-1, keepdims=True))
            p = jnp.exp(s - jnp.tile(m_next, (1, tk // 128)))
            pv = jnp.dot(p.astype(BF16), v_ref[g], preferred_element_type=F32)
            acc_ref[rs, :] = jnp.exp(m_prev - m_next) * acc_ref[rs, :] + pv
            m_ref[rs, :] = m_next

    reaches_past = ktile * tk + tk - 1 > qi * tq
    pl.when(reaches_past)(functools.partial(step, True))
    pl.when(jnp.logical_not(reaches_past))(functools.partial(step, False))

    @pl.when(last_ref[n] == 1)
    def _():
        a = acc_ref[...]
        o = a / pltpu.roll(a, 64, 1)
        for hh in range(NSA_HEADS):
            o_ref[:, hh * 64:(hh + 1) * 64] = o[hh * tq:(hh + 1) * tq, 0:64].astype(o_ref.dtype)


def _sel_attn(q_r, kt, v, sel, b, s, tq, tk):
    m = b * s
    nq = s // tq
    nkb = s // tk
    nsel = s // SEL_BLOCK
    pairs = []
    for qi in range(nq):
        kts = [k for k in range(nkb) if k * tk <= qi * tq + tq - 1]
        pairs += [(qi, k, int(j == 0), int(j == len(kts) - 1)) for j, k in enumerate(kts)]
    tabs = [jnp.asarray(np.array([pr[c] for pr in pairs], np.int32)) for c in range(4)]
    kern = functools.partial(_sel_attn_kernel, tq=tq, tk=tk, nsel=nsel)
    return pl.pallas_call(
        kern,
        grid_spec=pltpu.PrefetchScalarGridSpec(
            num_scalar_prefetch=4,
            grid=(b, len(pairs)),
            in_specs=[pl.BlockSpec((tq, 512), lambda bb, n, qt, kt_, f, l: (bb * nq + qt[n], 0)),
                      pl.BlockSpec((None, 128, tk), lambda bb, n, qt, kt_, f, l: (bb, 0, kt_[n])),
                      pl.BlockSpec((NSA_GROUPS, tk, 128), lambda bb, n, qt, kt_, f, l: (0, bb * nkb + kt_[n], 0)),
                      pl.BlockSpec((NSA_GROUPS, tq, nsel), lambda bb, n, qt, kt_, f, l: (0, bb * nq + qt[n], 0))],
            out_specs=pl.BlockSpec((tq, 512), lambda bb, n, qt, kt_, f, l: (bb * nq + qt[n], 0)),
            scratch_shapes=[pltpu.VMEM((NSA_HEADS * tq, 64 + nsel), BF16), pltpu.VMEM((NSA_HEADS * tq, 128), F32),
                            pltpu.VMEM((NSA_HEADS * tq, 128), F32)]),
        out_shape=jax.ShapeDtypeStruct((m, 512), BF16),
        compiler_params=_cp("parallel", "arbitrary"),
        name="nsa_sel_attn",
    )(*tabs, q_r, kt, v, sel)


def _win_attn_kernel(q_ref, *refs, tq, nblk):
    k_refs, v_refs, o_ref = refs[:nblk], refs[nblk:2 * nblk], refs[2 * nblk]
    qi = pl.program_id(1)
    kpos = (qi - (nblk - 1)) * tq + lax.broadcasted_iota(jnp.int32, (1, nblk * tq), 1)
    dist = qi * tq + lax.broadcasted_iota(jnp.int32, (tq, 1), 0) - kpos
    bias = jnp.where((dist >= 0) & (dist < WINDOW) & (kpos >= 0), 0.0, NEG_BIG)
    scores = []
    for g in range(NSA_GROUPS):
        q4 = jnp.concatenate([q_ref[:, (g * NSA_HPG + h) * 64:(g * NSA_HPG + h + 1) * 64]
                              for h in range(NSA_HPG)], axis=0)
        kt = jnp.concatenate([r[g * 64:(g + 1) * 64, :] for r in k_refs], axis=1)
        scores.append(jnp.dot(q4, kt, preferred_element_type=F32))
    for g, s in enumerate(scores):
        s = (s.reshape(NSA_HPG, tq, nblk * tq) + bias[None]).reshape(NSA_HPG * tq, nblk * tq)
        p = jnp.exp(s - jnp.max(s, axis=-1, keepdims=True))
        v = jnp.concatenate([r[g] for r in v_refs], axis=0)
        pv = jnp.dot(p.astype(BF16), v, preferred_element_type=F32)
        o = pv / pltpu.roll(pv, 64, 1)
        for h in range(NSA_HPG):
            c = (g * NSA_HPG + h) * 64
            o_ref[:, c:c + 64] = o[h * tq:(h + 1) * tq, 0:64].astype(o_ref.dtype)


def _win_attn(q_r, kt, v, b, s, tq):
    m = b * s
    nq = s // tq
    nblk = WINDOW // tq + 1
    kidx = lambda qi, j: jnp.maximum(qi - (nblk - 1) + j, 0)
    k_specs = [pl.BlockSpec((None, 128, tq), functools.partial(lambda bb, qi, j: (bb, 0, kidx(qi, j)), j=j))
               for j in range(nblk)]
    v_specs = [pl.BlockSpec((NSA_GROUPS, tq, 128), functools.partial(lambda bb, qi, j: (0, bb * nq + kidx(qi, j), 0), j=j))
               for j in range(nblk)]
    kern = functools.partial(_win_attn_kernel, tq=tq, nblk=nblk)
    return pl.pallas_call(
        kern,
        grid=(b, nq),
        in_specs=[pl.BlockSpec((tq, 512), lambda bb, qi: (bb * nq + qi, 0))] + k_specs + v_specs,
        out_specs=pl.BlockSpec((tq, 512), lambda bb, qi: (bb * nq + qi, 0)),
        out_shape=jax.ShapeDtypeStruct((m, 512), BF16),
        compiler_params=_cp("parallel", "parallel"),
        name="nsa_win_attn",
    )(q_r, *([kt] * nblk), *([v] * nblk))


def _consts_recurrent():
    bd_mean = np.kron(np.eye(4), np.full((64, 64), 1.0 / 64))
    bd_ones = np.kron(np.eye(4), np.ones((64, 64)))
    return jnp.asarray(bd_mean, BF16), jnp.asarray(bd_ones, BF16)


def _group_rmsnorm(o, bd_mean, g):
    ms = _dot_ls(o * o, bd_mean, 2)
    return o * lax.rsqrt(ms + EPS) * g


def _hgrn_kernel(q_ref, i_ref, g_ref, fh_ref, fl_ref, lbl_ref, on_ref, bdm_ref, bdo_ref, selp_ref,
                 o_ref, st_ref, z_ref, *, layer, depth, t):
    @pl.when(pl.program_id(1) == 0)
    def _():
        st_ref[...] = jnp.zeros(st_ref.shape, F32)

    lg = lbl_ref[...]
    mx = jnp.max(lg, axis=0, keepdims=True)
    ex = jnp.exp(lg - mx)
    pr = ex / jnp.sum(ex, axis=0, keepdims=True)
    cs = pr[0:1, :]
    for r in range(1, layer + 1):
        cs = cs + pr[r:r + 1, :]
    lb = cs - pr[0:1, :]

    q = _silu(q_ref[...].astype(F32))
    fz = fh_ref[...].astype(F32) + fl_ref[...].astype(F32)
    f = lb + (1.0 - lb) * _sigmoid(fz)
    logf = jnp.log(jnp.maximum(f, TINY))
    k = (1.0 - lb) * _sigmoid(-fz)
    v = i_ref[...].astype(F32)
    b = _chunk_cumsum(logf)

    bdm = _bd_mask()
    bdk = bdo_ref[...] > 0
    nsub = t // SUB
    tl = lax.broadcasted_iota(jnp.int32, (1, SUB, 1), 1)
    for hp in range(2):
        ls = slice(hp * 128, (hp + 1) * 128)
        q3 = q[:, ls].reshape(nsub, SUB, 128)
        k3 = k[:, ls].reshape(nsub, SUB, 128)
        b3 = b[:, ls].reshape(nsub, SUB, 128)
        for sl in range(SUB):
            e = jnp.exp(jnp.where(tl >= sl, b3 - b3[:, sl:sl + 1, :], NEG_BIG))
            zz = q3 * k3[:, sl:sl + 1, :] * e
            z_ref[hp * t:(hp + 1) * t, sl * 128:(sl + 1) * 128] = zz.reshape(t, 128)
    a_pairs = _dot_ls(z_ref[...], selp_ref[...], 2)
    a_all = jnp.concatenate([a_pairs[0:t], a_pairs[t:2 * t]], axis=1)
    trow = lax.broadcasted_iota(jnp.int32, (CHUNK, 256), 0)
    scol = lax.broadcasted_iota(jnp.int32, (CHUNK, 256), 1) % CHUNK
    diag_mask = (trow // SUB) == (scol // SUB)
    m1 = (trow >= 32) & (scol < 32)
    m2 = ((trow >= 16) & (trow < 32) & (scol < 16)) | ((trow >= 48) & (scol >= 32) & (scol < 48))
    rowi = lax.broadcasted_iota(jnp.int32, (CHUNK, 1), 0)

    chunks = [slice(c * CHUNK, (c + 1) * CHUNK) for c in range(t // CHUNK)]
    lvl1, lvl2 = [], []
    for rs in chunks:
        qc, kc, bc = q[rs], k[rs], b[rs]
        r1 = bc[31:32, :]
        q1 = qc * jnp.exp(jnp.minimum(bc - r1, 0.0))
        k1 = jnp.where(rowi < 32, kc * jnp.exp(jnp.minimum(r1 - bc, 0.0)), 0.0)
        r2 = jnp.where(rowi < 32, bc[15:16, :], bc[47:48, :])
        q2 = qc * jnp.exp(jnp.minimum(bc - r2, 0.0))
        k2 = kc * jnp.exp(jnp.minimum(r2 - bc, 0.0))
        lvl1.append(_dot_nt(q1, _block_diag4(k1, bdk)))
        lvl2.append(_dot_nt(q2, _block_diag4(k2, bdk)))
    o_intra, s_add, s_dec, q_dec = [], [], [], []
    for rs, a1, a2 in zip(chunks, lvl1, lvl2):
        kc, vc, bc = k[rs], v[rs], b[rs]
        attn = jnp.where(diag_mask, a_all[rs], 0.0) + jnp.where(m1, a1, 0.0) + jnp.where(m2, a2, 0.0)
        o_intra.append(_dot(attn, _block_diag4(vc, bdk)))
        bl = bc[CHUNK - 1:CHUNK, :]
        kdt = (kc * jnp.exp(bl - bc)).T
        s_add.append(jnp.where(bdm, _dot(kdt, vc), 0.0))
        s_dec.append(_as_column(jnp.exp(bl)))
        q_dec.append((q[rs] * jnp.exp(bc)).astype(BF16))

    st = st_ref[...]
    outs = []
    for oi, sa, sd, qd in zip(o_intra, s_add, s_dec, q_dec):
        outs.append(oi + jnp.dot(qd, st.astype(BF16), preferred_element_type=F32))
        st = sd * st + sa
    st_ref[...] = st
    o = jnp.concatenate(outs, axis=0)
    o_ref[...] = (_group_rmsnorm(o, bdm_ref[...], on_ref[...]) * _sigmoid(g_ref[...].astype(F32))).astype(BF16)


def _hgrn(z, lb_logits, out_norm, layer, b, s, t):
    m = b * s
    nt = s // t
    depth = lb_logits.shape[0]
    bd_mean, bd_ones = _consts_recurrent()
    selp = np.zeros((SUB, 2, 64, 2, CHUNK), np.float32)
    for sl in range(SUB):
        for h2 in range(2):
            selp[sl, h2, :, h2, sl::SUB] = 1.0
    selp = jnp.asarray(selp.reshape(SUB * 128, 128), BF16)
    col, colf = COL_HG // 256, COL_HF // 256
    c2 = lambda bb, i: (0, 0)
    kern = functools.partial(_hgrn_kernel, layer=layer, depth=depth, t=t)
    return pl.pallas_call(
        kern,
        grid=(b, nt),
        in_specs=[pl.BlockSpec((t, 256), lambda bb, i: (bb * nt + i, col)),
                  pl.BlockSpec((t, 256), lambda bb, i: (bb * nt + i, col + 1)),
                  pl.BlockSpec((t, 256), lambda bb, i: (bb * nt + i, col + 2)),
                  pl.BlockSpec((t, 256), lambda bb, i: (bb * nt + i, colf)),
                  pl.BlockSpec((t, 256), lambda bb, i: (bb * nt + i, colf + 1)),
                  pl.BlockSpec((depth, 256), c2),
                  pl.BlockSpec((1, 256), c2),
                  pl.BlockSpec((256, 256), c2),
                  pl.BlockSpec((256, 256), c2),
                  pl.BlockSpec((SUB * 128, 128), c2)],
        out_specs=pl.BlockSpec((t, 256), lambda bb, i: (bb * nt + i, 0)),
        out_shape=jax.ShapeDtypeStruct((m, 256), BF16),
        scratch_shapes=[pltpu.VMEM((256, 256), F32), pltpu.VMEM((2 * t, SUB * 128), F32)],
        compiler_params=_cp("parallel", "arbitrary"),
        name="hgrn2",
    )(z, z, z, z, z, lb_logits, jnp.tile(out_norm, 4).reshape(1, 256), bd_mean, bd_ones, selp)


def _gdn_kernel(q_ref, k_ref, v_ref, z_ref, ba_ref, cw_ref, al_ref, dt_ref, on_ref, bdm_ref, bdo_ref,
                eb_ref, ea_ref, o_ref, st_ref, prev_ref, *, t):
    @pl.when(pl.program_id(1) == 0)
    def _():
        st_ref[...] = jnp.zeros(st_ref.shape, F32)
        prev_ref[...] = jnp.zeros(prev_ref.shape, F32)

    def conv_silu(x_ref, p):
        x = x_ref[...].astype(F32)
        xp = jnp.concatenate([prev_ref[:, p * 256:(p + 1) * 256], x], axis=0)
        w = cw_ref[:, p * 256:(p + 1) * 256]
        y = w[3:4, :] * xp[8:, :]
        for j in range(1, 4):
            y = y + w[3 - j:4 - j, :] * pltpu.roll(xp, j, 0)[8:, :]
        return _silu(y), x[t - 8:, :]

    qa, qtail = conv_silu(q_ref, 0)
    ka, ktail = conv_silu(k_ref, 1)
    va, vtail = conv_silu(v_ref, 2)
    prev_ref[:, 0:256] = qtail
    prev_ref[:, 256:512] = ktail
    prev_ref[:, 512:768] = vtail

    bdo = bdo_ref[...]
    bdk = bdo > 0
    ss = _dot_ls(jnp.concatenate([qa * qa, ka * ka], axis=0), bdo, 2)
    q = qa * lax.rsqrt(ss[0:t] + EPS) * (HEAD_DIM ** -0.5)
    k = ka * lax.rsqrt(ss[t:2 * t] + EPS)
    v = va

    gl = ba_ref[...].astype(F32)
    beta = _dot_ls(_sigmoid(gl), eb_ref[...], 2)
    xs = gl + dt_ref[...]
    softplus = jnp.maximum(xs, 0.0) + jnp.log(1.0 + jnp.exp(-jnp.abs(xs)))
    b = _dot_ls(_chunk_cumsum(-jnp.exp(al_ref[...]) * softplus), ea_ref[...], 3)

    bdm = _bd_mask()
    trow = lax.broadcasted_iota(jnp.int32, (CHUNK, 256), 0)
    scol = lax.broadcasted_iota(jnp.int32, (CHUNK, 256), 1) % CHUNK
    incl = scol <= trow
    strict = scol < trow
    eye = jnp.where(scol == trow, 1.0, 0.0)

    chunks = [slice(c * CHUNK, (c + 1) * CHUNK) for c in range(t // CHUNK)]
    nch = len(chunks)
    aw, aq = [], []
    for rs in chunks:
        qc, kc, bc = q[rs], k[rs], b[rs]
        brow = jnp.concatenate([bc[:, h * 64:(h + 1) * 64].T for h in range(N_HEADS4)], axis=1)
        e = jnp.exp(jnp.where(incl, bc - brow, 0.0))
        gram = _dot_nt(jnp.concatenate([kc * beta[rs], qc], axis=0), _block_diag4(kc, bdk))
        aw.append(jnp.where(strict, gram[:CHUNK] * e, 0.0))
        aq.append(jnp.where(incl, gram[CHUNK:] * e, 0.0).astype(BF16))
    pw = [None] * nch
    tinv = [None] * nch
    for c in range(nch):
        bh, bl_ = _split(-aw[c], 2)
        r1 = _rows_dot([bh, bl_], _block_diag4(bh, bdk))
        pw[c] = r1[0] + r1[1] + jnp.dot(bh, _block_diag4(bl_, bdk), preferred_element_type=F32)
        tinv[c] = eye - aw[c]
    for c in range(nch):
        ph, pl_ = _split(pw[c], 2)
        th, tl_ = _split(tinv[c], 2)
        r1 = _rows_dot([ph, pl_, th, tl_], _block_diag4(ph, bdk))
        r2 = _rows_dot([ph, th], _block_diag4(pl_, bdk))
        pw[c] = r1[0] + r1[1] + r2[0]
        tinv[c] = tinv[c] + r1[2] + r1[3] + r2[1]
    for _ in range(3):
        for c in range(nch):
            ph = pw[c].astype(BF16)
            r1 = _rows_dot([ph, tinv[c].astype(BF16)], _block_diag4(ph, bdk))
            pw[c] = r1[0]
            tinv[c] = tinv[c] + r1[1]
    for c in range(nch):
        tinv[c] = tinv[c] + jnp.dot(tinv[c].astype(BF16), _block_diag4(pw[c], bdk), preferred_element_type=F32)
    u, w = [], []
    for c, rs in enumerate(chunks):
        t2 = _split(tinv[c], 2)
        kbe = k[rs] * beta[rs] * jnp.exp(b[rs])
        r1 = _rows_dot(t2, _block_diag4(v[rs] * beta[rs], bdk))
        r2 = _rows_dot(t2, _block_diag4(kbe, bdk))
        u.append(r1[0] + r1[1])
        w.append(r2[0] + r2[1])
    lhs, s_add, s_dec, o0 = [], [], [], []
    for c, rs in enumerate(chunks):
        bc = b[rs]
        bl = bc[CHUNK - 1:CHUNK, :]
        kdt = (k[rs] * jnp.exp(bl - bc)).T.astype(BF16)
        rn = jnp.dot(kdt, jnp.concatenate([w[c], u[c]], axis=1).astype(BF16), preferred_element_type=F32)
        qeff = q[rs] * jnp.exp(bc) - jnp.dot(aq[c], _block_diag4(w[c], bdk), preferred_element_type=F32)
        lhs.append(jnp.concatenate([qeff.astype(BF16), jnp.where(bdm, rn[:, 0:256], 0.0).astype(BF16)], axis=0))
        s_add.append(jnp.where(bdm, rn[:, 256:512], 0.0))
        s_dec.append(_as_column(jnp.exp(bl)))
        o0.append(jnp.dot(aq[c], _block_diag4(u[c], bdk), preferred_element_type=F32))

    st = st_ref[...]
    outs = []
    for c in range(nch):
        r = jnp.dot(lhs[c], st.astype(BF16), preferred_element_type=F32)
        outs.append(o0[c] + r[0:CHUNK])
        st = s_dec[c] * st - r[CHUNK:] + s_add[c]
    st_ref[...] = st
    o = jnp.concatenate(outs, axis=0)
    o_ref[...] = (_group_rmsnorm(o, bdm_ref[...], on_ref[...]) * _silu(z_ref[...].astype(F32))).astype(BF16)


def _gdn(z, conv_w, a_log, dt_bias, out_norm, b, s, t):
    m = b * s
    nt = s // t
    bd_mean, bd_ones = _consts_recurrent()
    eb = np.zeros((128, 256), np.float32)
    ea = np.zeros((128, 256), np.float32)
    for h in range(N_HEADS4):
        eb[h, h * 64:(h + 1) * 64] = 1.0
        ea[4 + h, h * 64:(h + 1) * 64] = 1.0
    al = jnp.zeros((1, 128), F32).at[0, 4:8].set(a_log)
    dt = jnp.zeros((1, 128), F32).at[0, 4:8].set(dt_bias)
    col = COL_GD // 256
    c2 = lambda bb, i: (0, 0)
    kern = functools.partial(_gdn_kernel, t=t)
    return pl.pallas_call(
        kern,
        grid=(b, nt),
        in_specs=[pl.BlockSpec((t, 256), lambda bb, i: (bb * nt + i, col)),
                  pl.BlockSpec((t, 256), lambda bb, i: (bb * nt + i, col + 1)),
                  pl.BlockSpec((t, 256), lambda bb, i: (bb * nt + i, col + 2)),
                  pl.BlockSpec((t, 256), lambda bb, i: (bb * nt + i, col + 3)),
                  pl.BlockSpec((t, 128), lambda bb, i: (bb * nt + i, COL_BA // 128)),
                  pl.BlockSpec((4, 768), c2),
                  pl.BlockSpec((1, 128), c2), pl.BlockSpec((1, 128), c2),
                  pl.BlockSpec((1, 256), c2),
                  pl.BlockSpec((256, 256), c2), pl.BlockSpec((256, 256), c2),
                  pl.BlockSpec((128, 256), c2), pl.BlockSpec((128, 256), c2)],
        out_specs=pl.BlockSpec((t, 256), lambda bb, i: (bb * nt + i, 0)),
        out_shape=jax.ShapeDtypeStruct((m, 256), BF16),
        scratch_shapes=[pltpu.VMEM((256, 256), F32), pltpu.VMEM((8, 768), F32)],
        compiler_params=_cp("parallel", "arbitrary"),
        name="gdn",
    )(z, z, z, z, z, conv_w, al, dt, jnp.tile(out_norm, 4).reshape(1, 256), bd_mean, bd_ones,
      jnp.asarray(eb, BF16), jnp.asarray(ea, BF16))


def _merge_xattn_kernel(x_ref, oc_ref, os_ref, ow_ref, ng_ref, yb_ref, yc_ref, ma_ref, mb_ref, mc_ref,
                        ex_ref, wa_ref, wb_ref, wc_ref, wmix_ref, g_ref, wq_ref, qn_ref, kn_ref, mkv_ref, wo_ref,
                        o_ref):
    sg = _sigmoid(ng_ref[...].astype(F32))
    ya = (_dot_ls(sg, ex_ref[0], 2) * oc_ref[...].astype(F32) + _dot_ls(sg, ex_ref[1], 2) * os_ref[...].astype(F32)
          + _dot_ls(sg, ex_ref[2], 2) * ow_ref[...].astype(F32))
    merged = (_sigmoid(ma_ref[...].astype(F32)) * _dot(ya, wa_ref[...])
              + _sigmoid(mb_ref[...].astype(F32)) * jnp.dot(yb_ref[...], wb_ref[...], preferred_element_type=F32)
              + _sigmoid(mc_ref[...].astype(F32)) * jnp.dot(yc_ref[...], wc_ref[...], preferred_element_type=F32))
    x = x_ref[...] + _dot(merged, wmix_ref[...])

    h = x * lax.rsqrt(jnp.mean(x * x, axis=-1, keepdims=True) + EPS) * g_ref[...]
    q = _dot(h, wq_ref[...])
    scores = []
    for hd in range(N_HEADS4):
        ls = slice(hd * 128, (hd + 1) * 128)
        qh = q[:, ls]
        qh = qh * lax.rsqrt(jnp.mean(qh * qh, axis=-1, keepdims=True) + EPS) * qn_ref[...]
        kh = mkv_ref[:, ls]
        kh = kh * lax.rsqrt(jnp.mean(kh * kh, axis=-1, keepdims=True) + EPS) * kn_ref[...]
        scores.append(_dot_nt(qh, kh) * (XATTN_HEAD_DIM ** -0.5))
    outs = []
    for hd, s in enumerate(scores):
        e = jnp.exp(s - jnp.max(s, axis=-1, keepdims=True))
        pv = _dot(e, mkv_ref[:, 512 + hd * 128:512 + (hd + 1) * 128])
        outs.append(pv / jnp.sum(e, axis=-1, keepdims=True))
    o = jnp.concatenate(outs, axis=1)
    o_ref[...] = x + _dot(o, wo_ref[...])


def _merge_xattn(x2, o_cmp, o_sel, o_win, z, yb, yc, wa, wb, wc, wmix, g, wq, qn, kn, mkv, wo, layer, s, tm):
    m, d = x2.shape
    wl = lambda i: (layer, 0, 0)
    nt = s // tm
    ex = np.zeros((3, 128, 512), np.float32)
    for c in range(3):
        for h in range(NSA_HEADS):
            ex[c, c * NSA_HEADS + h, h * 64:(h + 1) * 64] = 1.0
    row = lambda i: (i, 0)
    c2 = lambda i: (0, 0)
    once = dict(pipeline_mode=pl.Buffered(1))
    return pl.pallas_call(
        _merge_xattn_kernel,
        grid=(m // tm,),
        in_specs=[pl.BlockSpec((tm, d), row),
                  pl.BlockSpec((tm, 512), row), pl.BlockSpec((tm, 512), row), pl.BlockSpec((tm, 512), row),
                  pl.BlockSpec((tm, 128), lambda i: (i, COL_NG // 128)),
                  pl.BlockSpec((tm, 256), row), pl.BlockSpec((tm, 256), row),
                  pl.BlockSpec((tm, 1024), lambda i: (i, 0)),
                  pl.BlockSpec((tm, 1024), lambda i: (i, 1)),
                  pl.BlockSpec((tm, 1024), lambda i: (i, 2)),
                  pl.BlockSpec((3, 128, 512), lambda i: (0, 0, 0)),
                  pl.BlockSpec((None, 512, d), wl, **once), pl.BlockSpec((None, 256, d), wl, **once),
                  pl.BlockSpec((None, 256, d), wl, **once),
                  pl.BlockSpec((None, d, d), wl, **once),
                  pl.BlockSpec((1, d), c2),
                  pl.BlockSpec((None, d, 512), wl, **once),
                  pl.BlockSpec((1, 128), c2), pl.BlockSpec((1, 128), c2),
                  pl.BlockSpec((N_MEM, 1024), lambda i: (i // nt, 0)),
                  pl.BlockSpec((None, 512, d), wl, **once)],
        out_specs=pl.BlockSpec((tm, d), row),
        out_shape=jax.ShapeDtypeStruct((m, d), F32),
        compiler_params=_cp("parallel"),
        name="merge_xattn",
    )(x2, o_cmp, o_sel, o_win, z, yb, yc, z, z, z, jnp.asarray(ex, BF16), wa, wb, wc, wmix,
      g.reshape(1, d), wq, qn.reshape(1, 128), kn.reshape(1, 128), mkv, wo)


FFN_HALO = 16


def _ffn_kernel(x_ref, xh_ref, g_ref, wua_ref, wub_ref, cwa_ref, cwb_ref, wd_ref, o_ref, h_ref, acc_ref, *, nt, tm):
    i = pl.program_id(0)
    f = pl.program_id(1)

    @pl.when(f == 0)
    def _():
        def norm(x):
            return x * lax.rsqrt(jnp.mean(x * x, axis=-1, keepdims=True) + EPS) * g_ref[...]
        first = (i % nt) == 0
        h_ref[0:FFN_HALO, :] = jnp.where(first, 0.0, norm(xh_ref[...])).astype(BF16)
        h_ref[FFN_HALO:, :] = norm(x_ref[...]).astype(BF16)
        acc_ref[...] = jnp.zeros(acc_ref.shape, F32)

    h = h_ref[...]

    def up_conv(w_ref, cw_ref):
        u = jnp.dot(h, w_ref[...], preferred_element_type=F32)
        cw = cw_ref[...]
        y = cw[2:3, :] * u + cw[1:2, :] * pltpu.roll(u, 1, 0) + cw[0:1, :] * pltpu.roll(u, 2, 0)
        return y[FFN_HALO:, :]

    a = up_conv(wua_ref, cwa_ref)
    bb = up_conv(wub_ref, cwb_ref)
    acc_ref[...] += _dot(_silu(a) * bb, wd_ref[...])

    @pl.when(f == pl.num_programs(1) - 1)
    def _():
        o_ref[...] = x_ref[...] + acc_ref[...]


def _ffn(x2, g, w_up, conv_w, w_down, layer, s, tm, tf):
    m, d = x2.shape
    nt = s // tm
    nf = D_FF // tf
    hb = tm // FFN_HALO
    kern = functools.partial(_ffn_kernel, nt=nt, tm=tm)
    return pl.pallas_call(
        kern,
        grid=(m // tm, nf),
        in_specs=[pl.BlockSpec((tm, d), lambda i, f: (i, 0)),
                  pl.BlockSpec((FFN_HALO, d), lambda i, f: (jnp.maximum(i * hb - 1, 0), 0)),
                  pl.BlockSpec((1, d), lambda i, f: (0, 0)),
                  pl.BlockSpec((None, d, tf), lambda i, f: (layer, 0, f)),
                  pl.BlockSpec((None, d, tf), lambda i, f: (layer, 0, nf + f)),
                  pl.BlockSpec((None, 3, tf), lambda i, f: (layer, 0, f)),
                  pl.BlockSpec((None, 3, tf), lambda i, f: (layer, 0, nf + f)),
                  pl.BlockSpec((None, tf, d), lambda i, f: (layer, f, 0))],
        out_specs=pl.BlockSpec((tm, d), lambda i, f: (i, 0)),
        out_shape=jax.ShapeDtypeStruct((m, d), F32),
        scratch_shapes=[pltpu.VMEM((tm + FFN_HALO, d), BF16), pltpu.VMEM((tm, d), F32)],
        compiler_params=_cp("parallel", "arbitrary"),
        name="conv_glu_ffn",
    )(x2, x2, g.reshape(1, d), w_up, w_up, conv_w, conv_w, w_down)


def _permute_w_in(w):
    nl, d = w.shape[0], w.shape[1]
    gate = w[:, :, 1280:1304].reshape(nl, d, NSA_HEADS, 3).transpose(0, 1, 3, 2).reshape(nl, d, 24)
    pieces = [(COL_M, w[:, :, 3360:6432]), (COL_NQ, w[:, :, 0:1280]), (COL_NG, gate), (COL_BA, w[:, :, 3352:3360]),
              (COL_HG, w[:, :, 1304:1560]), (COL_HG + 256, w[:, :, 1816:2328]), (COL_GD, w[:, :, 2328:3352]),
              (COL_HF, w[:, :, 1560:1816])]
    out = jnp.zeros((nl, d, IN_WIDTH_P), BF16)
    for dst, src in pieces:
        out = lax.dynamic_update_slice(out, src.astype(BF16), (0, 0, dst))
    return out


def _rope_tables(s):
    inv_freq = (1.0 / (ROPE_THETA ** (np.arange(0, HEAD_DIM, 2, dtype=np.float32) / HEAD_DIM))).astype(np.float32)
    ang = np.arange(s, dtype=np.float32)[:, None] * inv_freq[None, :]
    c, sn = np.cos(ang), np.sin(ang)
    return (jnp.asarray(np.tile(np.concatenate([c, c], axis=1), (1, 2)), F32),
            jnp.asarray(np.tile(np.concatenate([-sn, sn], axis=1), (1, 2)), F32))


def kernel(x, mem, mem_norm, mem_w_kv, hgrn_lb_logits, norm_mix, w_in, nsa_q_norm, nsa_k_norm, cmp_pos_k, cmp_pos_v, cmp_k_w1, cmp_k_w2, cmp_v_w1, cmp_v_w2, hgrn_out_norm, gdn_conv, gdn_a_log, gdn_dt_bias, gdn_out_norm, w_branch_a, w_branch_b, w_branch_c, w_mix_out, norm_cross, xattn_wq, xattn_q_norm, xattn_k_norm, xattn_wo, norm_ffn, ffn_w_up, ffn_conv, ffn_w_down):
    b, s, d = x.shape
    m = b * s
    depth = w_in.shape[0]
    cos_t, sin_t = _rope_tables(s)
    x2 = x.reshape(m, d)
    mkv = _norm_matmul(mem.reshape(b * N_MEM, d), mem_norm, mem_w_kv.astype(BF16)[None], 0, N_MEM, 512)

    tm_in = min(1024, m)
    w_in_p = _permute_w_in(w_in)
    w_up, w_down = ffn_w_up.astype(BF16), ffn_w_down.astype(BF16)
    w_br_a, w_br_b, w_br_c = w_branch_a.astype(BF16), w_branch_b.astype(BF16), w_branch_c.astype(BF16)
    w_mix, w_xq, w_xo = w_mix_out.astype(BF16), xattn_wq.astype(BF16), xattn_wo.astype(BF16)
    for l in range(depth):
        z = _norm_matmul(x2, norm_mix[l], w_in_p, l, tm_in, IN_WIDTH_P // 3, BF16, split_last=256)
        q_r, rk, rv, kst, kwt, vs, vw = _nsa_prep(z, cos_t, sin_t, nsa_q_norm[l], nsa_k_norm[l], b, s, 512)
        kct, vc = _compress(rk, rv, cmp_pos_k[l], cmp_pos_v[l], cmp_k_w1[l], cmp_k_w2[l], cmp_v_w1[l], cmp_v_w2[l])
        o_cmp, sel = _cmp_attn(q_r, kct, vc, b, s, 256)
        o_sel = _sel_attn(q_r, kst, vs, sel, b, s, 512, 512)
        o_win = _win_attn(q_r, kwt, vw, b, s, 256)
        yb = _hgrn(z, hgrn_lb_logits, hgrn_out_norm[l], l, b, s, 512)
        yc = _gdn(z, gdn_conv[l], gdn_a_log[l], gdn_dt_bias[l], gdn_out_norm[l], b, s, 512)
        x2 = _merge_xattn(x2, o_cmp, o_sel, o_win, z, yb, yc, w_br_a, w_br_b, w_br_c, w_mix,
                          norm_cross[l], w_xq, xattn_q_norm[l], xattn_k_norm[l], mkv, w_xo, l, s, 512)
        x2 = _ffn(x2, norm_ffn[l], w_up, ffn_conv, w_down, l, s, 512, 1408)
    return x2.reshape(b, s, d)
```

```python
import functools
import math

import numpy as np
import jax
import jax.numpy as jnp
from jax import lax
from jax.experimental import pallas as pl
from jax.experimental.pallas import tpu as pltpu

F32 = jnp.float32
BF16 = jnp.bfloat16

EPS = 1e-6
ROPE_THETA = 10000.0
NEG_BIG = -1e30
TINY = 1e-20
FORCE_SCORE = 1e6
SEL_MASK = 2.0 ** 30

N_MEM = 256
HEAD_DIM = 64
NSA_HEADS = 8
NSA_GROUPS = 2
NSA_HPG = 4
CMP_BLOCK = 32
CMP_STRIDE = 16
CMP_HIDDEN = 128
SEL_BLOCK = 64
SEL_TOPK = 16
WINDOW = 512
N_HEADS4 = 4
CHUNK = 64
SUB = 16
XATTN_HEAD_DIM = 128
D_FF = 2816

COL_M = 0
COL_NQ = 3072
COL_KV = 3584
COL_NG = 4352
COL_BA = 4480
COL_HG = 4608
COL_GD = 5376
COL_HF = 6400
IN_WIDTH_P = 6912

VMEM_LIMIT = 48 * 1024 * 1024


def _cp(*sem):
    return pltpu.CompilerParams(dimension_semantics=sem, vmem_limit_bytes=VMEM_LIMIT)


def _dot(a, b):
    return jnp.dot(a.astype(BF16), b.astype(BF16), preferred_element_type=F32)


def _dot_nt(a, b):
    return lax.dot_general(a.astype(BF16), b.astype(BF16), (((1,), (1,)), ((), ())),
                           preferred_element_type=F32)


def _split(a, n):
    parts = []
    r = a
    for _ in range(n):
        p = r.astype(BF16)
        parts.append(p)
        r = r - p.astype(F32)
    return parts


def _rows_dot(blocks, rhs):
    if len(blocks) == 1:
        return [jnp.dot(blocks[0], rhs, preferred_element_type=F32)]
    r = jnp.dot(jnp.concatenate(blocks, axis=0), rhs, preferred_element_type=F32)
    out, o = [], 0
    for blk in blocks:
        out.append(r[o:o + blk.shape[0]])
        o += blk.shape[0]
    return out


def _dot_ls(a, b_exact, n=2):
    parts = _rows_dot(_split(a, n), b_exact)
    acc = parts[0]
    for p in parts[1:]:
        acc = acc + p
    return acc


def _chunk_cumsum(x):
    row = lax.broadcasted_iota(jnp.int32, (x.shape[0], 1), 0) % CHUNK
    sh = 1
    while sh < CHUNK:
        x = x + jnp.where(row >= sh, pltpu.roll(x, sh, 0), 0.0)
        sh *= 2
    return x


def _as_column(row):
    return jnp.broadcast_to(row, (8, row.shape[1])).T[:, 0:1]


def _sigmoid(x):
    return 1.0 / (1.0 + jnp.exp(-x))


def _silu(x):
    return x * _sigmoid(x)


def _block_diag4(y, keep):
    yb = y.astype(BF16)
    return jnp.where(keep, jnp.concatenate([yb, yb, yb, yb], axis=0), jnp.zeros((), BF16))


def _bd_mask():
    r = lax.broadcasted_iota(jnp.int32, (256, 256), 0) // 64
    c = lax.broadcasted_iota(jnp.int32, (256, 256), 1) // 64
    return r == c


def _norm_matmul_kernel(x_ref, g_ref, w_ref, o_ref, h_ref, *, split_last):
    j = pl.program_id(1)

    @pl.when(j == 0)
    def _():
        x = x_ref[...]
        ms = jnp.mean(x * x, axis=-1, keepdims=True)
        h_ref[...] = (x * lax.rsqrt(ms + EPS) * g_ref[...]).astype(BF16)

    r = jnp.dot(h_ref[...], w_ref[...], preferred_element_type=F32)
    if not split_last:
        o_ref[...] = r.astype(o_ref.dtype)
    else:
        last = pl.num_programs(1) - 1
        n, k = r.shape[1], split_last

        @pl.when(j != last)
        def _():
            o_ref[...] = r.astype(o_ref.dtype)

        @pl.when(j == last)
        def _():
            o_ref[:, 0:n - 2 * k] = r[:, 0:n - 2 * k].astype(o_ref.dtype)
            lead = r[:, n - 2 * k:n - k].astype(BF16)
            o_ref[:, n - 2 * k:n - k] = lead
            o_ref[:, n - k:n] = (r[:, n - 2 * k:n - k] - lead.astype(F32)).astype(BF16)


def _norm_matmul(x, g, w, layer, tm, tn, out_dtype=F32, split_last=0):
    m, d = x.shape
    n = w.shape[2]
    return pl.pallas_call(
        functools.partial(_norm_matmul_kernel, split_last=split_last),
        grid=(m // tm, n // tn),
        in_specs=[pl.BlockSpec((tm, d), lambda i, j: (i, 0)),
                  pl.BlockSpec((1, d), lambda i, j: (0, 0)),
                  pl.BlockSpec((None, d, tn), lambda i, j: (layer, 0, j))],
        out_specs=pl.BlockSpec((tm, tn), lambda i, j: (i, j)),
        out_shape=jax.ShapeDtypeStruct((m, n), out_dtype),
        scratch_shapes=[pltpu.VMEM((tm, d), BF16)],
        compiler_params=_cp("parallel", "arbitrary"),
        name="norm_matmul",
    )(x, g.reshape(1, d), w)


def _head_norm(x, bd, g):
    ms = _dot_ls(x * x, bd, 2)
    return x * lax.rsqrt(ms + EPS) * g


def _rope(x, c, s):
    w = x.shape[1]
    lane = lax.broadcasted_iota(jnp.int32, x.shape, 1)
    sw = jnp.where((lane & 32) != 0, pltpu.roll(x, 32, 1), pltpu.roll(x, w - 32, 1))
    return x * c + sw * s


def _nsa_prep_kernel(q_ref, kvc_ref, kvs_ref, kvw_ref, cos_ref, sin_ref, qn_ref, kn_ref, bd512_ref, bd128_ref,
                     qo_ref, kc_ref, vc_ref, kst_ref, kwt_ref, vs_ref, vw_ref, tok_ref):
    c = cos_ref[...]
    s = sin_ref[...]
    c4 = jnp.concatenate([c, c, c, c], axis=1)
    s4 = jnp.concatenate([s, s, s, s], axis=1)
    q = _rope(_head_norm(q_ref[...].astype(F32), bd512_ref[...], qn_ref[...]), c4, s4)
    qo_ref[...] = (q * (HEAD_DIM ** -0.5)).astype(BF16)

    def key(ref, row):
        return _rope(_head_norm(ref[:, 0:128].astype(F32), bd128_ref[...], kn_ref[row:row + 1, :]), c, s)

    tok_ref[0] = key(kvc_ref, 0)
    tok_ref[1] = kvc_ref[:, 128:256].astype(F32)
    n16 = tok_ref.shape[1] // CMP_STRIDE
    for r in range(CMP_STRIDE):
        kc_ref[:, r * 128:(r + 1) * 128] = tok_ref[0, pl.ds(r, n16, stride=CMP_STRIDE), :]
        vc_ref[:, r * 128:(r + 1) * 128] = tok_ref[1, pl.ds(r, n16, stride=CMP_STRIDE), :]
    kst_ref[...] = key(kvs_ref, 1).T.astype(BF16)
    kwt_ref[...] = key(kvw_ref, 2).T.astype(BF16)
    vs = kvs_ref[:, 128:256].astype(BF16)
    vw = kvw_ref[:, 128:256].astype(BF16)
    ones = jnp.ones((vs.shape[0], 64), BF16)
    for g in range(NSA_GROUPS):
        vs_ref[g] = jnp.concatenate([vs[:, g * 64:(g + 1) * 64], ones], axis=1)
        vw_ref[g] = jnp.concatenate([vw[:, g * 64:(g + 1) * 64], ones], axis=1)


def _nsa_prep(z, cos_t, sin_t, q_norm, k_norm, b, s, tm):
    m = b * s
    nt = s // tm
    bd512 = jnp.asarray(np.kron(np.eye(8), np.full((64, 64), 1.0 / 64)), BF16)
    bd128 = jnp.asarray(np.kron(np.eye(2), np.full((64, 64), 1.0 / 64)), BF16)
    qn = jnp.tile(q_norm, 8).reshape(1, 512)
    kn = jnp.tile(k_norm, (1, 2))
    row = lambda i: (i, 0)
    const = lambda i: (0, 0)
    return pl.pallas_call(
        _nsa_prep_kernel,
        grid=(m // tm,),
        in_specs=[pl.BlockSpec((tm, 512), lambda i: (i, COL_NQ // 512)),
                  pl.BlockSpec((tm, 256), lambda i: (i, COL_KV // 256)),
                  pl.BlockSpec((tm, 256), lambda i: (i, COL_KV // 256 + 1)),
                  pl.BlockSpec((tm, 256), lambda i: (i, COL_KV // 256 + 2)),
                  pl.BlockSpec((tm, 128), lambda i: (i % nt, 0)),
                  pl.BlockSpec((tm, 128), lambda i: (i % nt, 0)),
                  pl.BlockSpec((1, 512), const),
                  pl.BlockSpec((3, 128), const),
                  pl.BlockSpec((512, 512), const),
                  pl.BlockSpec((128, 128), const)],
        out_specs=[pl.BlockSpec((tm, 512), row),
                   pl.BlockSpec((None, tm // CMP_STRIDE, CMP_STRIDE * 128), lambda i: (i // nt, i % nt, 0)),
                   pl.BlockSpec((None, tm // CMP_STRIDE, CMP_STRIDE * 128), lambda i: (i // nt, i % nt, 0)),
                   pl.BlockSpec((None, 128, tm), lambda i: (i // nt, 0, i % nt)),
                   pl.BlockSpec((None, 128, tm), lambda i: (i // nt, 0, i % nt)),
                   pl.BlockSpec((NSA_GROUPS, tm, 128), lambda i: (0, i, 0)),
                   pl.BlockSpec((NSA_GROUPS, tm, 128), lambda i: (0, i, 0))],
        out_shape=[jax.ShapeDtypeStruct((m, 512), BF16),
                   jax.ShapeDtypeStruct((b, s // CMP_STRIDE, CMP_STRIDE * 128), F32),
                   jax.ShapeDtypeStruct((b, s // CMP_STRIDE, CMP_STRIDE * 128), F32),
                   jax.ShapeDtypeStruct((b, 128, s), BF16),
                   jax.ShapeDtypeStruct((b, 128, s), BF16),
                   jax.ShapeDtypeStruct((NSA_GROUPS, m, 128), BF16),
                   jax.ShapeDtypeStruct((NSA_GROUPS, m, 128), BF16)],
        scratch_shapes=[pltpu.VMEM((2, tm, 128), F32)],
        compiler_params=_cp("parallel"),
        name="nsa_prep",
    )(z, z, z, z, cos_t, sin_t, qn, kn, bd512, bd128)


def _gelu_tanh(x):
    return 0.5 * x * (1.0 + jnp.tanh(math.sqrt(2.0 / math.pi) * (x + 0.044715 * (x * x * x))))


def _compress_kernel(rk_ref, rv_ref, pk_ref, pv_ref, wkt_ref, wkb_ref, wvt_ref, wvb_ref, wk2_ref, wv2_ref,
                     kct_ref, vc_ref):
    def mlp(r_ref, p_ref, wt_ref, wb_ref, w2_ref):
        r = r_ref[...]
        n = r.shape[0]
        top = _dot(r + p_ref[0:1, :], wt_ref[...])
        bot = _dot(r + p_ref[1:2, :], wb_ref[...])
        hid = top + pltpu.roll(bot, n - 1, 0)
        return _dot(_gelu_tanh(hid), w2_ref[...])

    kc = mlp(rk_ref, pk_ref, wkt_ref, wkb_ref, wk2_ref)
    vc = mlp(rv_ref, pv_ref, wvt_ref, wvb_ref, wv2_ref)
    kct_ref[...] = kc.T
    for g in range(NSA_GROUPS):
        vc_ref[g] = vc[:, g * 64:(g + 1) * 64]


def _compress_weights(pos, w1, w2):
    w1r = w1.reshape(2, 16, 64, CMP_HIDDEN)
    zero = jnp.zeros_like(w1r)
    big = jnp.stack([jnp.stack([w1r, zero], axis=-2), jnp.stack([zero, w1r], axis=-2)], axis=2)
    big = big.reshape(2, 16 * 2 * 64, 2 * CMP_HIDDEN).astype(BF16)
    posr = jnp.broadcast_to(pos.reshape(2, 16, 1, 64), (2, 16, 2, 64)).reshape(2, 2048)
    w2bd = jnp.zeros((2, CMP_HIDDEN, 2, 64), F32)
    w2bd = w2bd.at[0, :, 0, :].set(w2).at[1, :, 1, :].set(w2).reshape(2 * CMP_HIDDEN, 128).astype(BF16)
    return posr, big[0], big[1], w2bd


def _compress(rk, rv, pos_k, pos_v, ck_w1, ck_w2, cv_w1, cv_w2):
    b, n, _ = rk.shape
    pk, wkt, wkb, wk2 = _compress_weights(pos_k, ck_w1, ck_w2)
    pv, wvt, wvb, wv2 = _compress_weights(pos_v, cv_w1, cv_w2)
    c2 = lambda i: (0, 0)
    return pl.pallas_call(
        _compress_kernel,
        grid=(b,),
        in_specs=[pl.BlockSpec((None, n, 2048), lambda i: (i, 0, 0)),
                  pl.BlockSpec((None, n, 2048), lambda i: (i, 0, 0)),
                  pl.BlockSpec((2, 2048), c2), pl.BlockSpec((2, 2048), c2),
                  pl.BlockSpec((2048, 256), c2), pl.BlockSpec((2048, 256), c2),
                  pl.BlockSpec((2048, 256), c2), pl.BlockSpec((2048, 256), c2),
                  pl.BlockSpec((256, 128), c2), pl.BlockSpec((256, 128), c2)],
        out_specs=[pl.BlockSpec((None, 128, n), lambda i: (i, 0, 0)),
                   pl.BlockSpec((None, NSA_GROUPS, n, 64), lambda i: (i, 0, 0, 0))],
        out_shape=[jax.ShapeDtypeStruct((b, 128, n), F32),
                   jax.ShapeDtypeStruct((b, NSA_GROUPS, n, 64), F32)],
        compiler_params=_cp("parallel"),
        name="nsa_compress",
    )(rk, rv, pk, pv, wkt, wkb, wvt, wvb, wk2, wv2)


def _cmp_attn_kernel(q_ref, kct_ref, vc_ref, ovt_ref, o_ref, sel_ref, imp_ref, *, tq, ncp, nsel):
    i = pl.program_id(1)
    tpos = i * tq + lax.broadcasted_iota(jnp.int32, (tq, 1), 0)
    nblk = lax.broadcasted_iota(jnp.int32, (1, ncp), 1)
    bias = jnp.where((nblk * CMP_STRIDE + (CMP_BLOCK - 1)) <= tpos, 0.0, NEG_BIG)
    any_valid = jnp.where(tpos >= CMP_BLOCK - 1, 1.0, 0.0)
    j = lax.broadcasted_iota(jnp.int32, (nsel, 1), 0)
    cur = (i * tq + lax.broadcasted_iota(jnp.int32, (1, tq), 1)) // SEL_BLOCK
    forced = (j == 0) | (j == cur) | (j == cur - 1)
    jl = lax.broadcasted_iota(jnp.int32, (8, 1), 0)
    ovt = ovt_ref[...]

    scores = []
    for g in range(NSA_GROUPS):
        q4 = jnp.concatenate([q_ref[:, (g * NSA_HPG + h) * 64:(g * NSA_HPG + h + 1) * 64]
                              for h in range(NSA_HPG)], axis=0)
        scores.append(jnp.dot(q4, kct_ref[g * 64:(g + 1) * 64, :].astype(BF16), preferred_element_type=F32))
    psums = []
    for g, s in enumerate(scores):
        s = s.reshape(NSA_HPG, tq, ncp) + bias[None]
        e = jnp.exp(s - jnp.max(s, axis=-1, keepdims=True))
        p = e * (any_valid / jnp.maximum(jnp.sum(e, axis=-1, keepdims=True), 1e-30))
        o = jnp.dot(p.reshape(NSA_HPG * tq, ncp).astype(BF16), vc_ref[g].astype(BF16), preferred_element_type=F32)
        for h in range(NSA_HPG):
            c = (g * NSA_HPG + h) * 64
            o_ref[:, c:c + 64] = o[h * tq:(h + 1) * tq].astype(o_ref.dtype)
        psums.append(p[0] + p[1] + p[2] + p[3])
    for g, psum in enumerate(psums):
        ph, plo = _split(psum, 2)
        imp = _dot_nt(ovt, ph) + _dot_nt(ovt, plo)
        imp_ref[g] = jnp.where(j <= cur, jnp.where(forced, FORCE_SCORE, imp), -1.0)
    for g in range(NSA_GROUPS):
        val = imp_ref[g]
        rows8 = [val[r8 * 8:(r8 + 1) * 8] for r8 in range(nsel // 8)]
        cnt = [jnp.zeros((8, tq), F32) for _ in rows8]
        for r in range(nsel):
            vr = imp_ref[g, r:r + 1, :]
            for r8, vg in enumerate(rows8):
                if r8 * 8 > r:
                    beats = vr >= vg
                elif r8 * 8 + 7 < r:
                    beats = vr > vg
                else:
                    beats = (vr > vg) | ((vr == vg) & (jl + r8 * 8 > r))
                cnt[r8] = cnt[r8] + jnp.where(beats, 1.0, 0.0)
        sel = jnp.where(jnp.concatenate(cnt, axis=0) < float(min(SEL_TOPK, nsel)), 1.0, 0.0)
        sel_ref[g] = sel.T.astype(BF16)


def _cmp_attn(q_r, kct, vc, b, s, tq):
    m = b * s
    nq = s // tq
    ncp = kct.shape[2]
    nsel = s // SEL_BLOCK
    n_cmp = (s - CMP_BLOCK) // CMP_STRIDE + 1
    tok = np.arange(n_cmp)[:, None] * CMP_STRIDE + np.arange(CMP_BLOCK)[None, :]
    ov = np.zeros((ncp, nsel), np.float32)
    for l in range(CMP_BLOCK):
        ov[np.arange(n_cmp), tok[:, l] // SEL_BLOCK] += 1.0
    ovt = jnp.asarray(ov.T, BF16)
    kern = functools.partial(_cmp_attn_kernel, tq=tq, ncp=ncp, nsel=nsel)
    return pl.pallas_call(
        kern,
        grid=(b, nq),
        in_specs=[pl.BlockSpec((tq, 512), lambda bb, i: (bb * nq + i, 0)),
                  pl.BlockSpec((None, 128, ncp), lambda bb, i: (bb, 0, 0)),
                  pl.BlockSpec((None, NSA_GROUPS, ncp, 64), lambda bb, i: (bb, 0, 0, 0)),
                  pl.BlockSpec((nsel, ncp), lambda bb, i: (0, 0))],
        out_specs=[pl.BlockSpec((tq, 512), lambda bb, i: (bb * nq + i, 0)),
                   pl.BlockSpec((NSA_GROUPS, tq, nsel), lambda bb, i: (0, bb * nq + i, 0))],
        out_shape=[jax.ShapeDtypeStruct((m, 512), BF16),
                   jax.ShapeDtypeStruct((NSA_GROUPS, m, nsel), BF16)],
        scratch_shapes=[pltpu.VMEM((NSA_GROUPS, nsel, tq), F32)],
        compiler_params=_cp("parallel", "parallel"),
        name="nsa_cmp_attn",
    )(q_r, kct, vc, ovt)


def _sel_attn_kernel(qi_ref, kt_ref, first_ref, last_ref, q_ref, k_ref, v_ref, sel_ref, o_ref, q4_ref, m_ref, acc_ref,
                     *, tq, tk, nsel):
    n = pl.program_id(1)
    qi = qi_ref[n]
    ktile = kt_ref[n]
    rows = NSA_HPG * tq
    groups = [slice(g * rows, (g + 1) * rows) for g in range(NSA_GROUPS)]

    @pl.when(first_ref[n] == 1)
    def _():
        for g in range(NSA_GROUPS):
            unsel = ((sel_ref[g].astype(F32) - 1.0) * SEL_MASK).astype(BF16)
            for h in range(NSA_HPG):
                r0 = g * rows + h * tq
                q4_ref[r0:r0 + tq, 0:64] = q_ref[:, (g * NSA_HPG + h) * 64:(g * NSA_HPG + h + 1) * 64]
                q4_ref[r0:r0 + tq, 64:64 + nsel] = unsel
        m_ref[...] = jnp.full(m_ref.shape, NEG_BIG, F32)
        acc_ref[...] = jnp.zeros(acc_ref.shape, F32)

    def step(with_causal):
        kpos = ktile * tk + lax.broadcasted_iota(jnp.int32, (1, tk), 1)
        blk = lax.broadcasted_iota(jnp.int32, (nsel, 1), 0)
        own_block = jnp.where(blk == kpos // SEL_BLOCK, 1.0, 0.0).astype(BF16)
        scores = [jnp.dot(q4_ref[rs, :], jnp.concatenate([k_ref[g * 64:(g + 1) * 64, :], own_block], axis=0),
                          preferred_element_type=F32) for g, rs in enumerate(groups)]
        if with_causal:
            qpos = qi * tq + lax.broadcasted_iota(jnp.int32, (tq, 1), 0)
            causal = jnp.where(kpos <= qpos, 0.0, NEG_BIG)
        for g, (rs, s) in enumerate(zip(groups, scores)):
            if with_causal:
                s = (s.reshape(NSA_HPG, tq, tk) + causal[None]).reshape(rows, tk)
            m_prev = m_ref[rs, :]
            m_next = jnp.maximum(m_prev, jnp.max(s, axis=-1, keepdims=True))
            p = jnp.exp(s - jnp.tile(m_next, (1, tk // 128)))
            pv = jnp.dot(p.astype(BF16), v_ref[g], preferred_element_type=F32)
            acc_ref[rs, :] = jnp.exp(m_prev - m_next) * acc_ref[rs, :] + pv
            m_ref[rs, :] = m_next

    reaches_past = ktile * tk + tk - 1 > qi * tq
    pl.when(reaches_past)(functools.partial(step, True))
    pl.when(jnp.logical_not(reaches_past))(functools.partial(step, False))

    @pl.when(last_ref[n] == 1)
    def _():
        a = acc_ref[...]
        o = a / pltpu.roll(a, 64, 1)
        for hh in range(NSA_HEADS):
            o_ref[:, hh * 64:(hh + 1) * 64] = o[hh * tq:(hh + 1) * tq, 0:64].astype(o_ref.dtype)


def _sel_attn(q_r, kt, v, sel, b, s, tq, tk):
    m = b * s
    nq = s // tq
    nkb = s // tk
    nsel = s // SEL_BLOCK
    pairs = []
    for qi in range(nq):
        kts = [k for k in range(nkb) if k * tk <= qi * tq + tq - 1]
        pairs += [(qi, k, int(j == 0), int(j == len(kts) - 1)) for j, k in enumerate(kts)]
    tabs = [jnp.asarray(np.array([pr[c] for pr in pairs], np.int32)) for c in range(4)]
    kern = functools.partial(_sel_attn_kernel, tq=tq, tk=tk, nsel=nsel)
    return pl.pallas_call(
        kern,
        grid_spec=pltpu.PrefetchScalarGridSpec(
            num_scalar_prefetch=4,
            grid=(b, len(pairs)),
            in_specs=[pl.BlockSpec((tq, 512), lambda bb, n, qt, kt_, f, l: (bb * nq + qt[n], 0)),
                      pl.BlockSpec((None, 128, tk), lambda bb, n, qt, kt_, f, l: (bb, 0, kt_[n])),
                      pl.BlockSpec((NSA_GROUPS, tk, 128), lambda bb, n, qt, kt_, f, l: (0, bb * nkb + kt_[n], 0)),
                      pl.BlockSpec((NSA_GROUPS, tq, nsel), lambda bb, n, qt, kt_, f, l: (0, bb * nq + qt[n], 0))],
            out_specs=pl.BlockSpec((tq, 512), lambda bb, n, qt, kt_, f, l: (bb * nq + qt[n], 0)),
            scratch_shapes=[pltpu.VMEM((NSA_HEADS * tq, 64 + nsel), BF16), pltpu.VMEM((NSA_HEADS * tq, 128), F32),
                            pltpu.VMEM((NSA_HEADS * tq, 128), F32)]),
        out_shape=jax.ShapeDtypeStruct((m, 512), BF16),
        compiler_params=_cp("parallel", "arbitrary"),
        name="nsa_sel_attn",
    )(*tabs, q_r, kt, v, sel)


def _win_attn_kernel(q_ref, *refs, tq, nblk):
    k_refs, v_refs, o_ref = refs[:nblk], refs[nblk:2 * nblk], refs[2 * nblk]
    qi = pl.program_id(1)
    kpos = (qi - (nblk - 1)) * tq + lax.broadcasted_iota(jnp.int32, (1, nblk * tq), 1)
    dist = qi * tq + lax.broadcasted_iota(jnp.int32, (tq, 1), 0) - kpos
    bias = jnp.where((dist >= 0) & (dist < WINDOW) & (kpos >= 0), 0.0, NEG_BIG)
    scores = []
    for g in range(NSA_GROUPS):
        q4 = jnp.concatenate([q_ref[:, (g * NSA_HPG + h) * 64:(g * NSA_HPG + h + 1) * 64]
                              for h in range(NSA_HPG)], axis=0)
        kt = jnp.concatenate([r[g * 64:(g + 1) * 64, :] for r in k_refs], axis=1)
        scores.append(jnp.dot(q4, kt, preferred_element_type=F32))
    for g, s in enumerate(scores):
        s = (s.reshape(NSA_HPG, tq, nblk * tq) + bias[None]).reshape(NSA_HPG * tq, nblk * tq)
        p = jnp.exp(s - jnp.max(s, axis=-1, keepdims=True))
        v = jnp.concatenate([r[g] for r in v_refs], axis=0)
        pv = jnp.dot(p.astype(BF16), v, preferred_element_type=F32)
        o = pv / pltpu.roll(pv, 64, 1)
        for h in range(NSA_HPG):
            c = (g * NSA_HPG + h) * 64
            o_ref[:, c:c + 64] = o[h * tq:(h + 1) * tq, 0:64].astype(o_ref.dtype)


def _win_attn(q_r, kt, v, b, s, tq):
    m = b * s
    nq = s // tq
    nblk = WINDOW // tq + 1
    kidx = lambda qi, j: jnp.maximum(qi - (nblk - 1) + j, 0)
    k_specs = [pl.BlockSpec((None, 128, tq), functools.partial(lambda bb, qi, j: (bb, 0, kidx(qi, j)), j=j))
               for j in range(nblk)]
    v_specs = [pl.BlockSpec((NSA_GROUPS, tq, 128), functools.partial(lambda bb, qi, j: (0, bb * nq + kidx(qi, j), 0), j=j))
               for j in range(nblk)]
    kern = functools.partial(_win_attn_kernel, tq=tq, nblk=nblk)
    return pl.pallas_call(
        kern,
        grid=(b, nq),
        in_specs=[pl.BlockSpec((tq, 512), lambda bb, qi: (bb * nq + qi, 0))] + k_specs + v_specs,
        out_specs=pl.BlockSpec((tq, 512), lambda bb, qi: (bb * nq + qi, 0)),
        out_shape=jax.ShapeDtypeStruct((m, 512), BF16),
        compiler_params=_cp("parallel", "parallel"),
        name="nsa_win_attn",
    )(q_r, *([kt] * nblk), *([v] * nblk))


def _consts_recurrent():
    bd_mean = np.kron(np.eye(4), np.full((64, 64), 1.0 / 64))
    bd_ones = np.kron(np.eye(4), np.ones((64, 64)))
    return jnp.asarray(bd_mean, BF16), jnp.asarray(bd_ones, BF16)


def _group_rmsnorm(o, bd_mean, g):
    ms = _dot_ls(o * o, bd_mean, 2)
    return o * lax.rsqrt(ms + EPS) * g


def _hgrn_kernel(q_ref, i_ref, g_ref, fh_ref, fl_ref, lbl_ref, on_ref, bdm_ref, bdo_ref, selp_ref,
                 o_ref, st_ref, z_ref, *, layer, t):
    @pl.when(pl.program_id(1) == 0)
    def _():
        st_ref[...] = jnp.zeros(st_ref.shape, F32)

    lg = lbl_ref[...]
    mx = jnp.max(lg, axis=0, keepdims=True)
    ex = jnp.exp(lg - mx)
    pr = ex / jnp.sum(ex, axis=0, keepdims=True)
    cs = pr[0:1, :]
    for r in range(1, layer + 1):
        cs = cs + pr[r:r + 1, :]
    lb = cs - pr[0:1, :]

    q = _silu(q_ref[...].astype(F32))
    fz = fh_ref[...].astype(F32) + fl_ref[...].astype(F32)
    f = lb + (1.0 - lb) * _sigmoid(fz)
    logf = jnp.log(jnp.maximum(f, TINY))
    k = (1.0 - lb) * _sigmoid(-fz)
    v = i_ref[...].astype(F32)
    b = _chunk_cumsum(logf)

    bdm = _bd_mask()
    bdk = bdo_ref[...] > 0
    nsub = t // SUB
    tl = lax.broadcasted_iota(jnp.int32, (1, SUB, 1), 1)
    for hp in range(2):
        ls = slice(hp * 128, (hp + 1) * 128)
        q3 = q[:, ls].reshape(nsub, SUB, 128)
        k3 = k[:, ls].reshape(nsub, SUB, 128)
        b3 = b[:, ls].reshape(nsub, SUB, 128)
        for sl in range(SUB):
            e = jnp.exp(jnp.where(tl >= sl, b3 - b3[:, sl:sl + 1, :], NEG_BIG))
            zz = q3 * k3[:, sl:sl + 1, :] * e
            z_ref[hp * t:(hp + 1) * t, sl * 128:(sl + 1) * 128] = zz.reshape(t, 128)
    a_pairs = _dot_ls(z_ref[...], selp_ref[...], 2)
    a_all = jnp.concatenate([a_pairs[0:t], a_pairs[t:2 * t]], axis=1)
    trow = lax.broadcasted_iota(jnp.int32, (CHUNK, 256), 0)
    scol = lax.broadcasted_iota(jnp.int32, (CHUNK, 256), 1) % CHUNK
    diag_mask = (trow // SUB) == (scol // SUB)
    m1 = (trow >= 32) & (scol < 32)
    m2 = ((trow >= 16) & (trow < 32) & (scol < 16)) | ((trow >= 48) & (scol >= 32) & (scol < 48))
    rowi = lax.broadcasted_iota(jnp.int32, (CHUNK, 1), 0)

    chunks = [slice(c * CHUNK, (c + 1) * CHUNK) for c in range(t // CHUNK)]
    lvl1, lvl2 = [], []
    for rs in chunks:
        qc, kc, bc = q[rs], k[rs], b[rs]
        r1 = bc[31:32, :]
        q1 = qc * jnp.exp(jnp.minimum(bc - r1, 0.0))
        k1 = jnp.where(rowi < 32, kc * jnp.exp(jnp.minimum(r1 - bc, 0.0)), 0.0)
        r2 = jnp.where(rowi < 32, bc[15:16, :], bc[47:48, :])
        q2 = qc * jnp.exp(jnp.minimum(bc - r2, 0.0))
        k2 = kc * jnp.exp(jnp.minimum(r2 - bc, 0.0))
        lvl1.append(_dot_nt(q1, _block_diag4(k1, bdk)))
        lvl2.append(_dot_nt(q2, _block_diag4(k2, bdk)))
    o_intra, s_add, s_dec, q_dec = [], [], [], []
    for rs, a1, a2 in zip(chunks, lvl1, lvl2):
        kc, vc, bc = k[rs], v[rs], b[rs]
        attn = jnp.where(diag_mask, a_all[rs], 0.0) + jnp.where(m1, a1, 0.0) + jnp.where(m2, a2, 0.0)
        o_intra.append(_dot(attn, _block_diag4(vc, bdk)))
        bl = bc[CHUNK - 1:CHUNK, :]
        kdt = (kc * jnp.exp(bl - bc)).T
        s_add.append(jnp.where(bdm, _dot(kdt, vc), 0.0))
        s_dec.append(_as_column(jnp.exp(bl)))
        q_dec.append((q[rs] * jnp.exp(bc)).astype(BF16))

    st = st_ref[...]
    outs = []
    for oi, sa, sd, qd in zip(o_intra, s_add, s_dec, q_dec):
        outs.append(oi + jnp.dot(qd, st.astype(BF16), preferred_element_type=F32))
        st = sd * st + sa
    st_ref[...] = st
    o = jnp.concatenate(outs, axis=0)
    o_ref[...] = (_group_rmsnorm(o, bdm_ref[...], on_ref[...]) * _sigmoid(g_ref[...].astype(F32))).astype(BF16)


def _hgrn(z, lb_logits, out_norm, layer, b, s, t):
    m = b * s
    nt = s // t
    depth = lb_logits.shape[0]
    bd_mean, bd_ones = _consts_recurrent()
    selp = np.zeros((SUB, 2, 64, 2, CHUNK), np.float32)
    for sl in range(SUB):
        for h2 in range(2):
            selp[sl, h2, :, h2, sl::SUB] = 1.0
    selp = jnp.asarray(selp.reshape(SUB * 128, 128), BF16)
    col, colf = COL_HG // 256, COL_HF // 256
    c2 = lambda bb, i: (0, 0)
    kern = functools.partial(_hgrn_kernel, layer=layer, t=t)
    return pl.pallas_call(
        kern,
        grid=(b, nt),
        in_specs=[pl.BlockSpec((t, 256), lambda bb, i: (bb * nt + i, col)),
                  pl.BlockSpec((t, 256), lambda bb, i: (bb * nt + i, col + 1)),
                  pl.BlockSpec((t, 256), lambda bb, i: (bb * nt + i, col + 2)),
                  pl.BlockSpec((t, 256), lambda bb, i: (bb * nt + i, colf)),
                  pl.BlockSpec((t, 256), lambda bb, i: (bb * nt + i, colf + 1)),
                  pl.BlockSpec((depth, 256), c2),
                  pl.BlockSpec((1, 256), c2),
                  pl.BlockSpec((256, 256), c2),
                  pl.BlockSpec((256, 256), c2),
                  pl.BlockSpec((SUB * 128, 128), c2)],
        out_specs=pl.BlockSpec((t, 256), lambda bb, i: (bb * nt + i, 0)),
        out_shape=jax.ShapeDtypeStruct((m, 256), BF16),
        scratch_shapes=[pltpu.VMEM((256, 256), F32), pltpu.VMEM((2 * t, SUB * 128), F32)],
        compiler_params=_cp("parallel", "arbitrary"),
        name="hgrn2",
    )(z, z, z, z, z, lb_logits, jnp.tile(out_norm, 4).reshape(1, 256), bd_mean, bd_ones, selp)


def _gdn_kernel(q_ref, k_ref, v_ref, z_ref, ba_ref, cw_ref, al_ref, dt_ref, on_ref, bdm_ref, bdo_ref,
                eb_ref, ea_ref, o_ref, st_ref, prev_ref, *, t):
    @pl.when(pl.program_id(1) == 0)
    def _():
        st_ref[...] = jnp.zeros(st_ref.shape, F32)
        prev_ref[...] = jnp.zeros(prev_ref.shape, F32)

    def conv_silu(x_ref, p):
        x = x_ref[...].astype(F32)
        xp = jnp.concatenate([prev_ref[:, p * 256:(p + 1) * 256], x], axis=0)
        w = cw_ref[:, p * 256:(p + 1) * 256]
        y = w[3:4, :] * xp[8:, :]
        for j in range(1, 4):
            y = y + w[3 - j:4 - j, :] * pltpu.roll(xp, j, 0)[8:, :]
        return _silu(y), x[t - 8:, :]

    qa, qtail = conv_silu(q_ref, 0)
    ka, ktail = conv_silu(k_ref, 1)
    va, vtail = conv_silu(v_ref, 2)
    prev_ref[:, 0:256] = qtail
    prev_ref[:, 256:512] = ktail
    prev_ref[:, 512:768] = vtail

    bdo = bdo_ref[...]
    bdk = bdo > 0
    ss = _dot_ls(jnp.concatenate([qa * qa, ka * ka], axis=0), bdo, 2)
    q = qa * lax.rsqrt(ss[0:t] + EPS) * (HEAD_DIM ** -0.5)
    k = ka * lax.rsqrt(ss[t:2 * t] + EPS)
    v = va

    gl = ba_ref[...].astype(F32)
    beta = _dot_ls(_sigmoid(gl), eb_ref[...], 2)
    xs = gl + dt_ref[...]
    softplus = jnp.maximum(xs, 0.0) + jnp.log(1.0 + jnp.exp(-jnp.abs(xs)))
    b = _dot_ls(_chunk_cumsum(-jnp.exp(al_ref[...]) * softplus), ea_ref[...], 3)

    bdm = _bd_mask()
    trow = lax.broadcasted_iota(jnp.int32, (CHUNK, 256), 0)
    scol = lax.broadcasted_iota(jnp.int32, (CHUNK, 256), 1) % CHUNK
    incl = scol <= trow
    strict = scol < trow
    eye = jnp.where(scol == trow, 1.0, 0.0)

    chunks = [slice(c * CHUNK, (c + 1) * CHUNK) for c in range(t // CHUNK)]
    nch = len(chunks)
    aw, aq = [], []
    for rs in chunks:
        qc, kc, bc = q[rs], k[rs], b[rs]
        brow = jnp.concatenate([bc[:, h * 64:(h + 1) * 64].T for h in range(N_HEADS4)], axis=1)
        e = jnp.exp(jnp.where(incl, bc - brow, 0.0))
        gram = _dot_nt(jnp.concatenate([kc * beta[rs], qc], axis=0), _block_diag4(kc, bdk))
        aw.append(jnp.where(strict, gram[:CHUNK] * e, 0.0))
        aq.append(jnp.where(incl, gram[CHUNK:] * e, 0.0).astype(BF16))
    pw = [None] * nch
    tinv = [None] * nch
    for c in range(nch):
        bh, bl_ = _split(-aw[c], 2)
        r1 = _rows_dot([bh, bl_], _block_diag4(bh, bdk))
        pw[c] = r1[0] + r1[1] + jnp.dot(bh, _block_diag4(bl_, bdk), preferred_element_type=F32)
        tinv[c] = eye - aw[c]
    for c in range(nch):
        ph, pl_ = _split(pw[c], 2)
        th, tl_ = _split(tinv[c], 2)
        r1 = _rows_dot([ph, pl_, th, tl_], _block_diag4(ph, bdk))
        r2 = _rows_dot([ph, th], _block_diag4(pl_, bdk))
        pw[c] = r1[0] + r1[1] + r2[0]
        tinv[c] = tinv[c] + r1[2] + r1[3] + r2[1]
    for _ in range(3):
        for c in range(nch):
            ph = pw[c].astype(BF16)
            r1 = _rows_dot([ph, tinv[c].astype(BF16)], _block_diag4(ph, bdk))
            pw[c] = r1[0]
            tinv[c] = tinv[c] + r1[1]
    for c in range(nch):
        tinv[c] = tinv[c] + jnp.dot(tinv[c].astype(BF16), _block_diag4(pw[c], bdk), preferred_element_type=F32)
    u, w = [], []
    for c, rs in enumerate(chunks):
        t2 = _split(tinv[c], 2)
        kbe = k[rs] * beta[rs] * jnp.exp(b[rs])
        r1 = _rows_dot(t2, _block_diag4(v[rs] * beta[rs], bdk))
        r2 = _rows_dot(t2, _block_diag4(kbe, bdk))
        u.append(r1[0] + r1[1])
        w.append(r2[0] + r2[1])
    lhs, s_add, s_dec, o0 = [], [], [], []
    for c, rs in enumerate(chunks):
        bc = b[rs]
        bl = bc[CHUNK - 1:CHUNK, :]
        kdt = (k[rs] * jnp.exp(bl - bc)).T.astype(BF16)
        rn = jnp.dot(kdt, jnp.concatenate([w[c], u[c]], axis=1).astype(BF16), preferred_element_type=F32)
        qeff = q[rs] * jnp.exp(bc) - jnp.dot(aq[c], _block_diag4(w[c], bdk), preferred_element_type=F32)
        lhs.append(jnp.concatenate([qeff.astype(BF16), jnp.where(bdm, rn[:, 0:256], 0.0).astype(BF16)], axis=0))
        s_add.append(jnp.where(bdm, rn[:, 256:512], 0.0))
        s_dec.append(_as_column(jnp.exp(bl)))
        o0.append(jnp.dot(aq[c], _block_diag4(u[c], bdk), preferred_element_type=F32))

    st = st_ref[...]
    outs = []
    for c in range(nch):
        r = jnp.dot(lhs[c], st.astype(BF16), preferred_element_type=F32)
        outs.append(o0[c] + r[0:CHUNK])
        st = s_dec[c] * st - r[CHUNK:] + s_add[c]
    st_ref[...] = st
    o = jnp.concatenate(outs, axis=0)
    o_ref[...] = (_group_rmsnorm(o, bdm_ref[...], on_ref[...]) * _silu(z_ref[...].astype(F32))).astype(BF16)


def _gdn(z, conv_w, a_log, dt_bias, out_norm, b, s, t):
    m = b * s
    nt = s // t
    bd_mean, bd_ones = _consts_recurrent()
    eb = np.zeros((128, 256), np.float32)
    ea = np.zeros((128, 256), np.float32)
    for h in range(N_HEADS4):
        eb[h, h * 64:(h + 1) * 64] = 1.0
        ea[4 + h, h * 64:(h + 1) * 64] = 1.0
    al = jnp.zeros((1, 128), F32).at[0, 4:8].set(a_log)
    dt = jnp.zeros((1, 128), F32).at[0, 4:8].set(dt_bias)
    col = COL_GD // 256
    c2 = lambda bb, i: (0, 0)
    kern = functools.partial(_gdn_kernel, t=t)
    return pl.pallas_call(
        kern,
        grid=(b, nt),
        in_specs=[pl.BlockSpec((t, 256), lambda bb, i: (bb * nt + i, col)),
                  pl.BlockSpec((t, 256), lambda bb, i: (bb * nt + i, col + 1)),
                  pl.BlockSpec((t, 256), lambda bb, i: (bb * nt + i, col + 2)),
                  pl.BlockSpec((t, 256), lambda bb, i: (bb * nt + i, col + 3)),
                  pl.BlockSpec((t, 128), lambda bb, i: (bb * nt + i, COL_BA // 128)),
                  pl.BlockSpec((4, 768), c2),
                  pl.BlockSpec((1, 128), c2), pl.BlockSpec((1, 128), c2),
                  pl.BlockSpec((1, 256), c2),
                  pl.BlockSpec((256, 256), c2), pl.BlockSpec((256, 256), c2),
                  pl.BlockSpec((128, 256), c2), pl.BlockSpec((128, 256), c2)],
        out_specs=pl.BlockSpec((t, 256), lambda bb, i: (bb * nt + i, 0)),
        out_shape=jax.ShapeDtypeStruct((m, 256), BF16),
        scratch_shapes=[pltpu.VMEM((256, 256), F32), pltpu.VMEM((8, 768), F32)],
        compiler_params=_cp("parallel", "arbitrary"),
        name="gdn",
    )(z, z, z, z, z, conv_w, al, dt, jnp.tile(out_norm, 4).reshape(1, 256), bd_mean, bd_ones,
      jnp.asarray(eb, BF16), jnp.asarray(ea, BF16))


def _merge_xattn_kernel(x_ref, oc_ref, os_ref, ow_ref, ng_ref, yb_ref, yc_ref, ma_ref, mb_ref, mc_ref,
                        ex_ref, wa_ref, wb_ref, wc_ref, wmix_ref, g_ref, wq_ref, qn_ref, kn_ref, mkv_ref, wo_ref,
                        o_ref):
    sg = _sigmoid(ng_ref[...].astype(F32))
    ya = (_dot_ls(sg, ex_ref[0], 2) * oc_ref[...].astype(F32) + _dot_ls(sg, ex_ref[1], 2) * os_ref[...].astype(F32)
          + _dot_ls(sg, ex_ref[2], 2) * ow_ref[...].astype(F32))
    merged = (_sigmoid(ma_ref[...].astype(F32)) * _dot(ya, wa_ref[...])
              + _sigmoid(mb_ref[...].astype(F32)) * jnp.dot(yb_ref[...], wb_ref[...], preferred_element_type=F32)
              + _sigmoid(mc_ref[...].astype(F32)) * jnp.dot(yc_ref[...], wc_ref[...], preferred_element_type=F32))
    x = x_ref[...] + _dot(merged, wmix_ref[...])

    h = x * lax.rsqrt(jnp.mean(x * x, axis=-1, keepdims=True) + EPS) * g_ref[...]
    q = _dot(h, wq_ref[...])
    scores = []
    for hd in range(N_HEADS4):
        ls = slice(hd * 128, (hd + 1) * 128)
        qh = q[:, ls]
        qh = qh * lax.rsqrt(jnp.mean(qh * qh, axis=-1, keepdims=True) + EPS) * qn_ref[...]
        kh = mkv_ref[:, ls]
        kh = kh * lax.rsqrt(jnp.mean(kh * kh, axis=-1, keepdims=True) + EPS) * kn_ref[...]
        scores.append(_dot_nt(qh, kh) * (XATTN_HEAD_DIM ** -0.5))
    outs = []
    for hd, s in enumerate(scores):
        e = jnp.exp(s - jnp.max(s, axis=-1, keepdims=True))
        pv = _dot(e, mkv_ref[:, 512 + hd * 128:512 + (hd + 1) * 128])
        outs.append(pv / jnp.sum(e, axis=-1, keepdims=True))
    o = jnp.concatenate(outs, axis=1)
    o_ref[...] = x + _dot(o, wo_ref[...])


def _merge_xattn(x2, o_cmp, o_sel, o_win, z, yb, yc, wa, wb, wc, wmix, g, wq, qn, kn, mkv, wo, layer, s, tm):
    m, d = x2.shape
    wl = lambda i: (layer, 0, 0)
    nt = s // tm
    ex = np.zeros((3, 128, 512), np.float32)
    for c in range(3):
        for h in range(NSA_HEADS):
            ex[c, c * NSA_HEADS + h, h * 64:(h + 1) * 64] = 1.0
    row = lambda i: (i, 0)
    c2 = lambda i: (0, 0)
    once = dict(pipeline_mode=pl.Buffered(1))
    return pl.pallas_call(
        _merge_xattn_kernel,
        grid=(m // tm,),
        in_specs=[pl.BlockSpec((tm, d), row),
                  pl.BlockSpec((tm, 512), row), pl.BlockSpec((tm, 512), row), pl.BlockSpec((tm, 512), row),
                  pl.BlockSpec((tm, 128), lambda i: (i, COL_NG // 128)),
                  pl.BlockSpec((tm, 256), row), pl.BlockSpec((tm, 256), row),
                  pl.BlockSpec((tm, 1024), lambda i: (i, 0)),
                  pl.BlockSpec((tm, 1024), lambda i: (i, 1)),
                  pl.BlockSpec((tm, 1024), lambda i: (i, 2)),
                  pl.BlockSpec((3, 128, 512), lambda i: (0, 0, 0)),
                  pl.BlockSpec((None, 512, d), wl, **once), pl.BlockSpec((None, 256, d), wl, **once),
                  pl.BlockSpec((None, 256, d), wl, **once),
                  pl.BlockSpec((None, d, d), wl, **once),
                  pl.BlockSpec((1, d), c2),
                  pl.BlockSpec((None, d, 512), wl, **once),
                  pl.BlockSpec((1, 128), c2), pl.BlockSpec((1, 128), c2),
                  pl.BlockSpec((N_MEM, 1024), lambda i: (i // nt, 0)),
                  pl.BlockSpec((None, 512, d), wl, **once)],
        out_specs=pl.BlockSpec((tm, d), row),
        out_shape=jax.ShapeDtypeStruct((m, d), F32),
        compiler_params=_cp("parallel"),
        name="merge_xattn",
    )(x2, o_cmp, o_sel, o_win, z, yb, yc, z, z, z, jnp.asarray(ex, BF16), wa, wb, wc, wmix,
      g.reshape(1, d), wq, qn.reshape(1, 128), kn.reshape(1, 128), mkv, wo)


FFN_HALO = 16


def _ffn_kernel(x_ref, xh_ref, g_ref, wua_ref, wub_ref, cwa_ref, cwb_ref, wd_ref, o_ref, h_ref, acc_ref, *, nt, tm):
    i = pl.program_id(0)
    f = pl.program_id(1)

    @pl.when(f == 0)
    def _():
        def norm(x):
            return x * lax.rsqrt(jnp.mean(x * x, axis=-1, keepdims=True) + EPS) * g_ref[...]
        first = (i % nt) == 0
        h_ref[0:FFN_HALO, :] = jnp.where(first, 0.0, norm(xh_ref[...])).astype(BF16)
        h_ref[FFN_HALO:, :] = norm(x_ref[...]).astype(BF16)
        acc_ref[...] = jnp.zeros(acc_ref.shape, F32)

    h = h_ref[...]

    def up_conv(w_ref, cw_ref):
        u = jnp.dot(h, w_ref[...], preferred_element_type=F32)
        cw = cw_ref[...]
        y = cw[2:3, :] * u + cw[1:2, :] * pltpu.roll(u, 1, 0) + cw[0:1, :] * pltpu.roll(u, 2, 0)
        return y[FFN_HALO:, :]

    a = up_conv(wua_ref, cwa_ref)
    bb = up_conv(wub_ref, cwb_ref)
    acc_ref[...] += _dot(_silu(a) * bb, wd_ref[...])

    @pl.when(f == pl.num_programs(1) - 1)
    def _():
        o_ref[...] = x_ref[...] + acc_ref[...]


def _ffn(x2, g, w_up, conv_w, w_down, layer, s, tm, tf):
    m, d = x2.shape
    nt = s // tm
    nf = D_FF // tf
    hb = tm // FFN_HALO
    kern = functools.partial(_ffn_kernel, nt=nt, tm=tm)
    return pl.pallas_call(
        kern,
        grid=(m // tm, nf),
        in_specs=[pl.BlockSpec((tm, d), lambda i, f: (i, 0)),
                  pl.BlockSpec((FFN_HALO, d), lambda i, f: (jnp.maximum(i * hb - 1, 0), 0)),
                  pl.BlockSpec((1, d), lambda i, f: (0, 0)),
                  pl.BlockSpec((None, d, tf), lambda i, f: (layer, 0, f)),
                  pl.BlockSpec((None, d, tf), lambda i, f: (layer, 0, nf + f)),
                  pl.BlockSpec((None, 3, tf), lambda i, f: (layer, 0, f)),
                  pl.BlockSpec((None, 3, tf), lambda i, f: (layer, 0, nf + f)),
                  pl.BlockSpec((None, tf, d), lambda i, f: (layer, f, 0))],
        out_specs=pl.BlockSpec((tm, d), lambda i, f: (i, 0)),
        out_shape=jax.ShapeDtypeStruct((m, d), F32),
        scratch_shapes=[pltpu.VMEM((tm + FFN_HALO, d), BF16), pltpu.VMEM((tm, d), F32)],
        compiler_params=_cp("parallel", "arbitrary"),
        name="conv_glu_ffn",
    )(x2, x2, g.reshape(1, d), w_up, w_up, conv_w, conv_w, w_down)


def _permute_w_in(w):
    nl, d = w.shape[0], w.shape[1]
    gate = w[:, :, 1280:1304].reshape(nl, d, NSA_HEADS, 3).transpose(0, 1, 3, 2).reshape(nl, d, 24)
    pieces = [(COL_M, w[:, :, 3360:6432]), (COL_NQ, w[:, :, 0:1280]), (COL_NG, gate), (COL_BA, w[:, :, 3352:3360]),
              (COL_HG, w[:, :, 1304:1560]), (COL_HG + 256, w[:, :, 1816:2328]), (COL_GD, w[:, :, 2328:3352]),
              (COL_HF, w[:, :, 1560:1816])]
    out = jnp.zeros((nl, d, IN_WIDTH_P), BF16)
    for dst, src in pieces:
        out = lax.dynamic_update_slice(out, src.astype(BF16), (0, 0, dst))
    return out


def _rope_tables(s):
    inv_freq = (1.0 / (ROPE_THETA ** (np.arange(0, HEAD_DIM, 2, dtype=np.float32) / HEAD_DIM))).astype(np.float32)
    ang = np.arange(s, dtype=np.float32)[:, None] * inv_freq[None, :]
    c, sn = np.cos(ang), np.sin(ang)
    return (jnp.asarray(np.tile(np.concatenate([c, c], axis=1), (1, 2)), F32),
            jnp.asarray(np.tile(np.concatenate([-sn, sn], axis=1), (1, 2)), F32))


def kernel(x, mem, mem_norm, mem_w_kv, hgrn_lb_logits, norm_mix, w_in, nsa_q_norm, nsa_k_norm, cmp_pos_k, cmp_pos_v, cmp_k_w1, cmp_k_w2, cmp_v_w1, cmp_v_w2, hgrn_out_norm, gdn_conv, gdn_a_log, gdn_dt_bias, gdn_out_norm, w_branch_a, w_branch_b, w_branch_c, w_mix_out, norm_cross, xattn_wq, xattn_q_norm, xattn_k_norm, xattn_wo, norm_ffn, ffn_w_up, ffn_conv, ffn_w_down):
    b, s, d = x.shape
    m = b * s
    depth = w_in.shape[0]
    cos_t, sin_t = _rope_tables(s)
    x2 = x.reshape(m, d)
    mkv = _norm_matmul(mem.reshape(b * N_MEM, d), mem_norm, mem_w_kv.astype(BF16)[None], 0, N_MEM, 512)

    tm_in = min(1024, m)
    w_in_p = _permute_w_in(w_in)
    w_up, w_down = ffn_w_up.astype(BF16), ffn_w_down.astype(BF16)
    w_br_a, w_br_b, w_br_c = w_branch_a.astype(BF16), w_branch_b.astype(BF16), w_branch_c.astype(BF16)
    w_mix, w_xq, w_xo = w_mix_out.astype(BF16), xattn_wq.astype(BF16), xattn_wo.astype(BF16)
    for l in range(depth):
        z = _norm_matmul(x2, norm_mix[l], w_in_p, l, tm_in, IN_WIDTH_P // 3, BF16, split_last=256)
        q_r, rk, rv, kst, kwt, vs, vw = _nsa_prep(z, cos_t, sin_t, nsa_q_norm[l], nsa_k_norm[l], b, s, 512)
        kct, vc = _compress(rk, rv, cmp_pos_k[l], cmp_pos_v[l], cmp_k_w1[l], cmp_k_w2[l], cmp_v_w1[l], cmp_v_w2[l])
        o_cmp, sel = _cmp_attn(q_r, kct, vc, b, s, 256)
        o_sel = _sel_attn(q_r, kst, vs, sel, b, s, 512, 512)
        o_win = _win_attn(q_r, kwt, vw, b, s, 256)
        yb = _hgrn(z, hgrn_lb_logits, hgrn_out_norm[l], l, b, s, 512)
        yc = _gdn(z, gdn_conv[l], gdn_a_log[l], gdn_dt_bias[l], gdn_out_norm[l], b, s, 512)
        x2 = _merge_xattn(x2, o_cmp, o_sel, o_win, z, yb, yc, w_br_a, w_br_b, w_br_c, w_mix,
                          norm_cross[l], w_xq, xattn_q_norm[l], xattn_k_norm[l], mkv, w_xo, l, s, 512)
        x2 = _ffn(x2, norm_ffn[l], w_up, ffn_conv, w_down, l, s, 512, 1408)
    return x2.reshape(b, s, d)
```
